```python
import jax
import jax.numpy as jnp
from jax import lax
import numpy as np

D_MODEL = 4096
BATCH = 4
SEQ = 2048
DEPTH = 1
DEC_BATCH = 128
DEC_SEQ = 1
PAST_LEN = 2048
PAGE_SIZE = 128

HEAD_DIM = 128
ATT_WIDTH = D_MODEL // 2
ATT_HEADS = ATT_WIDTH // HEAD_DIM
ATT_KV_HEADS = ATT_HEADS // 2
ATT_GROUP = ATT_HEADS // ATT_KV_HEADS
ATT_SCALE = HEAD_DIM ** -0.5
MOBA_BLOCK = 256
MOBA_TOPK = 3
Q_CHUNK = 16
M_WIDTH = D_MODEL - ATT_WIDTH
M_HEADS = 4
M_V = M_WIDTH // M_HEADS
M_QK = M_V // 2
M_CHUNK = 64
FORGET_BIAS = 3.0
N_GROUPS = 4
EXPERTS_PER_GROUP = 8
N_EXPERTS = N_GROUPS * EXPERTS_PER_GROUP
TOPK_IN_GROUP = 2
D_EXPERT = D_MODEL // 8
LN_EPS = 1e-5
NORM_EPS = 1e-6
DEEPNORM_ALPHA = (2 * DEPTH) ** 0.25
DEEPNORM_BETA = (8 * DEPTH) ** -0.25
ATT_Q_W = ATT_HEADS * HEAD_DIM
ATT_KV_W = ATT_KV_HEADS * HEAD_DIM
M_QK_W = M_HEADS * M_QK
M_V_W = M_HEADS * M_V
IN_WIDTH = ATT_Q_W + 2 * ATT_KV_W + 2 * M_QK_W + 2 * M_V_W + 2 * M_HEADS
SPLIT_POINTS = (
    ATT_Q_W,
    ATT_Q_W + ATT_KV_W,
    ATT_Q_W + 2 * ATT_KV_W,
    ATT_Q_W + 2 * ATT_KV_W + M_QK_W,
    ATT_Q_W + 2 * ATT_KV_W + 2 * M_QK_W,
    ATT_Q_W + 2 * ATT_KV_W + 2 * M_QK_W + M_V_W,
    ATT_Q_W + 2 * ATT_KV_W + 2 * M_QK_W + 2 * M_V_W,
)

kernel_name = 'hymba_moba_mlstm_hmoe_deepnorm_step'


def _layernorm(x, g, b):
    xf = x.astype(jnp.float32)
    mu = jnp.mean(xf, -1, keepdims=True)
    xc = xf - mu
    var = jnp.mean(jnp.square(xc), -1, keepdims=True)
    y = xc * lax.rsqrt(var + LN_EPS) * g.astype(jnp.float32) + b.astype(jnp.float32)
    return y.astype(x.dtype)


def _in_proj(x, w_in, b_gates):
    bsz, seq, _ = x.shape
    z = jnp.einsum('bld,de->ble', x, w_in)
    q_a, k_a, v_a, q_m, k_m, v_m, o_m, gates = jnp.split(z, SPLIT_POINTS, axis=-1)
    q_a = q_a.reshape(bsz, seq, ATT_HEADS, HEAD_DIM)
    k_a = k_a.reshape(bsz, seq, ATT_KV_HEADS, HEAD_DIM)
    v_a = v_a.reshape(bsz, seq, ATT_KV_HEADS, HEAD_DIM)

    def heads(t, d):
        return jnp.transpose(t.reshape(bsz, seq, M_HEADS, d), (0, 2, 1, 3)).astype(jnp.float32)

    q_m = heads(q_m, M_QK) * (M_QK ** -0.5)
    k_m = heads(k_m, M_QK)
    v_m = heads(v_m, M_V)
    g = gates.astype(jnp.float32) + b_gates.astype(jnp.float32)
    g = jnp.transpose(g.reshape(bsz, seq, 2, M_HEADS), (2, 0, 3, 1))
    ig = g[0]
    lf = jax.nn.log_sigmoid(g[1])
    return q_a, k_a, v_a, q_m, k_m, v_m, ig, lf, o_m


def _to_blocks(parts):
    total = sum(p.shape[1] for p in parts)
    nb = -(-total // MOBA_BLOCK)
    pad = nb * MOBA_BLOCK - total
    bsz, _, nh, d = parts[0].shape
    full = jnp.concatenate(list(parts) + [jnp.zeros((bsz, pad, nh, d), parts[0].dtype)], axis=1)
    return full.reshape(bsz, nb, MOBA_BLOCK, nh, d)


def _moba_chunk(qc, qpos, k_blk, v_blk, k_mean, tk):
    bsz, kvh = qc.shape[0], qc.shape[1]
    nb = k_blk.shape[1]
    qblk = qpos // MOBA_BLOCK
    gate = jnp.einsum('bhgqd,bnhd->bhgqn', qc.astype(jnp.float32), k_mean)
    fully_past = jnp.arange(nb)[None, :] < qblk[:, None]
    gate = jnp.where(fully_past, gate, -jnp.inf)
    _, sel = lax.top_k(gate, tk)
    sel_valid = sel < qblk[:, None]
    own = jnp.broadcast_to(qblk[:, None], sel.shape[:-1] + (1,)).astype(sel.dtype)
    idx = jnp.concatenate([sel, own], axis=-1)
    valid = jnp.concatenate([sel_valid, jnp.ones(own.shape, bool)], axis=-1)
    bi = jnp.arange(bsz)[:, None, None, None, None]
    hi = jnp.arange(kvh)[None, :, None, None, None]
    k_sel = k_blk[bi, idx, :, hi]
    v_sel = v_blk[bi, idx, :, hi]
    kpos = idx[..., None] * MOBA_BLOCK + jnp.arange(MOBA_BLOCK)
    mask = valid[..., None] & (kpos <= qpos[:, None, None])
    s = jnp.einsum('bhgqd,bhgqskd->bhgqsk', qc, k_sel).astype(jnp.float32) * ATT_SCALE
    s = jnp.where(mask, s, -jnp.inf)
    n_slots = idx.shape[-1]
    p = jax.nn.softmax(s.reshape(s.shape[:4] + (n_slots * MOBA_BLOCK,)), axis=-1).reshape(s.shape)
    return jnp.einsum('bhgqsk,bhgqskd->bhgqd', p.astype(v_sel.dtype), v_sel)


def moba_attention(q, k_parts, v_parts, q_start):
    bsz, nq, _, dh = q.shape
    k_blk = _to_blocks(k_parts)
    v_blk = _to_blocks(v_parts)
    nb = k_blk.shape[1]
    k_mean = jnp.mean(k_blk.astype(jnp.float32), axis=2)
    tk = min(MOBA_TOPK, nb)
    qg = jnp.transpose(q.reshape(bsz, nq, ATT_KV_HEADS, ATT_GROUP, dh), (0, 2, 3, 1, 4))
    pos = q_start + jnp.arange(nq, dtype=jnp.int32)
    if nq > Q_CHUNK and nq % Q_CHUNK == 0:
        nc = nq // Q_CHUNK
        qs = jnp.moveaxis(qg.reshape(bsz, ATT_KV_HEADS, ATT_GROUP, nc, Q_CHUNK, dh), 3, 0)
        ps = pos.reshape(nc, Q_CHUNK)
        out = lax.map(lambda a: _moba_chunk(a[0], a[1], k_blk, v_blk, k_mean, tk), (qs, ps))
        out = jnp.moveaxis(out, 0, 3).reshape(bsz, ATT_KV_HEADS, ATT_GROUP, nq, dh)
    else:
        out = _moba_chunk(qg, pos, k_blk, v_blk, k_mean, tk)
    return jnp.transpose(out, (0, 3, 1, 2, 4)).reshape(bsz, nq, ATT_HEADS * dh)


def _mlstm_chunk(carry, inputs):
    c0, n0, m0 = carry
    q, k, v, ig, lf = inputs
    L = q.shape[2]
    b = jnp.cumsum(lf, axis=-1)
    m = b + jnp.maximum(m0[..., None], lax.cummax(ig - b, axis=2))
    causal = jnp.tril(jnp.ones((L, L), dtype=bool))
    log_d = b[..., :, None] - b[..., None, :] + ig[..., None, :] - m[..., :, None]
    d = jnp.exp(jnp.where(causal, log_d, -jnp.inf))
    s = jnp.einsum('bhtd,bhsd->bhts', q, k) * d
    inter = jnp.exp(b + m0[..., None] - m)
    num = inter[..., None] * jnp.einsum('bhtd,bhde->bhte', q, c0) + jnp.einsum('bhts,bhse->bhte', s, v)
    den = inter * jnp.einsum('bhtd,bhd->bht', q, n0) + jnp.sum(s, axis=-1)
    h = num / jnp.maximum(jnp.abs(den), jnp.exp(-m))[..., None]
    b_last = b[..., -1]
    m_last = m[..., -1]
    w = jnp.exp(b_last[..., None] - b + ig - m_last[..., None])
    decay = jnp.exp(b_last + m0 - m_last)
    c = decay[..., None, None] * c0 + jnp.einsum('bhs,bhsd,bhse->bhde', w, k, v)
    n = decay[..., None] * n0 + jnp.einsum('bhs,bhsd->bhd', w, k)
    return (c, n, m_last), h


def _mlstm(q, k, v, ig, lf, c0, n0, m0):
    L = q.shape[2]
    if L > M_CHUNK and L % M_CHUNK == 0:
        nc = L // M_CHUNK

        def split(t):
            return jnp.moveaxis(t.reshape(t.shape[:2] + (nc, M_CHUNK) + t.shape[3:]), 2, 0)

        (c, n, m), h = lax.scan(_mlstm_chunk, (c0, n0, m0), (split(q), split(k), split(v), split(ig), split(lf)))
        h = jnp.moveaxis(h, 0, 2).reshape(q.shape[:3] + (v.shape[-1],))
    else:
        (c, n, m), h = _mlstm_chunk((c0, n0, m0), (q, k, v, ig, lf))
    return h, c, n, m


def _mlstm_readout(h, o_gate, g):
    h = h * lax.rsqrt(jnp.mean(jnp.square(h), -1, keepdims=True) + NORM_EPS)
    bsz, nh, L, dv = h.shape
    h = jnp.transpose(h, (0, 2, 1, 3)).reshape(bsz, L, nh * dv)
    return (h * g.astype(jnp.float32) * jax.nn.sigmoid(o_gate.astype(jnp.float32))).astype(o_gate.dtype)


def _hier_moe(x, w_group, b_group, w_router, b_router, w_e_gate, w_e_up, w_e_down):
    n = x.shape[0]
    rows = jnp.arange(n)
    xf = x.astype(jnp.float32)
    g_logits = xf @ w_group.astype(jnp.float32) + b_group.astype(jnp.float32)
    g_sel = jnp.argmax(g_logits, axis=-1)
    g_w = jax.nn.softmax(g_logits, axis=-1)[rows, g_sel][:, None]
    e_logits = (xf @ w_router.astype(jnp.float32) + b_router.astype(jnp.float32)).reshape(n, N_GROUPS, EXPERTS_PER_GROUP)
    e_in = e_logits[rows, g_sel]
    top_v, top_i = lax.top_k(e_in, TOPK_IN_GROUP)
    top_w = jax.nn.softmax(top_v, axis=-1) * g_w
    e_idx = g_sel[:, None] * EXPERTS_PER_GROUP + top_i
    gates = jnp.sum(jax.nn.one_hot(e_idx, N_EXPERTS, dtype=jnp.float32) * top_w[..., None], axis=1)
    hg = jnp.einsum('nd,edf->nef', x, w_e_gate)
    hu = jnp.einsum('nd,edf->nef', x, w_e_up)
    a = jax.nn.silu(hg) * hu * gates[..., None].astype(hg.dtype)
    return jnp.einsum('nef,efd->nd', a, w_e_down)


def _merge_and_ffn(x, att, mix_m, w_out, ln1_g, ln1_b, w_group, b_group, w_router, b_router,
                   w_e_gate, w_e_up, w_e_down, ln2_g, ln2_b):
    bsz, L, d = x.shape
    mixed = jnp.einsum('ble,ed->bld', jnp.concatenate([att, mix_m], axis=-1), w_out)
    h = _layernorm(DEEPNORM_ALPHA * x + mixed, ln1_g, ln1_b)
    ff = _hier_moe(h.reshape(bsz * L, d), w_group, b_group, w_router, b_router,
                   w_e_gate, w_e_up, w_e_down).reshape(bsz, L, d)
    return _layernorm(DEEPNORM_ALPHA * h + ff, ln2_g, ln2_b)


def setup_inputs(seed: int = 0) -> dict:
    key = jax.random.key(seed)
    ks = jax.random.split(key, 24)
    f32 = jnp.float32
    n_pages = PAST_LEN // PAGE_SIZE
    n_pool = (5 * DEC_BATCH * n_pages) // 4

    def nrm(k, shape, s):
        return s * jax.random.normal(k, shape, f32)

    x_prompt = nrm(ks[0], (BATCH, SEQ, D_MODEL), 1.0)
    x_sample = nrm(ks[1], (DEC_BATCH, DEC_SEQ, D_MODEL), 1.0)
    cache_k = nrm(ks[2], (DEPTH, n_pool, PAGE_SIZE, ATT_KV_HEADS, HEAD_DIM), 1.0)
    cache_v = nrm(ks[3], (DEPTH, n_pool, PAGE_SIZE, ATT_KV_HEADS, HEAD_DIM), 1.0)
    state_mlstm_c = nrm(ks[4], (DEPTH, DEC_BATCH, M_HEADS, M_QK, M_V), 0.5)
    state_mlstm_n = nrm(ks[5], (DEPTH, DEC_BATCH, M_HEADS, M_QK), 0.5)
    state_mlstm_m = nrm(ks[6], (DEPTH, DEC_BATCH, M_HEADS), 0.5)
    page_table = jax.random.permutation(ks[7], n_pool)[: DEC_BATCH * n_pages].reshape(DEC_BATCH, n_pages).astype(jnp.int32)
    col_scale = jnp.concatenate([
        jnp.ones((ATT_Q_W + ATT_KV_W,), f32),
        jnp.full((ATT_KV_W,), DEEPNORM_BETA, f32),
        jnp.ones((2 * M_QK_W,), f32),
        jnp.full((M_V_W,), DEEPNORM_BETA, f32),
        jnp.ones((M_V_W + 2 * M_HEADS,), f32),
    ])
    w_in = nrm(ks[8], (DEPTH, D_MODEL, IN_WIDTH), D_MODEL ** -0.5) * col_scale
    b_gates = jnp.concatenate([nrm(ks[9], (DEPTH, M_HEADS), 0.1),
                               FORGET_BIAS + nrm(ks[10], (DEPTH, M_HEADS), 0.5)], axis=-1)
    mlstm_norm_g = 1.0 + nrm(ks[11], (DEPTH, M_V_W), 0.02)
    w_out = nrm(ks[12], (DEPTH, D_MODEL, D_MODEL), D_MODEL ** -0.5 * DEEPNORM_BETA)
    ln1_g = 1.0 + nrm(ks[13], (DEPTH, D_MODEL), 0.02)
    ln1_b = nrm(ks[14], (DEPTH, D_MODEL), 0.02)
    w_group = nrm(ks[15], (DEPTH, D_MODEL, N_GROUPS), D_MODEL ** -0.5)
    b_group = nrm(ks[16], (DEPTH, N_GROUPS), 0.01)
    w_router = nrm(ks[17], (DEPTH, D_MODEL, N_EXPERTS), D_MODEL ** -0.5)
    b_router = nrm(ks[18], (DEPTH, N_EXPERTS), 0.01)
    w_e_gate = nrm(ks[19], (DEPTH, N_EXPERTS, D_MODEL, D_EXPERT), D_MODEL ** -0.5 * DEEPNORM_BETA)
    w_e_up = nrm(ks[20], (DEPTH, N_EXPERTS, D_MODEL, D_EXPERT), D_MODEL ** -0.5 * DEEPNORM_BETA)
    w_e_down = nrm(ks[21], (DEPTH, N_EXPERTS, D_EXPERT, D_MODEL), D_EXPERT ** -0.5 * DEEPNORM_BETA)
    ln2_g = 1.0 + nrm(ks[22], (DEPTH, D_MODEL), 0.02)
    ln2_b = nrm(ks[23], (DEPTH, D_MODEL), 0.02)
    return {
        'x_prompt': x_prompt, 'x_sample': x_sample,
        'cache_k': cache_k, 'cache_v': cache_v,
        'state_mlstm_c': state_mlstm_c, 'state_mlstm_n': state_mlstm_n, 'state_mlstm_m': state_mlstm_m,
        'page_table': page_table,
        'w_in': w_in, 'b_gates': b_gates, 'mlstm_norm_g': mlstm_norm_g, 'w_out': w_out,
        'ln1_g': ln1_g, 'ln1_b': ln1_b,
        'w_group': w_group, 'b_group': b_group, 'w_router': w_router, 'b_router': b_router,
        'w_e_gate': w_e_gate, 'w_e_up': w_e_up, 'w_e_down': w_e_down,
        'ln2_g': ln2_g, 'ln2_b': ln2_b,
    }


def reference(x_prompt, x_sample, cache_k, cache_v, state_mlstm_c, state_mlstm_n, state_mlstm_m, page_table,
              w_in, b_gates, mlstm_norm_g, w_out, ln1_g, ln1_b, w_group, b_group, w_router, b_router,
              w_e_gate, w_e_up, w_e_down, ln2_g, ln2_b):
    f32 = jnp.float32
    dec_b, n_pages = page_table.shape
    past_len = n_pages * cache_k.shape[2]
    bsz = x_prompt.shape[0]
    xp = x_prompt
    xs = x_sample
    p_k, p_v, s_k, s_v = [], [], [], []
    p_c, p_n, p_m, s_c, s_n, s_m = [], [], [], [], [], []
    for l in range(DEPTH):
        ffn = (w_out[l], ln1_g[l], ln1_b[l], w_group[l], b_group[l], w_router[l], b_router[l],
               w_e_gate[l], w_e_up[l], w_e_down[l], ln2_g[l], ln2_b[l])
        q_a, k_a, v_a, q_m, k_m, v_m, ig, lf, o_m = _in_proj(xp, w_in[l], b_gates[l])
        att = moba_attention(q_a, [k_a], [v_a], 0)
        h_m, c_new, n_new, m_new = _mlstm(q_m, k_m, v_m, ig, lf,
                                          jnp.zeros((bsz, M_HEADS, M_QK, M_V), f32),
                                          jnp.zeros((bsz, M_HEADS, M_QK), f32),
                                          jnp.zeros((bsz, M_HEADS), f32))
        mix_m = _mlstm_readout(h_m, o_m, mlstm_norm_g[l])
        xp = _merge_and_ffn(xp, att, mix_m, *ffn)
        p_k.append(k_a)
        p_v.append(v_a)
        p_c.append(c_new)
        p_n.append(n_new)
        p_m.append(m_new)
        q_a, k_a, v_a, q_m, k_m, v_m, ig, lf, o_m = _in_proj(xs, w_in[l], b_gates[l])
        past_k = cache_k[l][page_table].reshape(dec_b, past_len, ATT_KV_HEADS, HEAD_DIM)
        past_v = cache_v[l][page_table].reshape(dec_b, past_len, ATT_KV_HEADS, HEAD_DIM)
        att = moba_attention(q_a, [past_k, k_a], [past_v, v_a], past_len)
        h_m, c_new, n_new, m_new = _mlstm(q_m, k_m, v_m, ig, lf,
                                          state_mlstm_c[l].astype(f32),
                                          state_mlstm_n[l].astype(f32),
                                          state_mlstm_m[l].astype(f32))
        mix_m = _mlstm_readout(h_m, o_m, mlstm_norm_g[l])
        xs = _merge_and_ffn(xs, att, mix_m, *ffn)
        s_k.append(k_a)
        s_v.append(v_a)
        s_c.append(c_new)
        s_n.append(n_new)
        s_m.append(m_new)
    y_prompt = xp
    y_sample = xs
    k_prompt = jnp.stack(p_k)
    v_prompt = jnp.stack(p_v)
    k_sample = jnp.stack(s_k)
    v_sample = jnp.stack(s_v)
    c_prompt = jnp.stack(p_c)
    n_prompt = jnp.stack(p_n)
    m_prompt = jnp.stack(p_m)
    c_sample = jnp.stack(s_c)
    n_sample = jnp.stack(s_n)
    m_sample = jnp.stack(s_m)
    return (y_prompt, y_sample, k_prompt, v_prompt, k_sample, v_sample,
            c_prompt, n_prompt, m_prompt, c_sample, n_sample, m_sample)
```

```python
import functools

import jax
import jax.numpy as jnp
from jax import lax
from jax.experimental import pallas as pl
from jax.experimental.pallas import tpu as pltpu

F32, BF16, I32 = jnp.float32, jnp.bfloat16, jnp.int32

SUBLANES = 8
LANES = 128
MIB = 1024 * 1024

HEAD_DIM = 128
ATT_HEADS = 16
ATT_KV_HEADS = 8
ATT_GROUP = ATT_HEADS // ATT_KV_HEADS
ATT_SCALE = HEAD_DIM ** -0.5
MOBA_BLOCK = 256
MOBA_TOPK = 3
M_HEADS = 4
M_QK = 256
M_V = 512
M_Q_SCALE = M_QK ** -0.5
N_GROUPS = 4
EXPERTS_PER_GROUP = 8
N_EXPERTS = N_GROUPS * EXPERTS_PER_GROUP
LN_EPS = 1e-5
NORM_EPS = 1e-6

ATT_Q_W = ATT_HEADS * HEAD_DIM
ATT_KV_W = ATT_KV_HEADS * HEAD_DIM
M_QK_W = M_HEADS * M_QK
M_V_W = M_HEADS * M_V
COL_Q = 0
COL_K = COL_Q + ATT_Q_W
COL_V = COL_K + ATT_KV_W
COL_QM = COL_V + ATT_KV_W
COL_KM = COL_QM + M_QK_W
COL_VM = COL_KM + M_QK_W
COL_OM = COL_VM + M_V_W
COL_G = COL_OM + M_V_W

D_MODEL = 4096
SLAB_ROWS = D_MODEL // LANES
MOE_TILE = 512
MOE_FCHUNK = 128
GATHER_WINDOW = 32


def _params(sem, vmem_mib):
    return pltpu.CompilerParams(dimension_semantics=sem, vmem_limit_bytes=vmem_mib * MIB)


def _dot(a, b):
    return jnp.dot(a, b, preferred_element_type=F32)


def _dot_nt(a, b):
    return lax.dot_general(a, b, (((1,), (1,)), ((), ())), preferred_element_type=F32)


def _log_sigmoid(x):
    return jnp.minimum(x, 0.0) - jnp.log1p(jnp.exp(-jnp.abs(x)))


def _tile(n, candidates):
    for c in candidates:
        if n % c == 0:
            return c
    raise ValueError(f"no tile for {n} in {candidates}")


def _matmul_kernel(x_ref, w_ref, o_ref):
    o_ref[...] = _dot(x_ref[...], w_ref[...])


def _in_proj(xb, wb):
    n, d = xb.shape
    w = wb.shape[1]
    tm = _tile(n, (832, 640, 512, 256, 128, 64))
    tn = _tile(w, (1024, 512, 256, 128))
    return pl.pallas_call(
        _matmul_kernel,
        out_shape=jax.ShapeDtypeStruct((n, w), F32),
        grid=(w // tn, n // tm),
        in_specs=[pl.BlockSpec((tm, d), lambda j, i: (i, 0)),
                  pl.BlockSpec((d, tn), lambda j, i: (0, j))],
        out_specs=pl.BlockSpec((tm, tn), lambda j, i: (i, j)),
        compiler_params=_params(("parallel", "parallel"), 48),
        name="in_proj",
    )(xb, wb)


def _gates_kernel(x_ref, w_ref, b_ref, o_ref):
    x = x_ref[...]
    w = w_ref[...]
    xh = x.astype(BF16)
    xl = (x - xh.astype(F32)).astype(BF16)
    wh = w.astype(BF16)
    wl = (w - wh.astype(F32)).astype(BF16)
    o_ref[...] = _dot(xh, wh) + _dot(xl, wh) + _dot(xh, wl) + b_ref[...]


def _gates(x, wg, bg):
    n, d = x.shape
    tm = _tile(n, (416, 320, 256, 128, 64))
    return pl.pallas_call(
        _gates_kernel,
        out_shape=jax.ShapeDtypeStruct((n, LANES), F32),
        grid=(n // tm,),
        in_specs=[pl.BlockSpec((tm, d), lambda i: (i, 0)),
                  pl.BlockSpec((d, LANES), lambda i: (0, 0)),
                  pl.BlockSpec((1, LANES), lambda i: (0, 0))],
        out_specs=pl.BlockSpec((tm, LANES), lambda i: (i, 0)),
        compiler_params=_params(("parallel",), 40),
        name="gates",
    )(x, wg, bg)


def _moba_prompt_kernel(q_ref, k_ref, v_ref, o_ref, kb_s, vt_s, km_s, sel_s, acc_s, m_s, l_s):
    i = pl.program_id(2)
    nb = kb_s.shape[0]
    blk = MOBA_BLOCK
    nq = ATT_GROUP * blk

    @pl.when(i == 0)
    def _():
        for j in range(nb):
            kj = k_ref[j * blk:(j + 1) * blk, :]
            kb_s[j] = kj.astype(BF16)
            km_s[j:j + 1, :] = jnp.mean(kj, axis=0, keepdims=True)
            vt_s[j] = v_ref[j * blk:(j + 1) * blk, :].T.astype(BF16)
        km_s[nb:, :] = jnp.zeros((km_s.shape[0] - nb, HEAD_DIM), F32)

    q = q_ref[...]
    q2 = jnp.concatenate([q[:, g * HEAD_DIM:(g + 1) * HEAD_DIM] for g in range(ATT_GROUP)], axis=0).astype(BF16)

    gate = _dot_nt(km_s[...].astype(BF16), q2)
    jj = lax.broadcasted_iota(I32, gate.shape, 0)
    g = jnp.where(jj < i, gate, -jnp.inf)
    sel = jnp.zeros(gate.shape, F32)
    for _ in range(MOBA_TOPK):
        mx = jnp.max(g, axis=0, keepdims=True)
        first = jnp.min(jnp.where(g == mx, jj, 2 * nb), axis=0, keepdims=True)
        pick = (jj == first) & (mx > -jnp.inf)
        sel = jnp.where(pick, 1.0, sel)
        g = jnp.where(pick, -jnp.inf, g)
    for j in range(nb):
        sel_s[j] = sel[j:j + 1, :]

    m_s[...] = jnp.full(m_s.shape, -jnp.inf, F32)
    l_s[...] = jnp.zeros(l_s.shape, F32)
    acc_s[...] = jnp.zeros(acc_s.shape, F32)

    def attend(j, mask):
        s = _dot_nt(kb_s[j], q2) * ATT_SCALE
        s = jnp.where(mask, s, -jnp.inf)
        m_old = m_s[...]
        m_new = jnp.maximum(m_old, jnp.max(s, axis=0, keepdims=True))
        alpha = jnp.exp(m_old - m_new)
        p = jnp.exp(s - m_new)
        l_s[...] = alpha * l_s[...] + jnp.sum(p, axis=0, keepdims=True)
        acc_s[...] = alpha * acc_s[...] + _dot(vt_s[j], p.astype(BF16))
        m_s[...] = m_new

    ri = lax.broadcasted_iota(I32, (blk, nq), 0)
    ci = lax.broadcasted_iota(I32, (blk, nq), 1)
    attend(i, ri <= (ci & (blk - 1)))

    def past(j, carry):
        attend(j, sel_s[j] > 0.5)
        return carry

    lax.fori_loop(0, i, past, 0)

    out = (acc_s[...] / l_s[...]).T
    for g_i in range(ATT_GROUP):
        o_ref[:, g_i * HEAD_DIM:(g_i + 1) * HEAD_DIM] = out[g_i * blk:(g_i + 1) * blk, :].astype(o_ref.dtype)


def _moba_prompt(z, bsz, seq):
    nb = seq // MOBA_BLOCK
    nq = ATT_GROUP * MOBA_BLOCK
    qw = ATT_GROUP * HEAD_DIM
    return pl.pallas_call(
        _moba_prompt_kernel,
        out_shape=jax.ShapeDtypeStruct((bsz * seq, ATT_Q_W), BF16),
        grid=(bsz, ATT_KV_HEADS, nb),
        in_specs=[pl.BlockSpec((MOBA_BLOCK, qw), lambda b, h, i: (b * nb + i, COL_Q // qw + h)),
                  pl.BlockSpec((seq, HEAD_DIM), lambda b, h, i: (b, COL_K // HEAD_DIM + h)),
                  pl.BlockSpec((seq, HEAD_DIM), lambda b, h, i: (b, COL_V // HEAD_DIM + h))],
        out_specs=pl.BlockSpec((MOBA_BLOCK, qw), lambda b, h, i: (b * nb + i, h)),
        scratch_shapes=[pltpu.VMEM((nb, MOBA_BLOCK, HEAD_DIM), BF16),
                        pltpu.VMEM((nb, HEAD_DIM, MOBA_BLOCK), BF16),
                        pltpu.VMEM((2 * SUBLANES, HEAD_DIM), F32),
                        pltpu.VMEM((nb, 1, nq), F32),
                        pltpu.VMEM((HEAD_DIM, nq), F32),
                        pltpu.VMEM((1, nq), F32),
                        pltpu.VMEM((1, nq), F32)],
        compiler_params=_params(("parallel", "parallel", "arbitrary"), 32),
        name="moba_prompt",
    )(z, z, z)


def _moba_sample_kernel(n_pages, pt_ref, q_ref, kn_ref, vn_ref, *rest):
    del pt_ref
    kp = rest[:n_pages]
    vp = rest[n_pages:2 * n_pages]
    o_ref = rest[2 * n_pages]
    page = kp[0].shape[2]
    per_blk = MOBA_BLOCK // page
    nb = n_pages // per_blk
    kvw = ATT_KV_W

    q = q_ref[0]
    r8 = lax.broadcasted_iota(I32, (ATT_KV_HEADS, kvw), 0)
    lh = lax.broadcasted_iota(I32, (ATT_KV_HEADS, kvw), 1) // HEAD_DIM
    bd = r8 == lh
    parts = []
    for g in range(ATT_GROUP):
        qg = jnp.concatenate([q[:, (ATT_GROUP * h + g) * HEAD_DIM:(ATT_GROUP * h + g + 1) * HEAD_DIM]
                              for h in range(ATT_KV_HEADS)], axis=1)
        parts.append(jnp.where(bd, qg, 0.0))
    wq = jnp.concatenate(parts, axis=0)
    wqb = wq.astype(BF16)
    bd2 = jnp.concatenate([bd] * ATT_GROUP, axis=0)

    scores, ksum = [], []
    for p in range(n_pages):
        kpg = kp[p][0, 0]
        scores.append(_dot_nt(wqb, kpg.astype(BF16)) * ATT_SCALE)
        ksum.append(jnp.sum(kpg, axis=0, keepdims=True))
    gates = []
    for j in range(nb):
        kmean = sum(ksum[j * per_blk:(j + 1) * per_blk]) * (1.0 / MOBA_BLOCK)
        gates.append(jnp.sum(wq * kmean, axis=1, keepdims=True))
    sel = []
    for j in range(nb):
        rank = jnp.zeros(gates[j].shape, F32)
        for jp in range(nb):
            if jp < j:
                rank = rank + (gates[jp] >= gates[j]).astype(F32)
            elif jp > j:
                rank = rank + (gates[jp] > gates[j]).astype(F32)
        sel.append(rank < MOBA_TOPK)

    s_new = jnp.sum(wq * kn_ref[0], axis=1, keepdims=True) * ATT_SCALE
    m = s_new
    for p in range(n_pages):
        scores[p] = jnp.where(sel[p // per_blk], scores[p], -jnp.inf)
        m = jnp.maximum(m, jnp.max(scores[p], axis=1, keepdims=True))
    p_new = jnp.exp(s_new - m)
    l = p_new
    acc = p_new * vn_ref[0]
    for p in range(n_pages):
        pp = jnp.exp(scores[p] - m)
        l = l + jnp.sum(pp, axis=1, keepdims=True)
        acc = acc + _dot(pp.astype(BF16), vp[p][0, 0].astype(BF16))
    out = jnp.where(bd2, acc / l, 0.0)
    per_g = [jnp.sum(out[g * ATT_KV_HEADS:(g + 1) * ATT_KV_HEADS], axis=0, keepdims=True)
             for g in range(ATT_GROUP)]
    pieces = []
    for h in range(ATT_KV_HEADS):
        for g in range(ATT_GROUP):
            pieces.append(per_g[g][:, h * HEAD_DIM:(h + 1) * HEAD_DIM])
    o_ref[0] = jnp.concatenate(pieces, axis=1).astype(o_ref.dtype)


def _moba_sample(zs3, cache_k4, cache_v4, page_table, layer):
    ns = zs3.shape[0]
    n_pages = page_table.shape[1]
    page, kvw = cache_k4.shape[2], cache_k4.shape[3]
    pt_flat = page_table.reshape(-1).astype(I32)

    def page_spec(p):
        return pl.BlockSpec((1, 1, page, kvw), lambda s, pt: (layer, pt[s * n_pages + p], 0, 0))

    in_specs = ([pl.BlockSpec((1, 1, ATT_Q_W), lambda s, pt: (s, 0, COL_Q // ATT_Q_W)),
                 pl.BlockSpec((1, 1, kvw), lambda s, pt: (s, 0, COL_K // kvw)),
                 pl.BlockSpec((1, 1, kvw), lambda s, pt: (s, 0, COL_V // kvw))]
                + [page_spec(p) for p in range(n_pages)] * 2)
    grid_spec = pltpu.PrefetchScalarGridSpec(
        num_scalar_prefetch=1, grid=(ns,), in_specs=in_specs,
        out_specs=pl.BlockSpec((1, 1, ATT_Q_W), lambda s, pt: (s, 0, 0)))
    return pl.pallas_call(
        functools.partial(_moba_sample_kernel, n_pages),
        out_shape=jax.ShapeDtypeStruct((ns, 1, ATT_Q_W), BF16),
        grid_spec=grid_spec,
        compiler_params=_params(("parallel",), 48),
        name="moba_sample",
    )(pt_flat, zs3, zs3, zs3, *([cache_k4] * n_pages), *([cache_v4] * n_pages))


def _mlstm_prompt_kernel(q_ref, k_ref, v_ref, og_ref, igr_ref, fgr_ref, igc_ref, fgc_ref, ng_ref,
                         y_ref, c_ref, n_ref, m_ref):
    @pl.when(pl.program_id(2) == 0)
    def _():
        c_ref[...] = jnp.zeros(c_ref.shape, F32)
        n_ref[...] = jnp.zeros(n_ref.shape, F32)
        m_ref[...] = jnp.zeros(m_ref.shape, F32)

    L = q_ref.shape[0]
    q = q_ref[...] * M_Q_SCALE
    k = k_ref[...]
    ig_r = igr_ref[0, 0]
    lf_r = _log_sigmoid(fgr_ref[0, 0])
    ig_c = igc_ref[0, 0]
    lf_c = _log_sigmoid(fgc_ref[0, 0])
    ri = lax.broadcasted_iota(I32, (L, L), 0)
    ci = lax.broadcasted_iota(I32, (L, L), 1)
    tril = ci <= ri
    b_c = jnp.sum(jnp.where(tril, lf_r, 0.0), axis=1, keepdims=True)
    b_r = jnp.sum(jnp.where(ri <= ci, lf_c, 0.0), axis=0, keepdims=True)
    r_r = ig_r - b_r
    m0 = m_ref[0, 0][:, 0:1]
    m_c = b_c + jnp.maximum(m0, jnp.max(jnp.where(tril, r_r, -jnp.inf), axis=1, keepdims=True))
    dmat = jnp.exp(jnp.where(tril, b_c - m_c + r_r, -jnp.inf))
    qb = q.astype(BF16)
    vb = v_ref[...].astype(BF16)
    s = _dot_nt(qb, k.astype(BF16)) * dmat
    inter = jnp.exp(b_c + m0 - m_c)
    c0 = c_ref[0, 0]
    n0 = n_ref[0, 0]
    num = inter * _dot(qb, c0.astype(BF16)) + _dot(s.astype(BF16), vb)
    den = inter * jnp.sum(q * n0, axis=1, keepdims=True) + jnp.sum(s, axis=1, keepdims=True)
    h = num / jnp.maximum(jnp.abs(den), jnp.exp(-m_c))
    hn = h * lax.rsqrt(jnp.mean(h * h, axis=1, keepdims=True) + NORM_EPS)
    y_ref[...] = (hn * ng_ref[...] * jax.nn.sigmoid(og_ref[...])).astype(y_ref.dtype)

    b_last = b_c[L - 1:L, :]
    m_last = m_c[L - 1:L, :]
    kw = k * jnp.exp(b_last - b_c + ig_c - m_last)
    decay = jnp.exp(b_last + m0 - m_last)
    c_ref[0, 0] = decay * c0 + _dot(kw.T.astype(BF16), vb)
    n_ref[0, 0] = decay * n0 + jnp.sum(kw, axis=0, keepdims=True)
    m_ref[0, 0] = jnp.broadcast_to(m_last, (1, LANES))


def _mlstm_prompt(z, g_all, norm_g, bsz, seq):
    L = _tile(seq, (256, 128))
    nc = seq // L
    gp = jnp.transpose(g_all[:bsz * seq, :2 * M_HEADS].reshape(bsz, seq, 2 * M_HEADS), (0, 2, 1))
    g_row = gp.reshape(bsz, 2 * M_HEADS, 1, seq)
    g_col = gp.reshape(bsz, 2 * M_HEADS, seq, 1)
    row = lambda off: pl.BlockSpec((1, 1, 1, L), lambda b, h, c: (b, off + h, 0, c))
    col = lambda off: pl.BlockSpec((1, 1, L, 1), lambda b, h, c: (b, off + h, c, 0))
    state = lambda shape: pl.BlockSpec((1, 1) + shape, lambda b, h, c: (b, h, 0, 0))
    return pl.pallas_call(
        _mlstm_prompt_kernel,
        out_shape=(jax.ShapeDtypeStruct((bsz * seq, M_V_W), BF16),
                   jax.ShapeDtypeStruct((bsz, M_HEADS, M_QK, M_V), F32),
                   jax.ShapeDtypeStruct((bsz, M_HEADS, 1, M_QK), F32),
                   jax.ShapeDtypeStruct((bsz, M_HEADS, 1, LANES), F32)),
        grid=(bsz, M_HEADS, nc),
        in_specs=[pl.BlockSpec((L, M_QK), lambda b, h, c: (b * nc + c, COL_QM // M_QK + h)),
                  pl.BlockSpec((L, M_QK), lambda b, h, c: (b * nc + c, COL_KM // M_QK + h)),
                  pl.BlockSpec((L, M_V), lambda b, h, c: (b * nc + c, COL_VM // M_V + h)),
                  pl.BlockSpec((L, M_V), lambda b, h, c: (b * nc + c, COL_OM // M_V + h)),
                  row(0), row(M_HEADS), col(0), col(M_HEADS),
                  pl.BlockSpec((1, M_V), lambda b, h, c: (0, h))],
        out_specs=(pl.BlockSpec((L, M_V), lambda b, h, c: (b * nc + c, h)),
                   state((M_QK, M_V)), state((1, M_QK)), state((1, LANES))),
        compiler_params=_params(("parallel", "parallel", "arbitrary"), 32),
        name="mlstm_prompt",
    )(z, z, z, z, g_row, g_row, g_col, g_col, norm_g)


def _mlstm_sample_kernel(qr_ref, kr_ref, vr_ref, or_ref, qc_ref, kc_ref, g_ref, m0_ref, n0_ref, c0_ref, ng_ref,
                         y_ref, c_ref, n_ref, m_ref):
    g = g_ref[0]
    m0_all = m0_ref[0]
    lane = lax.broadcasted_iota(I32, (1, LANES), 1)
    m_out = jnp.zeros((1, LANES), F32)
    for h in range(M_HEADS):
        ig = g[:, h:h + 1]
        lf = _log_sigmoid(g[:, M_HEADS + h:M_HEADS + h + 1])
        m0 = m0_all[:, h:h + 1]
        m = jnp.maximum(lf + m0, ig)
        w = jnp.exp(ig - m)
        decay = jnp.exp(lf + m0 - m)
        q_r = qr_ref[0, h] * M_Q_SCALE
        k_r = kr_ref[0, h]
        v_r = vr_ref[0, h]
        q_c = qc_ref[0, h] * M_Q_SCALE
        c0 = c0_ref[0, 0, h]
        n0 = n0_ref[0, 0, h]
        s = jnp.sum(q_r * k_r, axis=1, keepdims=True) * w
        num = decay * jnp.sum(q_c * c0, axis=0, keepdims=True) + s * v_r
        den = decay * jnp.sum(q_r * n0, axis=1, keepdims=True) + s
        hh = num / jnp.maximum(jnp.abs(den), jnp.exp(-m))
        hn = hh * lax.rsqrt(jnp.mean(hh * hh, axis=1, keepdims=True) + NORM_EPS)
        y = hn * ng_ref[:, h * M_V:(h + 1) * M_V] * jax.nn.sigmoid(or_ref[0, h])
        y_ref[0, :, h * M_V:(h + 1) * M_V] = y.astype(y_ref.dtype)
        c_ref[0, h] = decay * c0 + (w * kc_ref[0, h]) * v_r
        n_ref[0, h] = decay * n0 + w * k_r
        m_out = jnp.where(lane == h, m, m_out)
    m_ref[0] = m_out


def _mlstm_sample(zs, gs, state_c, state_n, state_m, norm_g, layer):
    ns = zs.shape[0]
    qm = zs[:, COL_QM:COL_QM + M_QK_W]
    km = zs[:, COL_KM:COL_KM + M_QK_W]
    rows = lambda a, w: a.reshape(ns, M_HEADS, 1, w)
    m0 = jnp.pad(state_m[layer], ((0, 0), (0, LANES - M_HEADS))).reshape(ns, 1, LANES)
    row_spec = lambda w: pl.BlockSpec((1, M_HEADS, 1, w), lambda s: (s, 0, 0, 0))
    lane_spec = pl.BlockSpec((1, 1, LANES), lambda s: (s, 0, 0))
    col_spec = pl.BlockSpec((1, M_HEADS, M_QK, 1), lambda s: (s, 0, 0, 0))
    return pl.pallas_call(
        _mlstm_sample_kernel,
        out_shape=(jax.ShapeDtypeStruct((ns, 1, M_V_W), BF16),
                   jax.ShapeDtypeStruct((ns, M_HEADS, M_QK, M_V), F32),
                   jax.ShapeDtypeStruct((ns, M_HEADS, 1, M_QK), F32),
                   jax.ShapeDtypeStruct((ns, 1, LANES), F32)),
        grid=(ns,),
        in_specs=[row_spec(M_QK), row_spec(M_QK), row_spec(M_V), row_spec(M_V), col_spec, col_spec,
                  lane_spec, lane_spec,
                  pl.BlockSpec((1, 1, M_HEADS, 1, M_QK), lambda s: (layer, s, 0, 0, 0)),
                  pl.BlockSpec((1, 1, M_HEADS, M_QK, M_V), lambda s: (layer, s, 0, 0, 0)),
                  pl.BlockSpec((1, M_V_W), lambda s: (0, 0))],
        out_specs=(pl.BlockSpec((1, 1, M_V_W), lambda s: (s, 0, 0)),
                   pl.BlockSpec((1, M_HEADS, M_QK, M_V), lambda s: (s, 0, 0, 0)),
                   row_spec(M_QK), lane_spec),
        compiler_params=_params(("parallel",), 32),
        name="mlstm_sample",
    )(rows(qm, M_QK), rows(km, M_QK), rows(zs[:, COL_VM:COL_VM + M_V_W], M_V), rows(zs[:, COL_OM:COL_OM + M_V_W], M_V),
      qm.reshape(ns, M_HEADS, M_QK, 1), km.reshape(ns, M_HEADS, M_QK, 1),
      gs.reshape(ns, 1, LANES), m0,
      state_n.reshape(state_n.shape[0], ns, M_HEADS, 1, M_QK), state_c, norm_g)


def _layernorm_chunks(chunks, d):
    total = sum(jnp.sum(c, axis=1, keepdims=True) for c in chunks)
    mu = total * (1.0 / d)
    var = sum(jnp.sum(jnp.square(c - mu), axis=1, keepdims=True) for c in chunks) * (1.0 / d)
    return mu, lax.rsqrt(var + LN_EPS)


def _merge_kernel(alpha, att_ref, mix_ref, wa_ref, wb_ref, x_ref, g_ref, b_ref, wr_ref, br_ref,
                  h_ref, ids_ref, wts_ref, acc_s, hb_s):
    j = pl.program_id(1)
    nj = acc_s.shape[0]
    tm, tn = acc_s.shape[1], acc_s.shape[2]
    d = nj * tn
    acc_s[j] = alpha * x_ref[...] + _dot(att_ref[...], wa_ref[...]) + _dot(mix_ref[...], wb_ref[...])

    @pl.when(j == nj - 1)
    def _():
        mu, rstd = _layernorm_chunks([acc_s[c] for c in range(nj)], d)
        slab_w = d // SLAB_ROWS
        for c in range(nj):
            y = (acc_s[c] - mu) * rstd * g_ref[:, c * tn:(c + 1) * tn] + b_ref[:, c * tn:(c + 1) * tn]
            hb_s[:, c * tn:(c + 1) * tn] = y.astype(BF16)
            for t in range(tn // slab_w):
                r = (c * tn) // slab_w + t
                h_ref[pl.ds(r, tm, stride=SLAB_ROWS), :] = y[:, t * slab_w:(t + 1) * slab_w]

        lg = _dot(hb_s[...], wr_ref[...]) + br_ref[...]
        lane = lax.broadcasted_iota(I32, lg.shape, 1)
        lane_f = lane.astype(F32)
        far = float(LANES)
        gl = jnp.where(lane < N_GROUPS, lg, -jnp.inf)
        gmax = jnp.max(gl, axis=1, keepdims=True)
        gsel = jnp.min(jnp.where(gl == gmax, lane_f, far), axis=1, keepdims=True).astype(I32)
        gw = 1.0 / jnp.sum(jnp.where(lane < N_GROUPS, jnp.exp(gl - gmax), 0.0), axis=1, keepdims=True)
        e_lane = lane - N_GROUPS
        in_grp = (e_lane >= 0) & (e_lane < N_EXPERTS) & ((e_lane // EXPERTS_PER_GROUP) == gsel)
        el = jnp.where(in_grp, lg, -jnp.inf)
        t1 = jnp.max(el, axis=1, keepdims=True)
        i1 = jnp.min(jnp.where(el == t1, lane_f, far), axis=1, keepdims=True)
        el2 = jnp.where(lane_f == i1, -jnp.inf, el)
        t2 = jnp.max(el2, axis=1, keepdims=True)
        i2 = jnp.min(jnp.where(el2 == t2, lane_f, far), axis=1, keepdims=True)
        e21 = jnp.exp(t2 - t1)
        w1 = gw / (1.0 + e21)
        w2 = w1 * e21
        ids = jnp.where(lane == 0, i1, jnp.where(lane == 1, i2, float(N_GROUPS))) - float(N_GROUPS)
        ids_ref[...] = ids.astype(I32)
        wts_ref[...] = jnp.where(lane == 0, w1, jnp.where(lane == 1, w2, 0.0))


def _merge(att, mix, wob, x, ln_g, ln_b, wr, br, alpha):
    n, d = x.shape
    half = att.shape[1]
    tm = _tile(n, (320, 256, 128, 64))
    tn = _tile(d, (1024, 512))
    nj = d // tn
    rows = lambda w: pl.BlockSpec((1, w), lambda i, j: (0, 0))
    return pl.pallas_call(
        functools.partial(_merge_kernel, alpha),
        out_shape=(jax.ShapeDtypeStruct((n * SLAB_ROWS, d // SLAB_ROWS), F32),
                   jax.ShapeDtypeStruct((n, LANES), I32),
                   jax.ShapeDtypeStruct((n, LANES), F32)),
        grid=(n // tm, nj),
        in_specs=[pl.BlockSpec((tm, half), lambda i, j: (i, 0)),
                  pl.BlockSpec((tm, half), lambda i, j: (i, 0)),
                  pl.BlockSpec((half, tn), lambda i, j: (0, j)),
                  pl.BlockSpec((half, tn), lambda i, j: (1, j)),
                  pl.BlockSpec((tm, tn), lambda i, j: (i, j)),
                  rows(d), rows(d),
                  pl.BlockSpec((d, LANES), lambda i, j: (0, 0)),
                  rows(LANES)],
        out_specs=(pl.BlockSpec((tm * SLAB_ROWS, d // SLAB_ROWS), lambda i, j: (i, 0)),
                   pl.BlockSpec((tm, LANES), lambda i, j: (i, 0)),
                   pl.BlockSpec((tm, LANES), lambda i, j: (i, 0))),
        scratch_shapes=[pltpu.VMEM((nj, tm, tn), F32), pltpu.VMEM((tm, d), BF16)],
        compiler_params=_params(("parallel", "arbitrary"), 48),
        name="merge_ln_router",
    )(att, mix, wob, wob, x, ln_g, ln_b, wr, br)


def _gather_kernel(n_ref, idx_ref, src_ref, out_ref, sem):
    n = n_ref[0]

    def copy(p):
        return pltpu.make_async_copy(src_ref.at[idx_ref[p]], out_ref.at[p], sem)

    def issue(p, carry):
        copy(p).start()

        @pl.when(p >= GATHER_WINDOW)
        def _():
            copy(p - GATHER_WINDOW).wait()
        return carry

    lax.fori_loop(0, n, issue, 0)

    def drain(p, carry):
        copy(p).wait()
        return carry

    lax.fori_loop(jnp.maximum(n - GATHER_WINDOW, 0), n, drain, 0)


def _row_gather(src3, idx, n_rows):
    p = idx.shape[0]
    grid_spec = pltpu.PrefetchScalarGridSpec(
        num_scalar_prefetch=2, grid=(1,),
        in_specs=[pl.BlockSpec(memory_space=pl.ANY)],
        out_specs=pl.BlockSpec(memory_space=pl.ANY),
        scratch_shapes=[pltpu.SemaphoreType.DMA(())])
    return pl.pallas_call(
        _gather_kernel,
        out_shape=jax.ShapeDtypeStruct((p,) + src3.shape[1:], src3.dtype),
        grid_spec=grid_spec,
        compiler_params=_params(("arbitrary",), 16),
        name="row_gather",
    )(n_rows, idx, src3)


def _moe_kernel(te_ref, ti_ref, nu_ref, x_ref, wg_ref, wu_ref, wd_ref, o_ref, xb_s):
    del te_ref, ti_ref
    t = pl.program_id(0)
    j = pl.program_id(1)
    tm, d = xb_s.shape
    slab_w = d // SLAB_ROWS
    fc = wg_ref.shape[3]

    @pl.when(t < nu_ref[0])
    def _():
        @pl.when(j == 0)
        def _():
            for r in range(SLAB_ROWS):
                xb_s[:, r * slab_w:(r + 1) * slab_w] = x_ref[pl.ds(r, tm, stride=SLAB_ROWS), :].astype(BF16)

        wgu = jnp.concatenate([wg_ref[0, 0].astype(BF16), wu_ref[0, 0].astype(BF16)], axis=1)
        hgu = _dot(xb_s[...], wgu)
        hg = hgu[:, :fc]
        a = (hg * jax.nn.sigmoid(hg) * hgu[:, fc:]).astype(BF16)
        part = _dot(a, wd_ref[0, 0].astype(BF16))

        @pl.when(j == 0)
        def _():
            for r in range(SLAB_ROWS):
                o_ref[pl.ds(r, tm, stride=SLAB_ROWS), :] = part[:, r * slab_w:(r + 1) * slab_w]

        @pl.when(j != 0)
        def _():
            for r in range(SLAB_ROWS):
                sl = pl.ds(r, tm, stride=SLAB_ROWS)
                o_ref[sl, :] = o_ref[sl, :] + part[:, r * slab_w:(r + 1) * slab_w]


def _moe_experts(xs2, w_gate, w_up, w_down, tile_expert, tile_index, n_used, layer):
    tm = MOE_TILE
    fc = MOE_FCHUNK
    n_tiles = tile_expert.shape[0]
    d = w_gate.shape[2]
    f = w_gate.shape[3]
    nj = f // fc
    slab_w = d // SLAB_ROWS

    def chunk(t, j, nu):
        return jnp.where(t < nu[0], j, nj - 1)

    grid_spec = pltpu.PrefetchScalarGridSpec(
        num_scalar_prefetch=3, grid=(n_tiles, nj),
        in_specs=[pl.BlockSpec((tm * SLAB_ROWS, slab_w), lambda t, j, te, ti, nu: (ti[t], 0)),
                  pl.BlockSpec((1, 1, d, fc), lambda t, j, te, ti, nu: (layer, te[t], 0, chunk(t, j, nu))),
                  pl.BlockSpec((1, 1, d, fc), lambda t, j, te, ti, nu: (layer, te[t], 0, chunk(t, j, nu))),
                  pl.BlockSpec((1, 1, fc, d), lambda t, j, te, ti, nu: (layer, te[t], chunk(t, j, nu), 0))],
        out_specs=pl.BlockSpec((tm * SLAB_ROWS, slab_w), lambda t, j, te, ti, nu: (ti[t], 0)),
        scratch_shapes=[pltpu.VMEM((tm, d), BF16)])
    return pl.pallas_call(
        _moe_kernel,
        out_shape=jax.ShapeDtypeStruct(xs2.shape, F32),
        grid_spec=grid_spec,
        compiler_params=_params(("arbitrary", "arbitrary"), 56),
        name="moe_experts",
    )(tile_expert, tile_index, n_used, xs2, w_gate, w_up, w_down)


def _combine_kernel(alpha, h_ref, f0_ref, f1_ref, w_ref, g_ref, b_ref, o_ref):
    tm, d = o_ref.shape
    slab_w = d // SLAB_ROWS
    w = w_ref[...]
    w1 = w[:, 0:1]
    w2 = w[:, 1:2]
    for r in range(SLAB_ROWS):
        sl = pl.ds(r, tm, stride=SLAB_ROWS)
        o_ref[:, r * slab_w:(r + 1) * slab_w] = alpha * h_ref[sl, :] + w1 * f0_ref[sl, :] + w2 * f1_ref[sl, :]
    cols = lambda r: slice(r * slab_w, (r + 1) * slab_w)
    mu, rstd = _layernorm_chunks([o_ref[:, cols(r)] for r in range(SLAB_ROWS)], d)
    for r in range(SLAB_ROWS):
        o_ref[:, cols(r)] = (o_ref[:, cols(r)] - mu) * rstd * g_ref[:, cols(r)] + b_ref[:, cols(r)]


def _combine(h2, ff2, wts, ln_g, ln_b, alpha, row0, nrows):
    n = wts.shape[0]
    d = ln_g.shape[1]
    slab_w = d // SLAB_ROWS
    tm = next(c for c in (256, 128, 64, 32, 16, 8) if nrows % c == 0 and row0 % c == 0 and n % c == 0)
    o0 = row0 // tm
    slab = lambda off: pl.BlockSpec((tm * SLAB_ROWS, slab_w), lambda i: (i + off, 0))
    rows = pl.BlockSpec((1, d), lambda i: (0, 0))
    return pl.pallas_call(
        functools.partial(_combine_kernel, alpha),
        out_shape=jax.ShapeDtypeStruct((nrows, d), F32),
        grid=(nrows // tm,),
        in_specs=[slab(o0), slab(o0), slab(o0 + n // tm),
                  pl.BlockSpec((tm, LANES), lambda i: (i + o0, 0)), rows, rows],
        out_specs=pl.BlockSpec((tm, d), lambda i: (i, 0)),
        compiler_params=_params(("parallel",), 40),
        name="combine_ln",
    )(h2, ff2, ff2, wts, ln_g, ln_b)


def _routing_tables(ids, n_tiles):
    n = ids.shape[0]
    tm = MOE_TILE
    e = ids[:, :2].reshape(-1)
    onehot = (e[:, None] == jnp.arange(N_EXPERTS, dtype=I32)[None, :]).astype(I32)
    csum = jnp.cumsum(onehot, axis=0)
    rank = jnp.sum((csum - onehot) * onehot, axis=1)
    counts = csum[-1]
    tiles = (counts + tm - 1) // tm
    tile_end = jnp.cumsum(tiles)
    n_used = tile_end[-1]
    pos = ((tile_end - tiles)[e] * tm + rank).astype(I32)
    src = jnp.zeros((n_tiles * tm,), I32).at[pos].set(jnp.arange(2 * n, dtype=I32) // 2)
    t = jnp.minimum(jnp.arange(n_tiles, dtype=I32), n_used - 1)
    tile_expert = jnp.minimum(jnp.searchsorted(tile_end, t, side="right"), N_EXPERTS - 1).astype(I32)
    pos2 = pos.reshape(n, 2)
    back = jnp.concatenate([pos2[:, 0], pos2[:, 1]])
    return src, back, tile_expert, t, n_used.reshape(1).astype(I32)


def _layer(l, x, bsz, seq, cache_k, cache_v, state_c, state_n, state_m, page_table,
           w_in, b_gates, norm_g, w_out, ln1_g, ln1_b, w_group, b_group, w_router, b_router,
           w_e_gate, w_e_up, w_e_down, ln2_g, ln2_b, alpha):
    n, d = x.shape
    n_p = bsz * seq
    n_s = n - n_p

    wz = w_in[l, :, :COL_G].astype(BF16)
    wg = jnp.pad(w_in[l, :, COL_G:], ((0, 0), (0, LANES - 2 * M_HEADS)))
    bg = jnp.pad(b_gates[l], (0, LANES - 2 * M_HEADS)).reshape(1, LANES)
    z = _in_proj(x.astype(BF16), wz)
    g_all = _gates(x, wg, bg)

    ng = norm_g[l].reshape(1, M_V_W)
    att_p = _moba_prompt(z, bsz, seq)
    mix_p, c_p, n_pr, m_p = _mlstm_prompt(z, g_all, ng, bsz, seq)

    zs = z[n_p:]
    pool, page = cache_k.shape[1], cache_k.shape[2]
    att_s = _moba_sample(zs.reshape(n_s, 1, COL_G), cache_k.reshape(-1, pool, page, ATT_KV_W),
                         cache_v.reshape(-1, pool, page, ATT_KV_W), page_table, l)
    mix_s, c_s, n_sm, m_s = _mlstm_sample(zs, g_all[n_p:], state_c, state_n, state_m, ng, l)

    att = jnp.concatenate([att_p, att_s.reshape(n_s, ATT_Q_W)], axis=0)
    mix = jnp.concatenate([mix_p, mix_s.reshape(n_s, M_V_W)], axis=0)

    wr = jnp.pad(jnp.concatenate([w_group[l], w_router[l]], axis=1),
                 ((0, 0), (0, LANES - N_GROUPS - N_EXPERTS))).astype(BF16)
    br = jnp.pad(jnp.concatenate([b_group[l], b_router[l]]), (0, LANES - N_GROUPS - N_EXPERTS)).reshape(1, LANES)
    h2, ids, wts = _merge(att, mix, w_out[l].astype(BF16), x, ln1_g[l].reshape(1, d), ln1_b[l].reshape(1, d),
                          wr, br, alpha)

    n_tiles = (2 * n) // MOE_TILE + N_EXPERTS + 1
    src, back, tile_expert, tile_index, n_used = _routing_tables(ids, n_tiles)
    slab_w = d // SLAB_ROWS
    xs3 = _row_gather(h2.reshape(n, SLAB_ROWS, slab_w), src, n_used * MOE_TILE)
    ys2 = _moe_experts(xs3.reshape(-1, slab_w), w_e_gate, w_e_up, w_e_down, tile_expert, tile_index, n_used, l)
    ff3 = _row_gather(ys2.reshape(-1, SLAB_ROWS, slab_w), back, jnp.full((1,), 2 * n, I32))
    ff2 = ff3.reshape(-1, slab_w)
    g2, b2 = ln2_g[l].reshape(1, d), ln2_b[l].reshape(1, d)
    y_p = _combine(h2, ff2, wts, g2, b2, alpha, 0, n_p)
    y_s = _combine(h2, ff2, wts, g2, b2, alpha, n_p, n_s)

    kv = lambda rows, col: rows[:, col:col + ATT_KV_W]
    return (y_p, y_s,
            kv(z[:n_p], COL_K).reshape(bsz, seq, ATT_KV_HEADS, HEAD_DIM),
            kv(z[:n_p], COL_V).reshape(bsz, seq, ATT_KV_HEADS, HEAD_DIM),
            kv(zs, COL_K).reshape(n_s, 1, ATT_KV_HEADS, HEAD_DIM),
            kv(zs, COL_V).reshape(n_s, 1, ATT_KV_HEADS, HEAD_DIM),
            c_p, n_pr.reshape(bsz, M_HEADS, M_QK), m_p[:, :, 0, 0],
            c_s, n_sm.reshape(n_s, M_HEADS, M_QK), m_s[:, 0, :M_HEADS])


def kernel(x_prompt, x_sample, cache_k, cache_v, state_mlstm_c, state_mlstm_n, state_mlstm_m, page_table,
           w_in, b_gates, mlstm_norm_g, w_out, ln1_g, ln1_b, w_group, b_group, w_router, b_router,
           w_e_gate, w_e_up, w_e_down, ln2_g, ln2_b):
    bsz, seq, d = x_prompt.shape
    n_s, dec_seq, _ = x_sample.shape
    depth = w_in.shape[0]
    assert dec_seq == 1 and seq % MOBA_BLOCK == 0 and w_in.shape[2] == COL_G + 2 * M_HEADS
    assert (page_table.shape[1] * cache_k.shape[2]) % MOBA_BLOCK == 0 and MOBA_BLOCK % cache_k.shape[2] == 0
    alpha = (2 * depth) ** 0.25
    x = jnp.concatenate([x_prompt.reshape(bsz * seq, d), x_sample.reshape(n_s, d)], axis=0)
    per_layer = []
    for l in range(depth):
        outs = _layer(l, x, bsz, seq, cache_k, cache_v, state_mlstm_c, state_mlstm_n, state_mlstm_m, page_table,
                      w_in, b_gates, mlstm_norm_g, w_out, ln1_g, ln1_b, w_group, b_group, w_router, b_router,
                      w_e_gate, w_e_up, w_e_down, ln2_g, ln2_b, alpha)
        x = jnp.concatenate([outs[0], outs[1]], axis=0)
        per_layer.append(outs[2:])
    stacked = [jnp.stack([p[i] for p in per_layer]) for i in range(10)]
    return (outs[0].reshape(bsz, seq, d), outs[1].reshape(n_s, 1, d), *stacked)
```

```python
import functools

import jax
import jax.numpy as jnp
from jax import lax
from jax.experimental import pallas as pl
from jax.experimental.pallas import tpu as pltpu

F32, BF16, I32 = jnp.float32, jnp.bfloat16, jnp.int32

SUBLANES = 8
LANES = 128
MIB = 1024 * 1024

HEAD_DIM = 128
ATT_HEADS = 16
ATT_KV_HEADS = 8
ATT_GROUP = ATT_HEADS // ATT_KV_HEADS
ATT_SCALE = HEAD_DIM ** -0.5
MOBA_BLOCK = 256
MOBA_TOPK = 3
M_HEADS = 4
M_QK = 256
M_V = 512
M_Q_SCALE = M_QK ** -0.5
N_GROUPS = 4
EXPERTS_PER_GROUP = 8
N_EXPERTS = N_GROUPS * EXPERTS_PER_GROUP
LN_EPS = 1e-5
NORM_EPS = 1e-6

ATT_Q_W = ATT_HEADS * HEAD_DIM
ATT_KV_W = ATT_KV_HEADS * HEAD_DIM
M_QK_W = M_HEADS * M_QK
M_V_W = M_HEADS * M_V
COL_Q = 0
COL_K = COL_Q + ATT_Q_W
COL_V = COL_K + ATT_KV_W
COL_QM = COL_V + ATT_KV_W
COL_KM = COL_QM + M_QK_W
COL_VM = COL_KM + M_QK_W
COL_OM = COL_VM + M_V_W
COL_G = COL_OM + M_V_W

MOE_TILE = 512
MOE_FCHUNK = 128
GATHER_CHUNK = 256
GATHER_UNROLL = 8


def _params(sem, vmem_mib):
    return pltpu.CompilerParams(dimension_semantics=sem, vmem_limit_bytes=vmem_mib * MIB)


def _dot(a, b):
    return jnp.dot(a, b, preferred_element_type=F32)


def _dot_nt(a, b):
    return lax.dot_general(a, b, (((1,), (1,)), ((), ())), preferred_element_type=F32)


def _log_sigmoid(x):
    return jnp.minimum(x, 0.0) - jnp.log1p(jnp.exp(-jnp.abs(x)))


def _tile(n, candidates):
    for c in candidates:
        if n % c == 0:
            return c
    raise ValueError(f"no tile for {n} in {candidates}")


def _matmul_kernel(x_ref, w_ref, o_ref):
    o_ref[...] = _dot(x_ref[...], w_ref[...])


def _in_proj(xb, wb):
    n, d = xb.shape
    w = wb.shape[1]
    tm = _tile(n, (832, 640, 512, 256, 128, 64))
    tn = _tile(w, (1024, 512, 256, 128))
    return pl.pallas_call(
        _matmul_kernel,
        out_shape=jax.ShapeDtypeStruct((n, w), F32),
        grid=(w // tn, n // tm),
        in_specs=[pl.BlockSpec((tm, d), lambda j, i: (i, 0)),
                  pl.BlockSpec((d, tn), lambda j, i: (0, j))],
        out_specs=pl.BlockSpec((tm, tn), lambda j, i: (i, j)),
        compiler_params=_params(("parallel", "parallel"), 48),
        name="in_proj",
    )(xb, wb)


def _gates_kernel(x_ref, w_ref, b_ref, o_ref):
    x = x_ref[...]
    w = w_ref[...]
    xh = x.astype(BF16)
    xl = (x - xh.astype(F32)).astype(BF16)
    wh = w.astype(BF16)
    wl = (w - wh.astype(F32)).astype(BF16)
    o_ref[...] = _dot(xh, wh) + _dot(xl, wh) + _dot(xh, wl) + b_ref[...]


def _gates(x, wg, bg):
    n, d = x.shape
    tm = _tile(n, (416, 320, 256, 128, 64))
    return pl.pallas_call(
        _gates_kernel,
        out_shape=jax.ShapeDtypeStruct((n, LANES), F32),
        grid=(n // tm,),
        in_specs=[pl.BlockSpec((tm, d), lambda i: (i, 0)),
                  pl.BlockSpec((d, LANES), lambda i: (0, 0)),
                  pl.BlockSpec((1, LANES), lambda i: (0, 0))],
        out_specs=pl.BlockSpec((tm, LANES), lambda i: (i, 0)),
        compiler_params=_params(("parallel",), 40),
        name="gates",
    )(x, wg, bg)


def _moba_prompt_kernel(q_ref, k_ref, v_ref, o_ref, kb_s, vt_s, km_s, sel_s, acc_s, m_s, l_s):
    i = pl.program_id(2)
    nb = kb_s.shape[0]
    blk = MOBA_BLOCK
    nq = ATT_GROUP * blk

    @pl.when(i == 0)
    def _():
        for j in range(nb):
            kj = k_ref[j * blk:(j + 1) * blk, :]
            kb_s[j] = kj.astype(BF16)
            km_s[j:j + 1, :] = jnp.mean(kj, axis=0, keepdims=True)
            vt_s[j] = v_ref[j * blk:(j + 1) * blk, :].T.astype(BF16)
        km_s[nb:, :] = jnp.zeros((km_s.shape[0] - nb, HEAD_DIM), F32)

    q = q_ref[...]
    q2 = jnp.concatenate([q[:, g * HEAD_DIM:(g + 1) * HEAD_DIM] for g in range(ATT_GROUP)], axis=0).astype(BF16)

    gate = _dot_nt(km_s[...].astype(BF16), q2)
    jj = lax.broadcasted_iota(I32, gate.shape, 0)
    g = jnp.where(jj < i, gate, -jnp.inf)
    sel = jnp.zeros(gate.shape, F32)
    for _ in range(MOBA_TOPK):
        mx = jnp.max(g, axis=0, keepdims=True)
        first = jnp.min(jnp.where(g == mx, jj, 2 * nb), axis=0, keepdims=True)
        pick = (jj == first) & (mx > -jnp.inf)
        sel = jnp.where(pick, 1.0, sel)
        g = jnp.where(pick, -jnp.inf, g)
    for j in range(nb):
        sel_s[j] = sel[j:j + 1, :]

    m_s[...] = jnp.full(m_s.shape, -jnp.inf, F32)
    l_s[...] = jnp.zeros(l_s.shape, F32)
    acc_s[...] = jnp.zeros(acc_s.shape, F32)

    def attend(j, mask):
        s = _dot_nt(kb_s[j], q2) * ATT_SCALE
        s = jnp.where(mask, s, -jnp.inf)
        m_old = m_s[...]
        m_new = jnp.maximum(m_old, jnp.max(s, axis=0, keepdims=True))
        alpha = jnp.exp(m_old - m_new)
        p = jnp.exp(s - m_new)
        l_s[...] = alpha * l_s[...] + jnp.sum(p, axis=0, keepdims=True)
        acc_s[...] = alpha * acc_s[...] + _dot(vt_s[j], p.astype(BF16))
        m_s[...] = m_new

    ri = lax.broadcasted_iota(I32, (blk, nq), 0)
    ci = lax.broadcasted_iota(I32, (blk, nq), 1)
    attend(i, ri <= (ci & (blk - 1)))

    def past(j, carry):
        attend(j, sel_s[j] > 0.5)
        return carry

    lax.fori_loop(0, i, past, 0)

    out = (acc_s[...] / l_s[...]).T
    for g_i in range(ATT_GROUP):
        o_ref[:, g_i * HEAD_DIM:(g_i + 1) * HEAD_DIM] = out[g_i * blk:(g_i + 1) * blk, :].astype(o_ref.dtype)


def _moba_prompt(z, bsz, seq):
    nb = seq // MOBA_BLOCK
    nq = ATT_GROUP * MOBA_BLOCK
    qw = ATT_GROUP * HEAD_DIM
    return pl.pallas_call(
        _moba_prompt_kernel,
        out_shape=jax.ShapeDtypeStruct((bsz * seq, ATT_Q_W), BF16),
        grid=(bsz, ATT_KV_HEADS, nb),
        in_specs=[pl.BlockSpec((MOBA_BLOCK, qw), lambda b, h, i: (b * nb + i, COL_Q // qw + h)),
                  pl.BlockSpec((seq, HEAD_DIM), lambda b, h, i: (b, COL_K // HEAD_DIM + h)),
                  pl.BlockSpec((seq, HEAD_DIM), lambda b, h, i: (b, COL_V // HEAD_DIM + h))],
        out_specs=pl.BlockSpec((MOBA_BLOCK, qw), lambda b, h, i: (b * nb + i, h)),
        scratch_shapes=[pltpu.VMEM((nb, MOBA_BLOCK, HEAD_DIM), BF16),
                        pltpu.VMEM((nb, HEAD_DIM, MOBA_BLOCK), BF16),
                        pltpu.VMEM((2 * SUBLANES, HEAD_DIM), F32),
                        pltpu.VMEM((nb, 1, nq), F32),
                        pltpu.VMEM((HEAD_DIM, nq), F32),
                        pltpu.VMEM((1, nq), F32),
                        pltpu.VMEM((1, nq), F32)],
        compiler_params=_params(("parallel", "parallel", "arbitrary"), 32),
        name="moba_prompt",
    )(z, z, z)


def _moba_sample_kernel(n_pages, pt_ref, q_ref, kn_ref, vn_ref, *rest):
    del pt_ref
    kp = rest[:n_pages]
    vp = rest[n_pages:2 * n_pages]
    o_ref = rest[2 * n_pages]
    page = kp[0].shape[2]
    per_blk = MOBA_BLOCK // page
    nb = n_pages // per_blk
    kvw = ATT_KV_W

    q = q_ref[0]
    r8 = lax.broadcasted_iota(I32, (ATT_KV_HEADS, kvw), 0)
    lh = lax.broadcasted_iota(I32, (ATT_KV_HEADS, kvw), 1) // HEAD_DIM
    bd = r8 == lh
    parts = []
    for g in range(ATT_GROUP):
        qg = jnp.concatenate([q[:, (ATT_GROUP * h + g) * HEAD_DIM:(ATT_GROUP * h + g + 1) * HEAD_DIM]
                              for h in range(ATT_KV_HEADS)], axis=1)
        parts.append(jnp.where(bd, qg, 0.0))
    wq = jnp.concatenate(parts, axis=0)
    wqb = wq.astype(BF16)
    bd2 = jnp.concatenate([bd] * ATT_GROUP, axis=0)

    scores, ksum = [], []
    for p in range(n_pages):
        kpg = kp[p][0, 0]
        scores.append(_dot_nt(wqb, kpg.astype(BF16)) * ATT_SCALE)
        ksum.append(jnp.sum(kpg, axis=0, keepdims=True))
    gates = []
    for j in range(nb):
        kmean = sum(ksum[j * per_blk:(j + 1) * per_blk]) * (1.0 / MOBA_BLOCK)
        gates.append(jnp.sum(wq * kmean, axis=1, keepdims=True))
    sel = []
    for j in range(nb):
        rank = jnp.zeros(gates[j].shape, F32)
        for jp in range(nb):
            if jp < j:
                rank = rank + (gates[jp] >= gates[j]).astype(F32)
            elif jp > j:
                rank = rank + (gates[jp] > gates[j]).astype(F32)
        sel.append(rank < MOBA_TOPK)

    s_new = jnp.sum(wq * kn_ref[0], axis=1, keepdims=True) * ATT_SCALE
    m = s_new
    for p in range(n_pages):
        scores[p] = jnp.where(sel[p // per_blk], scores[p], -jnp.inf)
        m = jnp.maximum(m, jnp.max(scores[p], axis=1, keepdims=True))
    p_new = jnp.exp(s_new - m)
    l = p_new
    acc = p_new * vn_ref[0]
    for p in range(n_pages):
        pp = jnp.exp(scores[p] - m)
        l = l + jnp.sum(pp, axis=1, keepdims=True)
        acc = acc + _dot(pp.astype(BF16), vp[p][0, 0].astype(BF16))
    out = jnp.where(bd2, acc / l, 0.0)
    per_g = [jnp.sum(out[g * ATT_KV_HEADS:(g + 1) * ATT_KV_HEADS], axis=0, keepdims=True)
             for g in range(ATT_GROUP)]
    pieces = []
    for h in range(ATT_KV_HEADS):
        for g in range(ATT_GROUP):
            pieces.append(per_g[g][:, h * HEAD_DIM:(h + 1) * HEAD_DIM])
    o_ref[0] = jnp.concatenate(pieces, axis=1).astype(o_ref.dtype)


def _moba_sample(zs3, cache_k4, cache_v4, page_table, layer):
    ns = zs3.shape[0]
    n_pages = page_table.shape[1]
    page, kvw = cache_k4.shape[2], cache_k4.shape[3]
    pt_flat = page_table.reshape(-1).astype(I32)

    def page_spec(p):
        return pl.BlockSpec((1, 1, page, kvw), lambda s, pt: (layer, pt[s * n_pages + p], 0, 0))

    in_specs = ([pl.BlockSpec((1, 1, ATT_Q_W), lambda s, pt: (s, 0, COL_Q // ATT_Q_W)),
                 pl.BlockSpec((1, 1, kvw), lambda s, pt: (s, 0, COL_K // kvw)),
                 pl.BlockSpec((1, 1, kvw), lambda s, pt: (s, 0, COL_V // kvw))]
                + [page_spec(p) for p in range(n_pages)] * 2)
    grid_spec = pltpu.PrefetchScalarGridSpec(
        num_scalar_prefetch=1, grid=(ns,), in_specs=in_specs,
        out_specs=pl.BlockSpec((1, 1, ATT_Q_W), lambda s, pt: (s, 0, 0)))
    return pl.pallas_call(
        functools.partial(_moba_sample_kernel, n_pages),
        out_shape=jax.ShapeDtypeStruct((ns, 1, ATT_Q_W), BF16),
        grid_spec=grid_spec,
        compiler_params=_params(("parallel",), 48),
        name="moba_sample",
    )(pt_flat, zs3, zs3, zs3, *([cache_k4] * n_pages), *([cache_v4] * n_pages))


def _mlstm_prompt_kernel(q_ref, k_ref, v_ref, og_ref, igr_ref, fgr_ref, igc_ref, fgc_ref, ng_ref,
                         y_ref, c_ref, n_ref, m_ref):
    @pl.when(pl.program_id(2) == 0)
    def _():
        c_ref[...] = jnp.zeros(c_ref.shape, F32)
        n_ref[...] = jnp.zeros(n_ref.shape, F32)
        m_ref[...] = jnp.zeros(m_ref.shape, F32)

    L = q_ref.shape[0]
    q = q_ref[...] * M_Q_SCALE
    k = k_ref[...]
    ig_r = igr_ref[0, 0]
    lf_r = _log_sigmoid(fgr_ref[0, 0])
    ig_c = igc_ref[0, 0]
    lf_c = _log_sigmoid(fgc_ref[0, 0])
    ri = lax.broadcasted_iota(I32, (L, L), 0)
    ci = lax.broadcasted_iota(I32, (L, L), 1)
    tril = ci <= ri
    b_c = jnp.sum(jnp.where(tril, lf_r, 0.0), axis=1, keepdims=True)
    b_r = jnp.sum(jnp.where(ri <= ci, lf_c, 0.0), axis=0, keepdims=True)
    r_r = ig_r - b_r
    m0 = m_ref[0, 0][:, 0:1]
    m_c = b_c + jnp.maximum(m0, jnp.max(jnp.where(tril, r_r, -jnp.inf), axis=1, keepdims=True))
    dmat = jnp.exp(jnp.where(tril, b_c - m_c + r_r, -jnp.inf))
    qb = q.astype(BF16)
    vb = v_ref[...].astype(BF16)
    s = _dot_nt(qb, k.astype(BF16)) * dmat
    inter = jnp.exp(b_c + m0 - m_c)
    c0 = c_ref[0, 0]
    n0 = n_ref[0, 0]
    num = inter * _dot(qb, c0.astype(BF16)) + _dot(s.astype(BF16), vb)
    den = inter * jnp.sum(q * n0, axis=1, keepdims=True) + jnp.sum(s, axis=1, keepdims=True)
    h = num / jnp.maximum(jnp.abs(den), jnp.exp(-m_c))
    hn = h * lax.rsqrt(jnp.mean(h * h, axis=1, keepdims=True) + NORM_EPS)
    y_ref[...] = (hn * ng_ref[...] * jax.nn.sigmoid(og_ref[...])).astype(y_ref.dtype)

    b_last = b_c[L - 1:L, :]
    m_last = m_c[L - 1:L, :]
    kw = k * jnp.exp(b_last - b_c + ig_c - m_last)
    decay = jnp.exp(b_last + m0 - m_last)
    c_ref[0, 0] = decay * c0 + _dot(kw.T.astype(BF16), vb)
    n_ref[0, 0] = decay * n0 + jnp.sum(kw, axis=0, keepdims=True)
    m_ref[0, 0] = jnp.broadcast_to(m_last, (1, LANES))


def _mlstm_prompt(z, g_all, norm_g, bsz, seq):
    L = _tile(seq, (256, 128))
    nc = seq // L
    gp = jnp.transpose(g_all[:bsz * seq, :2 * M_HEADS].reshape(bsz, seq, 2 * M_HEADS), (0, 2, 1))
    g_row = gp.reshape(bsz, 2 * M_HEADS, 1, seq)
    g_col = gp.reshape(bsz, 2 * M_HEADS, seq, 1)
    row = lambda off: pl.BlockSpec((1, 1, 1, L), lambda b, h, c: (b, off + h, 0, c))
    col = lambda off: pl.BlockSpec((1, 1, L, 1), lambda b, h, c: (b, off + h, c, 0))
    state = lambda shape: pl.BlockSpec((1, 1) + shape, lambda b, h, c: (b, h, 0, 0))
    return pl.pallas_call(
        _mlstm_prompt_kernel,
        out_shape=(jax.ShapeDtypeStruct((bsz * seq, M_V_W), BF16),
                   jax.ShapeDtypeStruct((bsz, M_HEADS, M_QK, M_V), F32),
                   jax.ShapeDtypeStruct((bsz, M_HEADS, 1, M_QK), F32),
                   jax.ShapeDtypeStruct((bsz, M_HEADS, 1, LANES), F32)),
        grid=(bsz, M_HEADS, nc),
        in_specs=[pl.BlockSpec((L, M_QK), lambda b, h, c: (b * nc + c, COL_QM // M_QK + h)),
                  pl.BlockSpec((L, M_QK), lambda b, h, c: (b * nc + c, COL_KM // M_QK + h)),
                  pl.BlockSpec((L, M_V), lambda b, h, c: (b * nc + c, COL_VM // M_V + h)),
                  pl.BlockSpec((L, M_V), lambda b, h, c: (b * nc + c, COL_OM // M_V + h)),
                  row(0), row(M_HEADS), col(0), col(M_HEADS),
                  pl.BlockSpec((1, M_V), lambda b, h, c: (0, h))],
        out_specs=(pl.BlockSpec((L, M_V), lambda b, h, c: (b * nc + c, h)),
                   state((M_QK, M_V)), state((1, M_QK)), state((1, LANES))),
        compiler_params=_params(("parallel", "parallel", "arbitrary"), 32),
        name="mlstm_prompt",
    )(z, z, z, z, g_row, g_row, g_col, g_col, norm_g)


def _mlstm_sample_kernel(qr_ref, kr_ref, vr_ref, or_ref, qc_ref, kc_ref, g_ref, m0_ref, n0_ref, c0_ref, ng_ref,
                         y_ref, c_ref, n_ref, m_ref):
    g = g_ref[0]
    m0_all = m0_ref[0]
    lane = lax.broadcasted_iota(I32, (1, LANES), 1)
    m_out = jnp.zeros((1, LANES), F32)
    for h in range(M_HEADS):
        ig = g[:, h:h + 1]
        lf = _log_sigmoid(g[:, M_HEADS + h:M_HEADS + h + 1])
        m0 = m0_all[:, h:h + 1]
        m = jnp.maximum(lf + m0, ig)
        w = jnp.exp(ig - m)
        decay = jnp.exp(lf + m0 - m)
        q_r = qr_ref[0, h] * M_Q_SCALE
        k_r = kr_ref[0, h]
        v_r = vr_ref[0, h]
        q_c = qc_ref[0, h] * M_Q_SCALE
        c0 = c0_ref[0, 0, h]
        n0 = n0_ref[0, 0, h]
        s = jnp.sum(q_r * k_r, axis=1, keepdims=True) * w
        num = decay * jnp.sum(q_c * c0, axis=0, keepdims=True) + s * v_r
        den = decay * jnp.sum(q_r * n0, axis=1, keepdims=True) + s
        hh = num / jnp.maximum(jnp.abs(den), jnp.exp(-m))
        hn = hh * lax.rsqrt(jnp.mean(hh * hh, axis=1, keepdims=True) + NORM_EPS)
        y = hn * ng_ref[:, h * M_V:(h + 1) * M_V] * jax.nn.sigmoid(or_ref[0, h])
        y_ref[0, :, h * M_V:(h + 1) * M_V] = y.astype(y_ref.dtype)
        c_ref[0, h] = decay * c0 + (w * kc_ref[0, h]) * v_r
        n_ref[0, h] = decay * n0 + w * k_r
        m_out = jnp.where(lane == h, m, m_out)
    m_ref[0] = m_out


def _mlstm_sample(zs, gs, state_c, state_n, state_m, norm_g, layer):
    ns = zs.shape[0]
    qm = zs[:, COL_QM:COL_QM + M_QK_W]
    km = zs[:, COL_KM:COL_KM + M_QK_W]
    rows = lambda a, w: a.reshape(ns, M_HEADS, 1, w)
    m0 = jnp.pad(state_m[layer], ((0, 0), (0, LANES - M_HEADS))).reshape(ns, 1, LANES)
    row_spec = lambda w: pl.BlockSpec((1, M_HEADS, 1, w), lambda s: (s, 0, 0, 0))
    lane_spec = pl.BlockSpec((1, 1, LANES), lambda s: (s, 0, 0))
    col_spec = pl.BlockSpec((1, M_HEADS, M_QK, 1), lambda s: (s, 0, 0, 0))
    return pl.pallas_call(
        _mlstm_sample_kernel,
        out_shape=(jax.ShapeDtypeStruct((ns, 1, M_V_W), BF16),
                   jax.ShapeDtypeStruct((ns, M_HEADS, M_QK, M_V), F32),
                   jax.ShapeDtypeStruct((ns, M_HEADS, 1, M_QK), F32),
                   jax.ShapeDtypeStruct((ns, 1, LANES), F32)),
        grid=(ns,),
        in_specs=[row_spec(M_QK), row_spec(M_QK), row_spec(M_V), row_spec(M_V), col_spec, col_spec,
                  lane_spec, lane_spec,
                  pl.BlockSpec((1, 1, M_HEADS, 1, M_QK), lambda s: (layer, s, 0, 0, 0)),
                  pl.BlockSpec((1, 1, M_HEADS, M_QK, M_V), lambda s: (layer, s, 0, 0, 0)),
                  pl.BlockSpec((1, M_V_W), lambda s: (0, 0))],
        out_specs=(pl.BlockSpec((1, 1, M_V_W), lambda s: (s, 0, 0)),
                   pl.BlockSpec((1, M_HEADS, M_QK, M_V), lambda s: (s, 0, 0, 0)),
                   row_spec(M_QK), lane_spec),
        compiler_params=_params(("parallel",), 32),
        name="mlstm_sample",
    )(rows(qm, M_QK), rows(km, M_QK), rows(zs[:, COL_VM:COL_VM + M_V_W], M_V), rows(zs[:, COL_OM:COL_OM + M_V_W], M_V),
      qm.reshape(ns, M_HEADS, M_QK, 1), km.reshape(ns, M_HEADS, M_QK, 1),
      gs.reshape(ns, 1, LANES), m0,
      state_n.reshape(state_n.shape[0], ns, M_HEADS, 1, M_QK), state_c, norm_g)


def _layernorm_chunks(chunks, d):
    total = sum(jnp.sum(c, axis=1, keepdims=True) for c in chunks)
    mu = total * (1.0 / d)
    var = sum(jnp.sum(jnp.square(c - mu), axis=1, keepdims=True) for c in chunks) * (1.0 / d)
    return mu, lax.rsqrt(var + LN_EPS)


def _merge_kernel(alpha, att_ref, mix_ref, wa_ref, wb_ref, x_ref, g_ref, b_ref, wr_ref, br_ref,
                  h_ref, ids_ref, wts_ref, acc_s, hb_s):
    j = pl.program_id(1)
    nj = acc_s.shape[0]
    tm, tn = acc_s.shape[1], acc_s.shape[2]
    d = nj * tn
    acc_s[j] = alpha * x_ref[...] + _dot(att_ref[...], wa_ref[...]) + _dot(mix_ref[...], wb_ref[...])

    @pl.when(j == nj - 1)
    def _():
        mu, rstd = _layernorm_chunks([acc_s[c] for c in range(nj)], d)
        for c in range(nj):
            y = (acc_s[c] - mu) * rstd * g_ref[:, c * tn:(c + 1) * tn] + b_ref[:, c * tn:(c + 1) * tn]
            hb_s[:, c * tn:(c + 1) * tn] = y.astype(BF16)
            h_ref[:, c * tn:(c + 1) * tn] = y

        lg = _dot(hb_s[...], wr_ref[...]) + br_ref[...]
        lane = lax.broadcasted_iota(I32, lg.shape, 1)
        lane_f = lane.astype(F32)
        far = float(LANES)
        gl = jnp.where(lane < N_GROUPS, lg, -jnp.inf)
        gmax = jnp.max(gl, axis=1, keepdims=True)
        gsel = jnp.min(jnp.where(gl == gmax, lane_f, far), axis=1, keepdims=True).astype(I32)
        gw = 1.0 / jnp.sum(jnp.where(lane < N_GROUPS, jnp.exp(gl - gmax), 0.0), axis=1, keepdims=True)
        e_lane = lane - N_GROUPS
        in_grp = (e_lane >= 0) & (e_lane < N_EXPERTS) & ((e_lane // EXPERTS_PER_GROUP) == gsel)
        el = jnp.where(in_grp, lg, -jnp.inf)
        t1 = jnp.max(el, axis=1, keepdims=True)
        i1 = jnp.min(jnp.where(el == t1, lane_f, far), axis=1, keepdims=True)
        el2 = jnp.where(lane_f == i1, -jnp.inf, el)
        t2 = jnp.max(el2, axis=1, keepdims=True)
        i2 = jnp.min(jnp.where(el2 == t2, lane_f, far), axis=1, keepdims=True)
        e21 = jnp.exp(t2 - t1)
        w1 = gw / (1.0 + e21)
        w2 = w1 * e21
        ids = jnp.where(lane == 0, i1, jnp.where(lane == 1, i2, float(N_GROUPS))) - float(N_GROUPS)
        ids_ref[...] = ids.astype(I32)
        wts_ref[...] = jnp.where(lane == 0, w1, jnp.where(lane == 1, w2, 0.0))


def _merge(att, mix, wob, x, ln_g, ln_b, wr, br, alpha):
    n, d = x.shape
    half = att.shape[1]
    tm = _tile(n, (320, 256, 128, 64))
    tn = _tile(d, (1024, 512))
    nj = d // tn
    rows = lambda w: pl.BlockSpec((1, w), lambda i, j: (0, 0))
    return pl.pallas_call(
        functools.partial(_merge_kernel, alpha),
        out_shape=(jax.ShapeDtypeStruct((n, d), F32),
                   jax.ShapeDtypeStruct((n, LANES), I32),
                   jax.ShapeDtypeStruct((n, LANES), F32)),
        grid=(n // tm, nj),
        in_specs=[pl.BlockSpec((tm, half), lambda i, j: (i, 0)),
                  pl.BlockSpec((tm, half), lambda i, j: (i, 0)),
                  pl.BlockSpec((half, tn), lambda i, j: (0, j)),
                  pl.BlockSpec((half, tn), lambda i, j: (1, j)),
                  pl.BlockSpec((tm, tn), lambda i, j: (i, j)),
                  rows(d), rows(d),
                  pl.BlockSpec((d, LANES), lambda i, j: (0, 0)),
                  rows(LANES)],
        out_specs=(pl.BlockSpec((tm, d), lambda i, j: (i, 0)),
                   pl.BlockSpec((tm, LANES), lambda i, j: (i, 0)),
                   pl.BlockSpec((tm, LANES), lambda i, j: (i, 0))),
        scratch_shapes=[pltpu.VMEM((nj, tm, tn), F32), pltpu.VMEM((tm, d), BF16)],
        compiler_params=_params(("parallel", "arbitrary"), 48),
        name="merge_ln_router",
    )(att, mix, wob, wob, x, ln_g, ln_b, wr, br)


def _gather_kernel(nc_ref, idx_ref, src_ref, out_ref, sem):
    c = pl.program_id(0)
    ch = out_ref.shape[0]

    def copy(r):
        return pltpu.make_async_copy(src_ref.at[pl.ds(idx_ref[c * ch + r], 1), :], out_ref.at[pl.ds(r, 1), :], sem)

    @pl.when(c < nc_ref[0])
    def _():
        def issue(r, carry):
            copy(r).start()
            return carry

        lax.fori_loop(0, ch, issue, 0, unroll=GATHER_UNROLL)

        def drain(r, carry):
            copy(r).wait()
            return carry

        lax.fori_loop(0, ch, drain, 0, unroll=GATHER_UNROLL)


def _row_gather(src, idx, n_chunks):
    p = idx.shape[0]
    w = src.shape[1]
    ch = GATHER_CHUNK
    grid_spec = pltpu.PrefetchScalarGridSpec(
        num_scalar_prefetch=2, grid=(p // ch,),
        in_specs=[pl.BlockSpec(memory_space=pl.ANY)],
        out_specs=pl.BlockSpec((ch, w), lambda c, nc, idx: (jnp.minimum(c, nc[0] - 1), 0)),
        scratch_shapes=[pltpu.SemaphoreType.DMA(())])
    return pl.pallas_call(
        _gather_kernel,
        out_shape=jax.ShapeDtypeStruct((p, w), src.dtype),
        grid_spec=grid_spec,
        compiler_params=_params(("arbitrary",), 16),
        name="row_gather",
    )(n_chunks, idx, src)


def _moe_kernel(te_ref, ti_ref, nu_ref, x_ref, wg_ref, wu_ref, wd_ref, o_ref, xb_s):
    del te_ref, ti_ref
    t = pl.program_id(0)
    j = pl.program_id(1)
    fc = wg_ref.shape[3]

    @pl.when(t < nu_ref[0])
    def _():
        @pl.when(j == 0)
        def _():
            xb_s[...] = x_ref[...].astype(BF16)

        wgu = jnp.concatenate([wg_ref[0, 0].astype(BF16), wu_ref[0, 0].astype(BF16)], axis=1)
        hgu = _dot(xb_s[...], wgu)
        hg = hgu[:, :fc]
        a = (hg * jax.nn.sigmoid(hg) * hgu[:, fc:]).astype(BF16)
        part = _dot(a, wd_ref[0, 0].astype(BF16))

        @pl.when(j == 0)
        def _():
            o_ref[...] = part

        @pl.when(j != 0)
        def _():
            o_ref[...] = o_ref[...] + part


def _moe_experts(xs, w_gate, w_up, w_down, tile_expert, tile_index, n_used, layer):
    tm = MOE_TILE
    fc = MOE_FCHUNK
    n_tiles = tile_expert.shape[0]
    d = w_gate.shape[2]
    f = w_gate.shape[3]
    nj = f // fc

    def chunk(t, j, nu):
        return jnp.where(t < nu[0], j, nj - 1)

    grid_spec = pltpu.PrefetchScalarGridSpec(
        num_scalar_prefetch=3, grid=(n_tiles, nj),
        in_specs=[pl.BlockSpec((tm, d), lambda t, j, te, ti, nu: (ti[t], 0)),
                  pl.BlockSpec((1, 1, d, fc), lambda t, j, te, ti, nu: (layer, te[t], 0, chunk(t, j, nu))),
                  pl.BlockSpec((1, 1, d, fc), lambda t, j, te, ti, nu: (layer, te[t], 0, chunk(t, j, nu))),
                  pl.BlockSpec((1, 1, fc, d), lambda t, j, te, ti, nu: (layer, te[t], chunk(t, j, nu), 0))],
        out_specs=pl.BlockSpec((tm, d), lambda t, j, te, ti, nu: (ti[t], 0)),
        scratch_shapes=[pltpu.VMEM((tm, d), BF16)])
    return pl.pallas_call(
        _moe_kernel,
        out_shape=jax.ShapeDtypeStruct(xs.shape, F32),
        grid_spec=grid_spec,
        compiler_params=_params(("arbitrary", "arbitrary"), 56),
        name="moe_experts",
    )(tile_expert, tile_index, n_used, xs, w_gate, w_up, w_down)


def _combine_kernel(alpha, h_ref, f0_ref, f1_ref, w_ref, g_ref, b_ref, o_ref):
    d = o_ref.shape[1]
    w = w_ref[...]
    v = alpha * h_ref[...] + w[:, 0:1] * f0_ref[...] + w[:, 1:2] * f1_ref[...]
    mu = jnp.mean(v, axis=1, keepdims=True)
    vc = v - mu
    var = jnp.sum(vc * vc, axis=1, keepdims=True) * (1.0 / d)
    o_ref[...] = vc * lax.rsqrt(var + LN_EPS) * g_ref[...] + b_ref[...]


def _combine(h, ff, wts, ln_g, ln_b, alpha, row0, nrows):
    n, d = h.shape
    tm = next(c for c in (256, 128, 64, 32, 16, 8) if nrows % c == 0 and row0 % c == 0 and n % c == 0)
    o0 = row0 // tm
    tile = lambda off: pl.BlockSpec((tm, d), lambda i: (i + off, 0))
    rows = pl.BlockSpec((1, d), lambda i: (0, 0))
    return pl.pallas_call(
        functools.partial(_combine_kernel, alpha),
        out_shape=jax.ShapeDtypeStruct((nrows, d), F32),
        grid=(nrows // tm,),
        in_specs=[tile(o0), tile(o0), tile(o0 + n // tm),
                  pl.BlockSpec((tm, LANES), lambda i: (i + o0, 0)), rows, rows],
        out_specs=pl.BlockSpec((tm, d), lambda i: (i, 0)),
        compiler_params=_params(("parallel",), 40),
        name="combine_ln",
    )(h, ff, ff, wts, ln_g, ln_b)


def _routing_tables(ids, n_tiles):
    n = ids.shape[0]
    tm = MOE_TILE
    e = ids[:, :2].reshape(-1)
    onehot = (e[:, None] == jnp.arange(N_EXPERTS, dtype=I32)[None, :]).astype(I32)
    csum = jnp.cumsum(onehot, axis=0)
    rank = jnp.sum((csum - onehot) * onehot, axis=1)
    counts = csum[-1]
    tiles = (counts + tm - 1) // tm
    tile_end = jnp.cumsum(tiles)
    n_used = tile_end[-1]
    pos = ((tile_end - tiles)[e] * tm + rank).astype(I32)
    src = jnp.zeros((n_tiles * tm,), I32).at[pos].set(jnp.arange(2 * n, dtype=I32) // 2)
    t = jnp.minimum(jnp.arange(n_tiles, dtype=I32), n_used - 1)
    tile_expert = jnp.minimum(jnp.sum((tile_end[None, :] <= t[:, None]).astype(I32), axis=1), N_EXPERTS - 1)
    pos2 = pos.reshape(n, 2)
    back = jnp.concatenate([pos2[:, 0], pos2[:, 1]])
    return src, back, tile_expert, t, n_used.reshape(1).astype(I32)


def _layer(l, x, bsz, seq, cache_k, cache_v, state_c, state_n, state_m, page_table,
           w_in, b_gates, norm_g, w_out, ln1_g, ln1_b, w_group, b_group, w_router, b_router,
           w_e_gate, w_e_up, w_e_down, ln2_g, ln2_b, alpha):
    n, d = x.shape
    n_p = bsz * seq
    n_s = n - n_p

    wz = w_in[l, :, :COL_G].astype(BF16)
    wg = jnp.pad(w_in[l, :, COL_G:], ((0, 0), (0, LANES - 2 * M_HEADS)))
    bg = jnp.pad(b_gates[l], (0, LANES - 2 * M_HEADS)).reshape(1, LANES)
    z = _in_proj(x.astype(BF16), wz)
    g_all = _gates(x, wg, bg)

    ng = norm_g[l].reshape(1, M_V_W)
    att_p = _moba_prompt(z, bsz, seq)
    mix_p, c_p, n_pr, m_p = _mlstm_prompt(z, g_all, ng, bsz, seq)

    zs = z[n_p:]
    pool, page = cache_k.shape[1], cache_k.shape[2]
    att_s = _moba_sample(zs.reshape(n_s, 1, COL_G), cache_k.reshape(-1, pool, page, ATT_KV_W),
                         cache_v.reshape(-1, pool, page, ATT_KV_W), page_table, l)
    mix_s, c_s, n_sm, m_s = _mlstm_sample(zs, g_all[n_p:], state_c, state_n, state_m, ng, l)

    att = jnp.concatenate([att_p, att_s.reshape(n_s, ATT_Q_W)], axis=0)
    mix = jnp.concatenate([mix_p, mix_s.reshape(n_s, M_V_W)], axis=0)

    wr = jnp.pad(jnp.concatenate([w_group[l], w_router[l]], axis=1),
                 ((0, 0), (0, LANES - N_GROUPS - N_EXPERTS))).astype(BF16)
    br = jnp.pad(jnp.concatenate([b_group[l], b_router[l]]), (0, LANES - N_GROUPS - N_EXPERTS)).reshape(1, LANES)
    h2, ids, wts = _merge(att, mix, w_out[l].astype(BF16), x, ln1_g[l].reshape(1, d), ln1_b[l].reshape(1, d),
                          wr, br, alpha)

    n_tiles = (2 * n) // MOE_TILE + N_EXPERTS + 1
    src, back, tile_expert, tile_index, n_used = _routing_tables(ids, n_tiles)
    xs = _row_gather(h2, src, n_used * (MOE_TILE // GATHER_CHUNK))
    ys = _moe_experts(xs, w_e_gate, w_e_up, w_e_down, tile_expert, tile_index, n_used, l)
    ff = _row_gather(ys, back, jnp.full((1,), (2 * n) // GATHER_CHUNK, I32))
    g2, b2 = ln2_g[l].reshape(1, d), ln2_b[l].reshape(1, d)
    y_p = _combine(h2, ff, wts, g2, b2, alpha, 0, n_p)
    y_s = _combine(h2, ff, wts, g2, b2, alpha, n_p, n_s)

    kv = lambda rows, col: rows[:, col:col + ATT_KV_W]
    return (y_p, y_s,
            kv(z[:n_p], COL_K).reshape(bsz, seq, ATT_KV_HEADS, HEAD_DIM),
            kv(z[:n_p], COL_V).reshape(bsz, seq, ATT_KV_HEADS, HEAD_DIM),
            kv(zs, COL_K).reshape(n_s, 1, ATT_KV_HEADS, HEAD_DIM),
            kv(zs, COL_V).reshape(n_s, 1, ATT_KV_HEADS, HEAD_DIM),
            c_p, n_pr.reshape(bsz, M_HEADS, M_QK), m_p[:, :, 0, 0],
            c_s, n_sm.reshape(n_s, M_HEADS, M_QK), m_s[:, 0, :M_HEADS])


def kernel(x_prompt, x_sample, cache_k, cache_v, state_mlstm_c, state_mlstm_n, state_mlstm_m, page_table,
           w_in, b_gates, mlstm_norm_g, w_out, ln1_g, ln1_b, w_group, b_group, w_router, b_router,
           w_e_gate, w_e_up, w_e_down, ln2_g, ln2_b):
    bsz, seq, d = x_prompt.shape
    n_s, dec_seq, _ = x_sample.shape
    depth = w_in.shape[0]
    assert dec_seq == 1 and seq % MOBA_BLOCK == 0 and w_in.shape[2] == COL_G + 2 * M_HEADS
    assert (page_table.shape[1] * cache_k.shape[2]) % MOBA_BLOCK == 0 and MOBA_BLOCK % cache_k.shape[2] == 0
    alpha = (2 * depth) ** 0.25
    x = jnp.concatenate([x_prompt.reshape(bsz * seq, d), x_sample.reshape(n_s, d)], axis=0)
    per_layer = []
    for l in range(depth):
        outs = _layer(l, x, bsz, seq, cache_k, cache_v, state_mlstm_c, state_mlstm_n, state_mlstm_m, page_table,
                      w_in, b_gates, mlstm_norm_g, w_out, ln1_g, ln1_b, w_group, b_group, w_router, b_router,
                      w_e_gate, w_e_up, w_e_down, ln2_g, ln2_b, alpha)
        x = jnp.concatenate([outs[0], outs[1]], axis=0)
        per_layer.append(outs[2:])
    stacked = [jnp.stack([p[i] for p in per_layer]) for i in range(10)]
    return (outs[0].reshape(bsz, seq, d), outs[1].reshape(n_s, 1, d), *stacked)
```

```python
import functools

import jax
import jax.numpy as jnp
from jax import lax
from jax.experimental import pallas as pl
from jax.experimental.pallas import tpu as pltpu

F32, BF16, I32 = jnp.float32, jnp.bfloat16, jnp.int32

SUBLANES = 8
LANES = 128
MIB = 1024 * 1024

HEAD_DIM = 128
ATT_HEADS = 16
ATT_KV_HEADS = 8
ATT_GROUP = ATT_HEADS // ATT_KV_HEADS
ATT_SCALE = HEAD_DIM ** -0.5
MOBA_BLOCK = 256
MOBA_TOPK = 3
M_HEADS = 4
M_QK = 256
M_V = 512
M_Q_SCALE = M_QK ** -0.5
N_GROUPS = 4
EXPERTS_PER_GROUP = 8
N_EXPERTS = N_GROUPS * EXPERTS_PER_GROUP
LN_EPS = 1e-5
NORM_EPS = 1e-6

ATT_Q_W = ATT_HEADS * HEAD_DIM
ATT_KV_W = ATT_KV_HEADS * HEAD_DIM
M_QK_W = M_HEADS * M_QK
M_V_W = M_HEADS * M_V
COL_Q = 0
COL_K = COL_Q + ATT_Q_W
COL_V = COL_K + ATT_KV_W
COL_QM = COL_V + ATT_KV_W
COL_KM = COL_QM + M_QK_W
COL_VM = COL_KM + M_QK_W
COL_OM = COL_VM + M_V_W
COL_G = COL_OM + M_V_W

MOE_TILE = 512
MOE_FCHUNK = 128
GATHER_CHUNK = 256
GATHER_UNROLL = 8


def _params(sem, vmem_mib):
    return pltpu.CompilerParams(dimension_semantics=sem, vmem_limit_bytes=vmem_mib * MIB)


def _dot(a, b):
    return jnp.dot(a, b, preferred_element_type=F32)


def _dot_nt(a, b):
    return lax.dot_general(a, b, (((1,), (1,)), ((), ())), preferred_element_type=F32)


def _log_sigmoid(x):
    return jnp.minimum(x, 0.0) - jnp.log1p(jnp.exp(-jnp.abs(x)))


def _tile(n, candidates):
    for c in candidates:
        if n % c == 0:
            return c
    raise ValueError(f"no tile for {n} in {candidates}")


def _matmul_kernel(x_ref, w_ref, o_ref):
    o_ref[...] = _dot(x_ref[...], w_ref[...])


def _in_proj(xb, wb):
    n, d = xb.shape
    w = wb.shape[1]
    tm = _tile(n, (832, 640, 512, 256, 128, 64))
    tn = _tile(w, (1024, 512, 256, 128))
    return pl.pallas_call(
        _matmul_kernel,
        out_shape=jax.ShapeDtypeStruct((n, w), F32),
        grid=(w // tn, n // tm),
        in_specs=[pl.BlockSpec((tm, d), lambda j, i: (i, 0)),
                  pl.BlockSpec((d, tn), lambda j, i: (0, j))],
        out_specs=pl.BlockSpec((tm, tn), lambda j, i: (i, j)),
        compiler_params=_params(("parallel", "parallel"), 48),
        name="in_proj",
    )(xb, wb)


def _gates_kernel(x_ref, w_ref, b_ref, o_ref):
    x = x_ref[...]
    w = w_ref[...]
    xh = x.astype(BF16)
    xl = (x - xh.astype(F32)).astype(BF16)
    wh = w.astype(BF16)
    wl = (w - wh.astype(F32)).astype(BF16)
    o_ref[...] = _dot(xh, wh) + _dot(xl, wh) + _dot(xh, wl) + b_ref[...]


def _gates(x, wg, bg):
    n, d = x.shape
    tm = _tile(n, (416, 320, 256, 128, 64))
    return pl.pallas_call(
        _gates_kernel,
        out_shape=jax.ShapeDtypeStruct((n, LANES), F32),
        grid=(n // tm,),
        in_specs=[pl.BlockSpec((tm, d), lambda i: (i, 0)),
                  pl.BlockSpec((d, LANES), lambda i: (0, 0)),
                  pl.BlockSpec((1, LANES), lambda i: (0, 0))],
        out_specs=pl.BlockSpec((tm, LANES), lambda i: (i, 0)),
        compiler_params=_params(("parallel",), 40),
        name="gates",
    )(x, wg, bg)


def _moba_prompt_kernel(q_ref, k_ref, v_ref, o_ref, kb_s, vt_s, km_s, sel_s, acc_s, m_s, l_s):
    i = pl.program_id(2)
    nb = kb_s.shape[0]
    blk = MOBA_BLOCK
    nq = ATT_GROUP * blk

    @pl.when(i == 0)
    def _():
        for j in range(nb):
            kj = k_ref[j * blk:(j + 1) * blk, :]
            kb_s[j] = kj.astype(BF16)
            km_s[j:j + 1, :] = jnp.mean(kj, axis=0, keepdims=True)
            vt_s[j] = v_ref[j * blk:(j + 1) * blk, :].T.astype(BF16)
        km_s[nb:, :] = jnp.zeros((km_s.shape[0] - nb, HEAD_DIM), F32)

    q = q_ref[...]
    q2 = jnp.concatenate([q[:, g * HEAD_DIM:(g + 1) * HEAD_DIM] for g in range(ATT_GROUP)], axis=0).astype(BF16)

    gate = _dot_nt(km_s[...].astype(BF16), q2)
    jj = lax.broadcasted_iota(I32, gate.shape, 0)
    g = jnp.where(jj < i, gate, -jnp.inf)
    sel = jnp.zeros(gate.shape, F32)
    for _ in range(MOBA_TOPK):
        mx = jnp.max(g, axis=0, keepdims=True)
        first = jnp.min(jnp.where(g == mx, jj, 2 * nb), axis=0, keepdims=True)
        pick = (jj == first) & (mx > -jnp.inf)
        sel = jnp.where(pick, 1.0, sel)
        g = jnp.where(pick, -jnp.inf, g)
    for j in range(nb):
        sel_s[j] = sel[j:j + 1, :]

    m_s[...] = jnp.full(m_s.shape, -jnp.inf, F32)
    l_s[...] = jnp.zeros(l_s.shape, F32)
    acc_s[...] = jnp.zeros(acc_s.shape, F32)

    def attend(j, mask):
        s = _dot_nt(kb_s[j], q2) * ATT_SCALE
        s = jnp.where(mask, s, -jnp.inf)
        m_old = m_s[...]
        m_new = jnp.maximum(m_old, jnp.max(s, axis=0, keepdims=True))
        alpha = jnp.exp(m_old - m_new)
        p = jnp.exp(s - m_new)
        l_s[...] = alpha * l_s[...] + jnp.sum(p, axis=0, keepdims=True)
        acc_s[...] = alpha * acc_s[...] + _dot(vt_s[j], p.astype(BF16))
        m_s[...] = m_new

    ri = lax.broadcasted_iota(I32, (blk, nq), 0)
    ci = lax.broadcasted_iota(I32, (blk, nq), 1)
    attend(i, ri <= (ci & (blk - 1)))

    def past(j, carry):
        attend(j, sel_s[j] > 0.5)
        return carry

    lax.fori_loop(0, i, past, 0)

    out = (acc_s[...] / l_s[...]).T
    for g_i in range(ATT_GROUP):
        o_ref[:, g_i * HEAD_DIM:(g_i + 1) * HEAD_DIM] = out[g_i * blk:(g_i + 1) * blk, :].astype(o_ref.dtype)


def _moba_prompt(z, bsz, seq):
    nb = seq // MOBA_BLOCK
    nq = ATT_GROUP * MOBA_BLOCK
    qw = ATT_GROUP * HEAD_DIM
    return pl.pallas_call(
        _moba_prompt_kernel,
        out_shape=jax.ShapeDtypeStruct((bsz * seq, ATT_Q_W), BF16),
        grid=(bsz, ATT_KV_HEADS, nb),
        in_specs=[pl.BlockSpec((MOBA_BLOCK, qw), lambda b, h, i: (b * nb + i, COL_Q // qw + h)),
                  pl.BlockSpec((seq, HEAD_DIM), lambda b, h, i: (b, COL_K // HEAD_DIM + h)),
                  pl.BlockSpec((seq, HEAD_DIM), lambda b, h, i: (b, COL_V // HEAD_DIM + h))],
        out_specs=pl.BlockSpec((MOBA_BLOCK, qw), lambda b, h, i: (b * nb + i, h)),
        scratch_shapes=[pltpu.VMEM((nb, MOBA_BLOCK, HEAD_DIM), BF16),
                        pltpu.VMEM((nb, HEAD_DIM, MOBA_BLOCK), BF16),
                        pltpu.VMEM((2 * SUBLANES, HEAD_DIM), F32),
                        pltpu.VMEM((nb, 1, nq), F32),
                        pltpu.VMEM((HEAD_DIM, nq), F32),
                        pltpu.VMEM((1, nq), F32),
                        pltpu.VMEM((1, nq), F32)],
        compiler_params=_params(("parallel", "parallel", "arbitrary"), 32),
        name="moba_prompt",
    )(z, z, z)


def _moba_sample_kernel(n_pages, pt_ref, q_ref, kn_ref, vn_ref, *rest):
    del pt_ref
    kp = rest[:n_pages]
    vp = rest[n_pages:2 * n_pages]
    o_ref = rest[2 * n_pages]
    page = kp[0].shape[2]
    per_blk = MOBA_BLOCK // page
    nb = n_pages // per_blk
    nrow = page * ATT_KV_HEADS
    both = lambda a: jnp.concatenate([a] * ATT_GROUP, axis=0)

    q = q_ref[0]
    qb = q.astype(BF16)
    ri = lax.broadcasted_iota(I32, (ATT_HEADS, nrow), 0)
    ci = lax.broadcasted_iota(I32, (ATT_HEADS, nrow), 1)
    own = (ci % ATT_KV_HEADS) == (ri % ATT_KV_HEADS)

    scores, ksum = [], []
    for p in range(n_pages):
        k3 = kp[p][0, 0]
        scores.append(_dot_nt(qb, k3.reshape(nrow, HEAD_DIM).astype(BF16)) * ATT_SCALE)
        ksum.append(jnp.sum(k3, axis=0))
    gates = []
    for j in range(nb):
        kmean = sum(ksum[j * per_blk:(j + 1) * per_blk]) * (1.0 / MOBA_BLOCK)
        gates.append(jnp.sum(q * both(kmean), axis=1, keepdims=True))
    sel = []
    for j in range(nb):
        rank = jnp.zeros(gates[j].shape, F32)
        for jp in range(nb):
            if jp < j:
                rank = rank + (gates[jp] >= gates[j]).astype(F32)
            elif jp > j:
                rank = rank + (gates[jp] > gates[j]).astype(F32)
        sel.append(rank < MOBA_TOPK)

    s_new = jnp.sum(q * both(kn_ref[0]), axis=1, keepdims=True) * ATT_SCALE
    m = s_new
    for p in range(n_pages):
        scores[p] = jnp.where(sel[p // per_blk], jnp.where(own, scores[p], -jnp.inf), -jnp.inf)
        m = jnp.maximum(m, jnp.max(scores[p], axis=1, keepdims=True))
    p_new = jnp.exp(s_new - m)
    l = p_new
    acc = p_new * both(vn_ref[0])
    for p in range(n_pages):
        pp = jnp.exp(scores[p] - m)
        l = l + jnp.sum(pp, axis=1, keepdims=True)
        acc = acc + _dot(pp.astype(BF16), vp[p][0, 0].reshape(nrow, HEAD_DIM).astype(BF16))
    o_ref[0] = (acc / l).astype(o_ref.dtype)


def _moba_sample(zs, cache_k, cache_v, page_table, layer):
    ns = zs.shape[0]
    n_pages = page_table.shape[1]
    page = cache_k.shape[2]
    pt_flat = page_table.reshape(-1).astype(I32)
    qg = jnp.transpose(zs[:, COL_Q:COL_Q + ATT_Q_W].reshape(ns, ATT_KV_HEADS, ATT_GROUP, HEAD_DIM), (0, 2, 1, 3))
    kv_new = lambda col: zs[:, col:col + ATT_KV_W].reshape(ns, ATT_KV_HEADS, HEAD_DIM)

    def page_spec(p):
        return pl.BlockSpec((1, 1, page, ATT_KV_HEADS, HEAD_DIM),
                            lambda s, pt: (layer, pt[s * n_pages + p], 0, 0, 0))

    rows = lambda r: pl.BlockSpec((1, r, HEAD_DIM), lambda s, pt: (s, 0, 0))
    grid_spec = pltpu.PrefetchScalarGridSpec(
        num_scalar_prefetch=1, grid=(ns,),
        in_specs=[rows(ATT_HEADS), rows(ATT_KV_HEADS), rows(ATT_KV_HEADS)] + [page_spec(p) for p in range(n_pages)] * 2,
        out_specs=rows(ATT_HEADS))
    out = pl.pallas_call(
        functools.partial(_moba_sample_kernel, n_pages),
        out_shape=jax.ShapeDtypeStruct((ns, ATT_HEADS, HEAD_DIM), BF16),
        grid_spec=grid_spec,
        compiler_params=_params(("parallel",), 48),
        name="moba_sample",
    )(pt_flat, qg.reshape(ns, ATT_HEADS, HEAD_DIM), kv_new(COL_K), kv_new(COL_V),
      *([cache_k] * n_pages), *([cache_v] * n_pages))
    return jnp.transpose(out.reshape(ns, ATT_GROUP, ATT_KV_HEADS, HEAD_DIM), (0, 2, 1, 3)).reshape(ns, ATT_Q_W)


def _mlstm_prompt_kernel(q_ref, k_ref, v_ref, og_ref, igr_ref, fgr_ref, igc_ref, fgc_ref, ng_ref,
                         y_ref, c_ref, n_ref, m_ref):
    @pl.when(pl.program_id(2) == 0)
    def _():
        c_ref[...] = jnp.zeros(c_ref.shape, F32)
        n_ref[...] = jnp.zeros(n_ref.shape, F32)
        m_ref[...] = jnp.zeros(m_ref.shape, F32)

    L = q_ref.shape[0]
    q = q_ref[...] * M_Q_SCALE
    k = k_ref[...]
    ig_r = igr_ref[0, 0]
    lf_r = _log_sigmoid(fgr_ref[0, 0])
    ig_c = igc_ref[0, 0]
    lf_c = _log_sigmoid(fgc_ref[0, 0])
    ri = lax.broadcasted_iota(I32, (L, L), 0)
    ci = lax.broadcasted_iota(I32, (L, L), 1)
    tril = ci <= ri
    b_c = jnp.sum(jnp.where(tril, lf_r, 0.0), axis=1, keepdims=True)
    b_r = jnp.sum(jnp.where(ri <= ci, lf_c, 0.0), axis=0, keepdims=True)
    r_r = ig_r - b_r
    m0 = m_ref[0, 0][:, 0:1]
    m_c = b_c + jnp.maximum(m0, jnp.max(jnp.where(tril, r_r, -jnp.inf), axis=1, keepdims=True))
    dmat = jnp.exp(jnp.where(tril, b_c - m_c + r_r, -jnp.inf))
    qb = q.astype(BF16)
    vb = v_ref[...].astype(BF16)
    s = _dot_nt(qb, k.astype(BF16)) * dmat
    inter = jnp.exp(b_c + m0 - m_c)
    c0 = c_ref[0, 0]
    n0 = n_ref[0, 0]
    num = inter * _dot(qb, c0.astype(BF16)) + _dot(s.astype(BF16), vb)
    den = inter * jnp.sum(q * n0, axis=1, keepdims=True) + jnp.sum(s, axis=1, keepdims=True)
    h = num / jnp.maximum(jnp.abs(den), jnp.exp(-m_c))
    hn = h * lax.rsqrt(jnp.mean(h * h, axis=1, keepdims=True) + NORM_EPS)
    y_ref[...] = (hn * ng_ref[...] * jax.nn.sigmoid(og_ref[...])).astype(y_ref.dtype)

    b_last = b_c[L - 1:L, :]
    m_last = m_c[L - 1:L, :]
    kw = k * jnp.exp(b_last - b_c + ig_c - m_last)
    decay = jnp.exp(b_last + m0 - m_last)
    c_ref[0, 0] = decay * c0 + _dot(kw.T.astype(BF16), vb)
    n_ref[0, 0] = decay * n0 + jnp.sum(kw, axis=0, keepdims=True)
    m_ref[0, 0] = jnp.broadcast_to(m_last, (1, LANES))


def _mlstm_prompt(z, g_all, norm_g, bsz, seq):
    L = _tile(seq, (256, 128))
    nc = seq // L
    gp = jnp.transpose(g_all[:bsz * seq, :2 * M_HEADS].reshape(bsz, seq, 2 * M_HEADS), (0, 2, 1))
    g_row = gp.reshape(bsz, 2 * M_HEADS, 1, seq)
    g_col = gp.reshape(bsz, 2 * M_HEADS, seq, 1)
    row = lambda off: pl.BlockSpec((1, 1, 1, L), lambda b, h, c: (b, off + h, 0, c))
    col = lambda off: pl.BlockSpec((1, 1, L, 1), lambda b, h, c: (b, off + h, c, 0))
    state = lambda shape: pl.BlockSpec((1, 1) + shape, lambda b, h, c: (b, h, 0, 0))
    return pl.pallas_call(
        _mlstm_prompt_kernel,
        out_shape=(jax.ShapeDtypeStruct((bsz * seq, M_V_W), BF16),
                   jax.ShapeDtypeStruct((bsz, M_HEADS, M_QK, M_V), F32),
                   jax.ShapeDtypeStruct((bsz, M_HEADS, 1, M_QK), F32),
                   jax.ShapeDtypeStruct((bsz, M_HEADS, 1, LANES), F32)),
        grid=(bsz, M_HEADS, nc),
        in_specs=[pl.BlockSpec((L, M_QK), lambda b, h, c: (b * nc + c, COL_QM // M_QK + h)),
                  pl.BlockSpec((L, M_QK), lambda b, h, c: (b * nc + c, COL_KM // M_QK + h)),
                  pl.BlockSpec((L, M_V), lambda b, h, c: (b * nc + c, COL_VM // M_V + h)),
                  pl.BlockSpec((L, M_V), lambda b, h, c: (b * nc + c, COL_OM // M_V + h)),
                  row(0), row(M_HEADS), col(0), col(M_HEADS),
                  pl.BlockSpec((1, M_V), lambda b, h, c: (0, h))],
        out_specs=(pl.BlockSpec((L, M_V), lambda b, h, c: (b * nc + c, h)),
                   state((M_QK, M_V)), state((1, M_QK)), state((1, LANES))),
        compiler_params=_params(("parallel", "parallel", "arbitrary"), 32),
        name="mlstm_prompt",
    )(z, z, z, z, g_row, g_row, g_col, g_col, norm_g)


def _mlstm_sample_kernel(qr_ref, kr_ref, vr_ref, or_ref, qc_ref, kc_ref, g_ref, m0_ref, n0_ref, c0_ref, ng_ref,
                         y_ref, c_ref, n_ref, m_ref):
    g = g_ref[0]
    m0_all = m0_ref[0]
    lane = lax.broadcasted_iota(I32, (1, LANES), 1)
    m_out = jnp.zeros((1, LANES), F32)
    for h in range(M_HEADS):
        ig = g[:, h:h + 1]
        lf = _log_sigmoid(g[:, M_HEADS + h:M_HEADS + h + 1])
        m0 = m0_all[:, h:h + 1]
        m = jnp.maximum(lf + m0, ig)
        w = jnp.exp(ig - m)
        decay = jnp.exp(lf + m0 - m)
        q_r = qr_ref[0, h] * M_Q_SCALE
        k_r = kr_ref[0, h]
        v_r = vr_ref[0, h]
        q_c = qc_ref[0, h] * M_Q_SCALE
        c0 = c0_ref[0, 0, h]
        n0 = n0_ref[0, 0, h]
        s = jnp.sum(q_r * k_r, axis=1, keepdims=True) * w
        num = decay * jnp.sum(q_c * c0, axis=0, keepdims=True) + s * v_r
        den = decay * jnp.sum(q_r * n0, axis=1, keepdims=True) + s
        hh = num / jnp.maximum(jnp.abs(den), jnp.exp(-m))
        hn = hh * lax.rsqrt(jnp.mean(hh * hh, axis=1, keepdims=True) + NORM_EPS)
        y = hn * ng_ref[:, h * M_V:(h + 1) * M_V] * jax.nn.sigmoid(or_ref[0, h])
        y_ref[0, :, h * M_V:(h + 1) * M_V] = y.astype(y_ref.dtype)
        c_ref[0, h] = decay * c0 + (w * kc_ref[0, h]) * v_r
        n_ref[0, h] = decay * n0 + w * k_r
        m_out = jnp.where(lane == h, m, m_out)
    m_ref[0] = m_out


def _mlstm_sample(zs, gs, state_c, state_n, state_m, norm_g, layer):
    ns = zs.shape[0]
    qm = zs[:, COL_QM:COL_QM + M_QK_W]
    km = zs[:, COL_KM:COL_KM + M_QK_W]
    rows = lambda a, w: a.reshape(ns, M_HEADS, 1, w)
    m0 = jnp.pad(state_m[layer], ((0, 0), (0, LANES - M_HEADS))).reshape(ns, 1, LANES)
    row_spec = lambda w: pl.BlockSpec((1, M_HEADS, 1, w), lambda s: (s, 0, 0, 0))
    lane_spec = pl.BlockSpec((1, 1, LANES), lambda s: (s, 0, 0))
    col_spec = pl.BlockSpec((1, M_HEADS, M_QK, 1), lambda s: (s, 0, 0, 0))
    return pl.pallas_call(
        _mlstm_sample_kernel,
        out_shape=(jax.ShapeDtypeStruct((ns, 1, M_V_W), BF16),
                   jax.ShapeDtypeStruct((ns, M_HEADS, M_QK, M_V), F32),
                   jax.ShapeDtypeStruct((ns, M_HEADS, 1, M_QK), F32),
                   jax.ShapeDtypeStruct((ns, 1, LANES), F32)),
        grid=(ns,),
        in_specs=[row_spec(M_QK), row_spec(M_QK), row_spec(M_V), row_spec(M_V), col_spec, col_spec,
                  lane_spec, lane_spec,
                  pl.BlockSpec((1, 1, M_HEADS, 1, M_QK), lambda s: (layer, s, 0, 0, 0)),
                  pl.BlockSpec((1, 1, M_HEADS, M_QK, M_V), lambda s: (layer, s, 0, 0, 0)),
                  pl.BlockSpec((1, M_V_W), lambda s: (0, 0))],
        out_specs=(pl.BlockSpec((1, 1, M_V_W), lambda s: (s, 0, 0)),
                   pl.BlockSpec((1, M_HEADS, M_QK, M_V), lambda s: (s, 0, 0, 0)),
                   row_spec(M_QK), lane_spec),
        compiler_params=_params(("parallel",), 32),
        name="mlstm_sample",
    )(rows(qm, M_QK), rows(km, M_QK), rows(zs[:, COL_VM:COL_VM + M_V_W], M_V), rows(zs[:, COL_OM:COL_OM + M_V_W], M_V),
      qm.reshape(ns, M_HEADS, M_QK, 1), km.reshape(ns, M_HEADS, M_QK, 1),
      gs.reshape(ns, 1, LANES), m0,
      state_n.reshape(state_n.shape[0], ns, M_HEADS, 1, M_QK), state_c, norm_g)


def _layernorm_chunks(chunks, d):
    total = sum(jnp.sum(c, axis=1, keepdims=True) for c in chunks)
    mu = total * (1.0 / d)
    var = sum(jnp.sum(jnp.square(c - mu), axis=1, keepdims=True) for c in chunks) * (1.0 / d)
    return mu, lax.rsqrt(var + LN_EPS)


def _merge_kernel(alpha, att_ref, mix_ref, wa_ref, wb_ref, x_ref, g_ref, b_ref, wr_ref, br_ref,
                  h_ref, ids_ref, wts_ref, acc_s, hb_s):
    j = pl.program_id(1)
    nj = acc_s.shape[0]
    tm, tn = acc_s.shape[1], acc_s.shape[2]
    d = nj * tn
    acc_s[j] = alpha * x_ref[...] + _dot(att_ref[...], wa_ref[...]) + _dot(mix_ref[...], wb_ref[...])

    @pl.when(j == nj - 1)
    def _():
        mu, rstd = _layernorm_chunks([acc_s[c] for c in range(nj)], d)
        for c in range(nj):
            y = (acc_s[c] - mu) * rstd * g_ref[:, c * tn:(c + 1) * tn] + b_ref[:, c * tn:(c + 1) * tn]
            hb_s[:, c * tn:(c + 1) * tn] = y.astype(BF16)
            h_ref[:, c * tn:(c + 1) * tn] = y

        lg = _dot(hb_s[...], wr_ref[...]) + br_ref[...]
        lane = lax.broadcasted_iota(I32, lg.shape, 1)
        lane_f = lane.astype(F32)
        far = float(LANES)
        gl = jnp.where(lane < N_GROUPS, lg, -jnp.inf)
        gmax = jnp.max(gl, axis=1, keepdims=True)
        gsel = jnp.min(jnp.where(gl == gmax, lane_f, far), axis=1, keepdims=True).astype(I32)
        gw = 1.0 / jnp.sum(jnp.where(lane < N_GROUPS, jnp.exp(gl - gmax), 0.0), axis=1, keepdims=True)
        e_lane = lane - N_GROUPS
        in_grp = (e_lane >= 0) & (e_lane < N_EXPERTS) & ((e_lane // EXPERTS_PER_GROUP) == gsel)
        el = jnp.where(in_grp, lg, -jnp.inf)
        t1 = jnp.max(el, axis=1, keepdims=True)
        i1 = jnp.min(jnp.where(el == t1, lane_f, far), axis=1, keepdims=True)
        el2 = jnp.where(lane_f == i1, -jnp.inf, el)
        t2 = jnp.max(el2, axis=1, keepdims=True)
        i2 = jnp.min(jnp.where(el2 == t2, lane_f, far), axis=1, keepdims=True)
        e21 = jnp.exp(t2 - t1)
        w1 = gw / (1.0 + e21)
        w2 = w1 * e21
        ids = jnp.where(lane == 0, i1, jnp.where(lane == 1, i2, float(N_GROUPS))) - float(N_GROUPS)
        ids_ref[...] = ids.astype(I32)
        wts_ref[...] = jnp.where(lane == 0, w1, jnp.where(lane == 1, w2, 0.0))


def _merge(att, mix, wob, x, ln_g, ln_b, wr, br, alpha):
    n, d = x.shape
    half = att.shape[1]
    tm = _tile(n, (320, 256, 128, 64))
    tn = _tile(d, (1024, 512))
    nj = d // tn
    rows = lambda w: pl.BlockSpec((1, w), lambda i, j: (0, 0))
    return pl.pallas_call(
        functools.partial(_merge_kernel, alpha),
        out_shape=(jax.ShapeDtypeStruct((n, d), F32),
                   jax.ShapeDtypeStruct((n, LANES), I32),
                   jax.ShapeDtypeStruct((n, LANES), F32)),
        grid=(n // tm, nj),
        in_specs=[pl.BlockSpec((tm, half), lambda i, j: (i, 0)),
                  pl.BlockSpec((tm, half), lambda i, j: (i, 0)),
                  pl.BlockSpec((half, tn), lambda i, j: (0, j)),
                  pl.BlockSpec((half, tn), lambda i, j: (1, j)),
                  pl.BlockSpec((tm, tn), lambda i, j: (i, j)),
                  rows(d), rows(d),
                  pl.BlockSpec((d, LANES), lambda i, j: (0, 0)),
                  rows(LANES)],
        out_specs=(pl.BlockSpec((tm, d), lambda i, j: (i, 0)),
                   pl.BlockSpec((tm, LANES), lambda i, j: (i, 0)),
                   pl.BlockSpec((tm, LANES), lambda i, j: (i, 0))),
        scratch_shapes=[pltpu.VMEM((nj, tm, tn), F32), pltpu.VMEM((tm, d), BF16)],
        compiler_params=_params(("parallel", "arbitrary"), 48),
        name="merge_ln_router",
    )(att, mix, wob, wob, x, ln_g, ln_b, wr, br)


def _gather_kernel(nc_ref, idx_ref, src_ref, out_ref, sem):
    c = pl.program_id(0)
    ch = out_ref.shape[0]

    def copy(r):
        return pltpu.make_async_copy(src_ref.at[pl.ds(idx_ref[c * ch + r], 1), :], out_ref.at[pl.ds(r, 1), :], sem)

    @pl.when(c < nc_ref[0])
    def _():
        def issue(r, carry):
            copy(r).start()
            return carry

        lax.fori_loop(0, ch, issue, 0, unroll=GATHER_UNROLL)

        def drain(r, carry):
            copy(r).wait()
            return carry

        lax.fori_loop(0, ch, drain, 0, unroll=GATHER_UNROLL)


def _row_gather(src, idx, n_chunks):
    p = idx.shape[0]
    w = src.shape[1]
    ch = GATHER_CHUNK
    grid_spec = pltpu.PrefetchScalarGridSpec(
        num_scalar_prefetch=2, grid=(p // ch,),
        in_specs=[pl.BlockSpec(memory_space=pl.ANY)],
        out_specs=pl.BlockSpec((ch, w), lambda c, nc, idx: (jnp.minimum(c, nc[0] - 1), 0)),
        scratch_shapes=[pltpu.SemaphoreType.DMA(())])
    return pl.pallas_call(
        _gather_kernel,
        out_shape=jax.ShapeDtypeStruct((p, w), src.dtype),
        grid_spec=grid_spec,
        compiler_params=_params(("arbitrary",), 16),
        name="row_gather",
    )(n_chunks, idx, src)


def _moe_kernel(te_ref, ti_ref, nu_ref, src_ref, h_ref, wg_ref, wu_ref, wd_ref, o_ref, xf_s, xb_s, sem):
    del te_ref, ti_ref
    t = pl.program_id(0)
    j = pl.program_id(1)
    fc = wg_ref.shape[3]
    tm = xb_s.shape[0]
    nu = nu_ref[0]

    def row_copy(tile, r):
        slot = tile % 2
        return pltpu.make_async_copy(h_ref.at[pl.ds(src_ref[tile * tm + r], 1), :],
                                     xf_s.at[slot, pl.ds(r, 1), :], sem.at[slot])

    def for_rows(fn):
        def body(r, carry):
            fn(r)
            return carry
        lax.fori_loop(0, tm, body, 0, unroll=GATHER_UNROLL)

    @pl.when(t < nu)
    def _():
        @pl.when(j == 0)
        def _():
            @pl.when(t == 0)
            def _():
                for_rows(lambda r: row_copy(t, r).start())

            for_rows(lambda r: row_copy(t, r).wait())
            xb_s[...] = xf_s[t % 2].astype(BF16)

            @pl.when(t + 1 < nu)
            def _():
                for_rows(lambda r: row_copy(t + 1, r).start())

        wgu = jnp.concatenate([wg_ref[0, 0].astype(BF16), wu_ref[0, 0].astype(BF16)], axis=1)
        hgu = _dot(xb_s[...], wgu)
        hg = hgu[:, :fc]
        a = (hg * jax.nn.sigmoid(hg) * hgu[:, fc:]).astype(BF16)
        part = _dot(a, wd_ref[0, 0].astype(BF16))

        @pl.when(j == 0)
        def _():
            o_ref[...] = part

        @pl.when(j != 0)
        def _():
            o_ref[...] = o_ref[...] + part


def _moe_experts(h, src, w_gate, w_up, w_down, tile_expert, tile_index, n_used, layer):
    tm = MOE_TILE
    fc = MOE_FCHUNK
    n_tiles = tile_expert.shape[0]
    d = w_gate.shape[2]
    f = w_gate.shape[3]
    nj = f // fc

    def chunk(t, j, nu):
        return jnp.where(t < nu[0], j, nj - 1)

    grid_spec = pltpu.PrefetchScalarGridSpec(
        num_scalar_prefetch=4, grid=(n_tiles, nj),
        in_specs=[pl.BlockSpec(memory_space=pl.ANY),
                  pl.BlockSpec((1, 1, d, fc), lambda t, j, te, ti, nu, src: (layer, te[t], 0, chunk(t, j, nu))),
                  pl.BlockSpec((1, 1, d, fc), lambda t, j, te, ti, nu, src: (layer, te[t], 0, chunk(t, j, nu))),
                  pl.BlockSpec((1, 1, fc, d), lambda t, j, te, ti, nu, src: (layer, te[t], chunk(t, j, nu), 0))],
        out_specs=pl.BlockSpec((tm, d), lambda t, j, te, ti, nu, src: (ti[t], 0)),
        scratch_shapes=[pltpu.VMEM((2, tm, d), F32), pltpu.VMEM((tm, d), BF16), pltpu.SemaphoreType.DMA((2,))])
    return pl.pallas_call(
        _moe_kernel,
        out_shape=jax.ShapeDtypeStruct((n_tiles * tm, d), F32),
        grid_spec=grid_spec,
        compiler_params=_params(("arbitrary", "arbitrary"), 56),
        name="moe_experts",
    )(tile_expert, tile_index, n_used, src, h, w_gate, w_up, w_down)


def _combine_kernel(alpha, h_ref, f0_ref, f1_ref, w_ref, g_ref, b_ref, o_ref):
    d = o_ref.shape[1]
    w = w_ref[...]
    v = alpha * h_ref[...] + w[:, 0:1] * f0_ref[...] + w[:, 1:2] * f1_ref[...]
    mu = jnp.mean(v, axis=1, keepdims=True)
    vc = v - mu
    var = jnp.sum(vc * vc, axis=1, keepdims=True) * (1.0 / d)
    o_ref[...] = vc * lax.rsqrt(var + LN_EPS) * g_ref[...] + b_ref[...]


def _combine(h, ff, wts, ln_g, ln_b, alpha, row0, nrows):
    n, d = h.shape
    tm = next(c for c in (256, 128, 64, 32, 16, 8) if nrows % c == 0 and row0 % c == 0 and n % c == 0)
    o0 = row0 // tm
    tile = lambda off: pl.BlockSpec((tm, d), lambda i: (i + off, 0))
    rows = pl.BlockSpec((1, d), lambda i: (0, 0))
    return pl.pallas_call(
        functools.partial(_combine_kernel, alpha),
        out_shape=jax.ShapeDtypeStruct((nrows, d), F32),
        grid=(nrows // tm,),
        in_specs=[tile(o0), tile(o0), tile(o0 + n // tm),
                  pl.BlockSpec((tm, LANES), lambda i: (i + o0, 0)), rows, rows],
        out_specs=pl.BlockSpec((tm, d), lambda i: (i, 0)),
        compiler_params=_params(("parallel",), 40),
        name="combine_ln",
    )(h, ff, ff, wts, ln_g, ln_b)


def _routing_tables(ids, n_tiles):
    n = ids.shape[0]
    tm = MOE_TILE
    e = ids[:, :2].reshape(-1)
    onehot = (e[:, None] == jnp.arange(N_EXPERTS, dtype=I32)[None, :]).astype(I32)
    csum = jnp.cumsum(onehot, axis=0)
    rank = jnp.sum((csum - onehot) * onehot, axis=1)
    counts = csum[-1]
    tiles = (counts + tm - 1) // tm
    tile_end = jnp.cumsum(tiles)
    n_used = tile_end[-1]
    pos = ((tile_end - tiles)[e] * tm + rank).astype(I32)
    src = (jnp.arange(n_tiles * tm, dtype=I32) % n).at[pos].set(jnp.arange(2 * n, dtype=I32) // 2)
    t = jnp.minimum(jnp.arange(n_tiles, dtype=I32), n_used - 1)
    tile_expert = jnp.minimum(jnp.sum((tile_end[None, :] <= t[:, None]).astype(I32), axis=1), N_EXPERTS - 1)
    pos2 = pos.reshape(n, 2)
    back = jnp.concatenate([pos2[:, 0], pos2[:, 1]])
    return src, back, tile_expert, t, n_used.reshape(1).astype(I32)


def _layer(l, x, bsz, seq, cache_k, cache_v, state_c, state_n, state_m, page_table,
           w_in, b_gates, norm_g, w_out, ln1_g, ln1_b, w_group, b_group, w_router, b_router,
           w_e_gate, w_e_up, w_e_down, ln2_g, ln2_b, alpha):
    n, d = x.shape
    n_p = bsz * seq
    n_s = n - n_p

    wz = w_in[l, :, :COL_G].astype(BF16)
    wg = jnp.pad(w_in[l, :, COL_G:], ((0, 0), (0, LANES - 2 * M_HEADS)))
    bg = jnp.pad(b_gates[l], (0, LANES - 2 * M_HEADS)).reshape(1, LANES)
    z = _in_proj(x.astype(BF16), wz)
    g_all = _gates(x, wg, bg)

    ng = norm_g[l].reshape(1, M_V_W)
    att_p = _moba_prompt(z, bsz, seq)
    mix_p, c_p, n_pr, m_p = _mlstm_prompt(z, g_all, ng, bsz, seq)

    zs = z[n_p:]
    att_s = _moba_sample(zs, cache_k, cache_v, page_table, l)
    mix_s, c_s, n_sm, m_s = _mlstm_sample(zs, g_all[n_p:], state_c, state_n, state_m, ng, l)

    att = jnp.concatenate([att_p, att_s], axis=0)
    mix = jnp.concatenate([mix_p, mix_s.reshape(n_s, M_V_W)], axis=0)

    wr = jnp.pad(jnp.concatenate([w_group[l], w_router[l]], axis=1),
                 ((0, 0), (0, LANES - N_GROUPS - N_EXPERTS))).astype(BF16)
    br = jnp.pad(jnp.concatenate([b_group[l], b_router[l]]), (0, LANES - N_GROUPS - N_EXPERTS)).reshape(1, LANES)
    h2, ids, wts = _merge(att, mix, w_out[l].astype(BF16), x, ln1_g[l].reshape(1, d), ln1_b[l].reshape(1, d),
                          wr, br, alpha)

    n_tiles = (2 * n) // MOE_TILE + N_EXPERTS + 1
    src, back, tile_expert, tile_index, n_used = _routing_tables(ids, n_tiles)
    ys = _moe_experts(h2, src, w_e_gate, w_e_up, w_e_down, tile_expert, tile_index, n_used, l)
    ff = _row_gather(ys, back, jnp.full((1,), (2 * n) // GATHER_CHUNK, I32))
    g2, b2 = ln2_g[l].reshape(1, d), ln2_b[l].reshape(1, d)
    y_p = _combine(h2, ff, wts, g2, b2, alpha, 0, n_p)
    y_s = _combine(h2, ff, wts, g2, b2, alpha, n_p, n_s)

    kv = lambda rows, col: rows[:, col:col + ATT_KV_W]
    return (y_p, y_s,
            kv(z[:n_p], COL_K).reshape(bsz, seq, ATT_KV_HEADS, HEAD_DIM),
            kv(z[:n_p], COL_V).reshape(bsz, seq, ATT_KV_HEADS, HEAD_DIM),
            kv(zs, COL_K).reshape(n_s, 1, ATT_KV_HEADS, HEAD_DIM),
            kv(zs, COL_V).reshape(n_s, 1, ATT_KV_HEADS, HEAD_DIM),
            c_p, n_pr.reshape(bsz, M_HEADS, M_QK), m_p[:, :, 0, 0],
            c_s, n_sm.reshape(n_s, M_HEADS, M_QK), m_s[:, 0, :M_HEADS])


def kernel(x_prompt, x_sample, cache_k, cache_v, state_mlstm_c, state_mlstm_n, state_mlstm_m, page_table,
           w_in, b_gates, mlstm_norm_g, w_out, ln1_g, ln1_b, w_group, b_group, w_router, b_router,
           w_e_gate, w_e_up, w_e_down, ln2_g, ln2_b):
    bsz, seq, d = x_prompt.shape
    n_s, dec_seq, _ = x_sample.shape
    depth = w_in.shape[0]
    assert dec_seq == 1 and seq % MOBA_BLOCK == 0 and w_in.shape[2] == COL_G + 2 * M_HEADS
    assert (page_table.shape[1] * cache_k.shape[2]) % MOBA_BLOCK == 0 and MOBA_BLOCK % cache_k.shape[2] == 0
    alpha = (2 * depth) ** 0.25
    x = jnp.concatenate([x_prompt.reshape(bsz * seq, d), x_sample.reshape(n_s, d)], axis=0)
    per_layer = []
    for l in range(depth):
        outs = _layer(l, x, bsz, seq, cache_k, cache_v, state_mlstm_c, state_mlstm_n, state_mlstm_m, page_table,
                      w_in, b_gates, mlstm_norm_g, w_out, ln1_g, ln1_b, w_group, b_group, w_router, b_router,
                      w_e_gate, w_e_up, w_e_down, ln2_g, ln2_b, alpha)
        x = jnp.concatenate([outs[0], outs[1]], axis=0)
        per_layer.append(outs[2:])
    stacked = [jnp.stack([p[i] for p in per_layer]) for i in range(10)]
    return (outs[0].reshape(bsz, seq, d), outs[1].reshape(n_s, 1, d), *stacked)
```

```python
import functools

import jax
import jax.numpy as jnp
from jax import lax
from jax.experimental import pallas as pl
from jax.experimental.pallas import tpu as pltpu

F32, BF16, I32 = jnp.float32, jnp.bfloat16, jnp.int32

SUBLANES = 8
LANES = 128
MIB = 1024 * 1024

HEAD_DIM = 128
ATT_HEADS = 16
ATT_KV_HEADS = 8
ATT_GROUP = ATT_HEADS // ATT_KV_HEADS
ATT_SCALE = HEAD_DIM ** -0.5
LOG2_E = 1.4426950408889634
MOBA_BLOCK = 256
MOBA_TOPK = 3
M_HEADS = 4
M_QK = 256
M_V = 512
M_Q_SCALE = M_QK ** -0.5
N_GROUPS = 4
EXPERTS_PER_GROUP = 8
N_EXPERTS = N_GROUPS * EXPERTS_PER_GROUP
LN_EPS = 1e-5
NORM_EPS = 1e-6

ATT_Q_W = ATT_HEADS * HEAD_DIM
ATT_KV_W = ATT_KV_HEADS * HEAD_DIM
M_QK_W = M_HEADS * M_QK
M_V_W = M_HEADS * M_V
COL_Q = 0
COL_K = COL_Q + ATT_Q_W
COL_V = COL_K + ATT_KV_W
COL_QM = COL_V + ATT_KV_W
COL_KM = COL_QM + M_QK_W
COL_VM = COL_KM + M_QK_W
COL_OM = COL_VM + M_V_W
COL_G = COL_OM + M_V_W

MOE_TILE = 512
MOE_CHUNK = 1024
GATHER_UNROLL = 8


def _params(sem, vmem_mib):
    return pltpu.CompilerParams(dimension_semantics=sem, vmem_limit_bytes=vmem_mib * MIB)


def _dot(a, b):
    return jnp.dot(a, b, preferred_element_type=F32)


def _dot_nt(a, b):
    return lax.dot_general(a, b, (((1,), (1,)), ((), ())), preferred_element_type=F32)


def _log_sigmoid(x):
    return jnp.minimum(x, 0.0) - jnp.log1p(jnp.exp(-jnp.abs(x)))


def _tile(n, candidates):
    for c in candidates:
        if n % c == 0:
            return c
    raise ValueError(f"no tile for {n} in {candidates}")


def _matmul_kernel(x_ref, w_ref, o_ref):
    o_ref[...] = _dot(x_ref[...], w_ref[...])


def _in_proj(xb, wb, w):
    n, d = xb.shape
    tm = _tile(n, (832, 640, 512, 256, 128, 64))
    tn = _tile(w, (1024, 512, 256, 128))
    return pl.pallas_call(
        _matmul_kernel,
        out_shape=jax.ShapeDtypeStruct((n, w), F32),
        grid=(w // tn, n // tm),
        in_specs=[pl.BlockSpec((tm, d), lambda j, i: (i, 0)),
                  pl.BlockSpec((d, tn), lambda j, i: (0, j))],
        out_specs=pl.BlockSpec((tm, tn), lambda j, i: (i, j)),
        compiler_params=_params(("parallel", "parallel"), 48),
        name="in_proj",
    )(xb, wb)


def _gates_kernel(x_ref, w_ref, b_ref, o_ref):
    x = x_ref[...]
    w = w_ref[...]
    xh = x.astype(BF16)
    xl = (x - xh.astype(F32)).astype(BF16)
    wh = w.astype(BF16)
    wl = (w - wh.astype(F32)).astype(BF16)
    o_ref[...] = _dot(xh, wh) + _dot(xl, wh) + _dot(xh, wl) + b_ref[...]


def _gates(x, wg, bg):
    n, d = x.shape
    tm = _tile(n, (416, 320, 256, 128, 64))
    return pl.pallas_call(
        _gates_kernel,
        out_shape=jax.ShapeDtypeStruct((n, LANES), F32),
        grid=(n // tm,),
        in_specs=[pl.BlockSpec((tm, d), lambda i: (i, 0)),
                  pl.BlockSpec((d, LANES), lambda i: (0, 0)),
                  pl.BlockSpec((1, LANES), lambda i: (0, 0))],
        out_specs=pl.BlockSpec((tm, LANES), lambda i: (i, 0)),
        compiler_params=_params(("parallel",), 40),
        name="gates",
    )(x, wg, bg)


def _moba_prompt_kernel(q_ref, k_ref, v_ref, o_ref, kb_s, vt_s, km_s):
    i = pl.program_id(2)
    nb = kb_s.shape[0]
    blk = MOBA_BLOCK
    nq = ATT_GROUP * blk

    @pl.when(i == 0)
    def _():
        for j in range(nb):
            kj = k_ref[j * blk:(j + 1) * blk, :]
            kb_s[j] = kj.astype(BF16)
            km_s[j:j + 1, :] = jnp.mean(kj, axis=0, keepdims=True)
            vt_s[j] = v_ref[j * blk:(j + 1) * blk, :].T.astype(BF16)
        km_s[nb:, :] = jnp.zeros((km_s.shape[0] - nb, HEAD_DIM), F32)

    q = q_ref[...]
    q2 = jnp.concatenate([q[:, g * HEAD_DIM:(g + 1) * HEAD_DIM] for g in range(ATT_GROUP)], axis=0).astype(BF16)

    gate = _dot_nt(km_s[...].astype(BF16), q2)
    jj = lax.broadcasted_iota(I32, gate.shape, 0)
    g = jnp.where(jj < i, gate, -jnp.inf)
    sel = jnp.zeros(gate.shape, F32)
    for _ in range(MOBA_TOPK):
        mx = jnp.max(g, axis=0, keepdims=True)
        first = jnp.min(jnp.where(g == mx, jj, 2 * nb), axis=0, keepdims=True)
        pick = (jj == first) & (mx > -jnp.inf)
        sel = jnp.where(pick, 1.0, sel)
        g = jnp.where(pick, -jnp.inf, g)
    ri = lax.broadcasted_iota(I32, (blk, nq), 0)
    ci = lax.broadcasted_iota(I32, (blk, nq), 1)
    causal = ri <= (ci & (blk - 1))
    c2 = ATT_SCALE * LOG2_E

    def attend_all(own):
        m = l = acc = None
        for j in (own,) + tuple(range(own)):
            s = _dot_nt(kb_s[j], q2)
            s = jnp.where(causal if j == own else sel[j:j + 1, :] > 0.5, s, -jnp.inf)
            m_blk = jnp.max(s, axis=0, keepdims=True)
            m_new = m_blk if m is None else jnp.maximum(m, m_blk)
            p = jnp.exp2((s - m_new) * c2)
            pv = _dot(vt_s[j], p.astype(BF16))
            if m is None:
                l, acc = jnp.sum(p, axis=0, keepdims=True), pv
            else:
                alpha = jnp.exp2((m - m_new) * c2)
                l = alpha * l + jnp.sum(p, axis=0, keepdims=True)
                acc = alpha * acc + pv
            m = m_new
        out = (acc / l).T
        for g_i in range(ATT_GROUP):
            o_ref[:, g_i * HEAD_DIM:(g_i + 1) * HEAD_DIM] = out[g_i * blk:(g_i + 1) * blk, :].astype(o_ref.dtype)

    for own in range(nb):
        pl.when(i == own)(functools.partial(attend_all, own))


def _moba_prompt(z, bsz, seq):
    nb = seq // MOBA_BLOCK
    qw = ATT_GROUP * HEAD_DIM
    return pl.pallas_call(
        _moba_prompt_kernel,
        out_shape=jax.ShapeDtypeStruct((bsz * seq, ATT_Q_W), BF16),
        grid=(bsz, ATT_KV_HEADS, nb),
        in_specs=[pl.BlockSpec((MOBA_BLOCK, qw), lambda b, h, i: (b * nb + i, COL_Q // qw + h)),
                  pl.BlockSpec((seq, HEAD_DIM), lambda b, h, i: (b, COL_K // HEAD_DIM + h)),
                  pl.BlockSpec((seq, HEAD_DIM), lambda b, h, i: (b, COL_V // HEAD_DIM + h))],
        out_specs=pl.BlockSpec((MOBA_BLOCK, qw), lambda b, h, i: (b * nb + i, h)),
        scratch_shapes=[pltpu.VMEM((nb, MOBA_BLOCK, HEAD_DIM), BF16),
                        pltpu.VMEM((nb, HEAD_DIM, MOBA_BLOCK), BF16),
                        pltpu.VMEM((2 * SUBLANES, HEAD_DIM), F32)],
        compiler_params=_params(("parallel", "parallel", "arbitrary"), 32),
        name="moba_prompt",
    )(z, z, z)


def _moba_sample_kernel(n_pages, pt_ref, q_ref, kn_ref, vn_ref, *rest):
    del pt_ref
    kp = rest[:n_pages]
    vp = rest[n_pages:2 * n_pages]
    o_ref = rest[2 * n_pages]
    page = kp[0].shape[2]
    per_blk = MOBA_BLOCK // page
    nb = n_pages // per_blk
    nrow = page * ATT_KV_HEADS
    both = lambda a: jnp.concatenate([a] * ATT_GROUP, axis=0)

    q = q_ref[0]
    qb = q.astype(BF16)
    ri = lax.broadcasted_iota(I32, (ATT_HEADS, nrow), 0)
    ci = lax.broadcasted_iota(I32, (ATT_HEADS, nrow), 1)
    own = (ci % ATT_KV_HEADS) == (ri % ATT_KV_HEADS)

    scores, ksum = [], []
    for p in range(n_pages):
        k3 = kp[p][0, 0]
        scores.append(_dot_nt(qb, k3.reshape(nrow, HEAD_DIM).astype(BF16)) * ATT_SCALE)
        ksum.append(jnp.sum(k3, axis=0))
    gates = []
    for j in range(nb):
        kmean = sum(ksum[j * per_blk:(j + 1) * per_blk]) * (1.0 / MOBA_BLOCK)
        gates.append(jnp.sum(q * both(kmean), axis=1, keepdims=True))
    sel = []
    for j in range(nb):
        rank = jnp.zeros(gates[j].shape, F32)
        for jp in range(nb):
            if jp < j:
                rank = rank + (gates[jp] >= gates[j]).astype(F32)
            elif jp > j:
                rank = rank + (gates[jp] > gates[j]).astype(F32)
        sel.append(rank < MOBA_TOPK)

    s_new = jnp.sum(q * both(kn_ref[0]), axis=1, keepdims=True) * ATT_SCALE
    m = s_new
    for p in range(n_pages):
        scores[p] = jnp.where(sel[p // per_blk], jnp.where(own, scores[p], -jnp.inf), -jnp.inf)
        m = jnp.maximum(m, jnp.max(scores[p], axis=1, keepdims=True))
    p_new = jnp.exp(s_new - m)
    l = p_new
    acc = p_new * both(vn_ref[0])
    for p in range(n_pages):
        pp = jnp.exp(scores[p] - m)
        l = l + jnp.sum(pp, axis=1, keepdims=True)
        acc = acc + _dot(pp.astype(BF16), vp[p][0, 0].reshape(nrow, HEAD_DIM).astype(BF16))
    o_ref[0] = (acc / l).astype(o_ref.dtype)


def _moba_sample(zs, cache_k, cache_v, page_table, layer):
    ns = zs.shape[0]
    n_pages = page_table.shape[1]
    page = cache_k.shape[2]
    pt_flat = page_table.reshape(-1).astype(I32)
    qg = jnp.transpose(zs[:, COL_Q:COL_Q + ATT_Q_W].reshape(ns, ATT_KV_HEADS, ATT_GROUP, HEAD_DIM), (0, 2, 1, 3))
    kv_new = lambda col: zs[:, col:col + ATT_KV_W].reshape(ns, ATT_KV_HEADS, HEAD_DIM)

    def page_spec(p):
        return pl.BlockSpec((1, 1, page, ATT_KV_HEADS, HEAD_DIM),
                            lambda s, pt: (layer, pt[s * n_pages + p], 0, 0, 0))

    rows = lambda r: pl.BlockSpec((1, r, HEAD_DIM), lambda s, pt: (s, 0, 0))
    grid_spec = pltpu.PrefetchScalarGridSpec(
        num_scalar_prefetch=1, grid=(ns,),
        in_specs=[rows(ATT_HEADS), rows(ATT_KV_HEADS), rows(ATT_KV_HEADS)] + [page_spec(p) for p in range(n_pages)] * 2,
        out_specs=rows(ATT_HEADS))
    out = pl.pallas_call(
        functools.partial(_moba_sample_kernel, n_pages),
        out_shape=jax.ShapeDtypeStruct((ns, ATT_HEADS, HEAD_DIM), BF16),
        grid_spec=grid_spec,
        compiler_params=_params(("parallel",), 48),
        name="moba_sample",
    )(pt_flat, qg.reshape(ns, ATT_HEADS, HEAD_DIM), kv_new(COL_K), kv_new(COL_V),
      *([cache_k] * n_pages), *([cache_v] * n_pages))
    return jnp.transpose(out.reshape(ns, ATT_GROUP, ATT_KV_HEADS, HEAD_DIM), (0, 2, 1, 3)).reshape(ns, ATT_Q_W)


def _mlstm_prompt_kernel(q_ref, k_ref, v_ref, og_ref, igr_ref, fgr_ref, igc_ref, fgc_ref, ng_ref,
                         y_ref, c_ref, n_ref, m_ref):
    @pl.when(pl.program_id(2) == 0)
    def _():
        c_ref[...] = jnp.zeros(c_ref.shape, F32)
        n_ref[...] = jnp.zeros(n_ref.shape, F32)
        m_ref[...] = jnp.zeros(m_ref.shape, F32)

    L = q_ref.shape[0]
    q = q_ref[...] * M_Q_SCALE
    k = k_ref[...]
    ig_r = igr_ref[0, 0]
    lf_r = _log_sigmoid(fgr_ref[0, 0])
    ig_c = igc_ref[0, 0]
    lf_c = _log_sigmoid(fgc_ref[0, 0])
    ri = lax.broadcasted_iota(I32, (L, L), 0)
    ci = lax.broadcasted_iota(I32, (L, L), 1)
    tril = ci <= ri
    b_c = jnp.sum(jnp.where(tril, lf_r, 0.0), axis=1, keepdims=True)
    b_r = jnp.sum(jnp.where(ri <= ci, lf_c, 0.0), axis=0, keepdims=True)
    r_r = ig_r - b_r
    m0 = m_ref[0, 0][:, 0:1]
    m_c = b_c + jnp.maximum(m0, jnp.max(jnp.where(tril, r_r, -jnp.inf), axis=1, keepdims=True))
    dmat = jnp.exp(jnp.where(tril, b_c - m_c + r_r, -jnp.inf))
    qb = q.astype(BF16)
    vb = v_ref[...].astype(BF16)
    s = _dot_nt(qb, k.astype(BF16)) * dmat
    inter = jnp.exp(b_c + m0 - m_c)
    c0 = c_ref[0, 0]
    n0 = n_ref[0, 0]
    num = inter * _dot(qb, c0.astype(BF16)) + _dot(s.astype(BF16), vb)
    den = inter * jnp.sum(q * n0, axis=1, keepdims=True) + jnp.sum(s, axis=1, keepdims=True)
    h = num / jnp.maximum(jnp.abs(den), jnp.exp(-m_c))
    hn = h * lax.rsqrt(jnp.mean(h * h, axis=1, keepdims=True) + NORM_EPS)
    y_ref[...] = (hn * ng_ref[...] * jax.nn.sigmoid(og_ref[...])).astype(y_ref.dtype)

    b_last = b_c[L - 1:L, :]
    m_last = m_c[L - 1:L, :]
    kw = k * jnp.exp(b_last - b_c + ig_c - m_last)
    decay = jnp.exp(b_last + m0 - m_last)
    c_ref[0, 0] = decay * c0 + _dot(kw.T.astype(BF16), vb)
    n_ref[0, 0] = decay * n0 + jnp.sum(kw, axis=0, keepdims=True)
    m_ref[0, 0] = jnp.broadcast_to(m_last, (1, LANES))


def _mlstm_prompt(z, g_all, norm_g, bsz, seq):
    L = _tile(seq, (256, 128))
    nc = seq // L
    gp = jnp.transpose(g_all[:bsz * seq, :2 * M_HEADS].reshape(bsz, seq, 2 * M_HEADS), (0, 2, 1))
    g_row = gp.reshape(bsz, 2 * M_HEADS, 1, seq)
    g_col = gp.reshape(bsz, 2 * M_HEADS, seq, 1)
    row = lambda off: pl.BlockSpec((1, 1, 1, L), lambda b, h, c: (b, off + h, 0, c))
    col = lambda off: pl.BlockSpec((1, 1, L, 1), lambda b, h, c: (b, off + h, c, 0))
    state = lambda shape: pl.BlockSpec((1, 1) + shape, lambda b, h, c: (b, h, 0, 0))
    return pl.pallas_call(
        _mlstm_prompt_kernel,
        out_shape=(jax.ShapeDtypeStruct((bsz * seq, M_V_W), BF16),
                   jax.ShapeDtypeStruct((bsz, M_HEADS, M_QK, M_V), F32),
                   jax.ShapeDtypeStruct((bsz, M_HEADS, 1, M_QK), F32),
                   jax.ShapeDtypeStruct((bsz, M_HEADS, 1, LANES), F32)),
        grid=(bsz, M_HEADS, nc),
        in_specs=[pl.BlockSpec((L, M_QK), lambda b, h, c: (b * nc + c, COL_QM // M_QK + h)),
                  pl.BlockSpec((L, M_QK), lambda b, h, c: (b * nc + c, COL_KM // M_QK + h)),
                  pl.BlockSpec((L, M_V), lambda b, h, c: (b * nc + c, COL_VM // M_V + h)),
                  pl.BlockSpec((L, M_V), lambda b, h, c: (b * nc + c, COL_OM // M_V + h)),
                  row(0), row(M_HEADS), col(0), col(M_HEADS),
                  pl.BlockSpec((1, M_V), lambda b, h, c: (0, h))],
        out_specs=(pl.BlockSpec((L, M_V), lambda b, h, c: (b * nc + c, h)),
                   state((M_QK, M_V)), state((1, M_QK)), state((1, LANES))),
        compiler_params=_params(("parallel", "parallel", "arbitrary"), 32),
        name="mlstm_prompt",
    )(z, z, z, z, g_row, g_row, g_col, g_col, norm_g)


def _mlstm_sample_kernel(qr_ref, kr_ref, vr_ref, or_ref, qc_ref, kc_ref, g_ref, m0_ref, n0_ref, c0_ref, ng_ref,
                         y_ref, c_ref, n_ref, m_ref):
    g = g_ref[0]
    m0_all = m0_ref[0]
    lane = lax.broadcasted_iota(I32, (1, LANES), 1)
    m_out = jnp.zeros((1, LANES), F32)
    for h in range(M_HEADS):
        ig = g[:, h:h + 1]
        lf = _log_sigmoid(g[:, M_HEADS + h:M_HEADS + h + 1])
        m0 = m0_all[:, h:h + 1]
        m = jnp.maximum(lf + m0, ig)
        w = jnp.exp(ig - m)
        decay = jnp.exp(lf + m0 - m)
        q_r = qr_ref[0, h] * M_Q_SCALE
        k_r = kr_ref[0, h]
        v_r = vr_ref[0, h]
        q_c = qc_ref[0, h] * M_Q_SCALE
        c0 = c0_ref[0, 0, h]
        n0 = n0_ref[0, 0, h]
        s = jnp.sum(q_r * k_r, axis=1, keepdims=True) * w
        num = decay * jnp.sum(q_c * c0, axis=0, keepdims=True) + s * v_r
        den = decay * jnp.sum(q_r * n0, axis=1, keepdims=True) + s
        hh = num / jnp.maximum(jnp.abs(den), jnp.exp(-m))
        hn = hh * lax.rsqrt(jnp.mean(hh * hh, axis=1, keepdims=True) + NORM_EPS)
        y = hn * ng_ref[:, h * M_V:(h + 1) * M_V] * jax.nn.sigmoid(or_ref[0, h])
        y_ref[0, :, h * M_V:(h + 1) * M_V] = y.astype(y_ref.dtype)
        c_ref[0, h] = decay * c0 + (w * kc_ref[0, h]) * v_r
        n_ref[0, h] = decay * n0 + w * k_r
        m_out = jnp.where(lane == h, m, m_out)
    m_ref[0] = m_out


def _mlstm_sample(zs, gs, state_c, state_n, state_m, norm_g, layer):
    ns = zs.shape[0]
    qm = zs[:, COL_QM:COL_QM + M_QK_W]
    km = zs[:, COL_KM:COL_KM + M_QK_W]
    rows = lambda a, w: a.reshape(ns, M_HEADS, 1, w)
    m0 = jnp.pad(state_m[layer], ((0, 0), (0, LANES - M_HEADS))).reshape(ns, 1, LANES)
    row_spec = lambda w: pl.BlockSpec((1, M_HEADS, 1, w), lambda s: (s, 0, 0, 0))
    lane_spec = pl.BlockSpec((1, 1, LANES), lambda s: (s, 0, 0))
    col_spec = pl.BlockSpec((1, M_HEADS, M_QK, 1), lambda s: (s, 0, 0, 0))
    return pl.pallas_call(
        _mlstm_sample_kernel,
        out_shape=(jax.ShapeDtypeStruct((ns, 1, M_V_W), BF16),
                   jax.ShapeDtypeStruct((ns, M_HEADS, M_QK, M_V), F32),
                   jax.ShapeDtypeStruct((ns, M_HEADS, 1, M_QK), F32),
                   jax.ShapeDtypeStruct((ns, 1, LANES), F32)),
        grid=(ns,),
        in_specs=[row_spec(M_QK), row_spec(M_QK), row_spec(M_V), row_spec(M_V), col_spec, col_spec,
                  lane_spec, lane_spec,
                  pl.BlockSpec((1, 1, M_HEADS, 1, M_QK), lambda s: (layer, s, 0, 0, 0)),
                  pl.BlockSpec((1, 1, M_HEADS, M_QK, M_V), lambda s: (layer, s, 0, 0, 0)),
                  pl.BlockSpec((1, M_V_W), lambda s: (0, 0))],
        out_specs=(pl.BlockSpec((1, 1, M_V_W), lambda s: (s, 0, 0)),
                   pl.BlockSpec((1, M_HEADS, M_QK, M_V), lambda s: (s, 0, 0, 0)),
                   row_spec(M_QK), lane_spec),
        compiler_params=_params(("parallel",), 32),
        name="mlstm_sample",
    )(rows(qm, M_QK), rows(km, M_QK), rows(zs[:, COL_VM:COL_VM + M_V_W], M_V), rows(zs[:, COL_OM:COL_OM + M_V_W], M_V),
      qm.reshape(ns, M_HEADS, M_QK, 1), km.reshape(ns, M_HEADS, M_QK, 1),
      gs.reshape(ns, 1, LANES), m0,
      state_n.reshape(state_n.shape[0], ns, M_HEADS, 1, M_QK), state_c, norm_g)


def _layernorm_chunks(chunks, d):
    total = sum(jnp.sum(c, axis=1, keepdims=True) for c in chunks)
    mu = total * (1.0 / d)
    var = sum(jnp.sum(jnp.square(c - mu), axis=1, keepdims=True) for c in chunks) * (1.0 / d)
    return mu, lax.rsqrt(var + LN_EPS)


def _merge_kernel(alpha, att_ref, mix_ref, wa_ref, wb_ref, x_ref, g_ref, b_ref, wr_ref, br_ref,
                  h_ref, ids_ref, wts_ref, acc_s, hb_s):
    j = pl.program_id(1)
    nj = acc_s.shape[0]
    tm, tn = acc_s.shape[1], acc_s.shape[2]
    d = nj * tn
    acc_s[j] = alpha * x_ref[...] + _dot(att_ref[...], wa_ref[...]) + _dot(mix_ref[...], wb_ref[...])

    @pl.when(j == nj - 1)
    def _():
        mu, rstd = _layernorm_chunks([acc_s[c] for c in range(nj)], d)
        for c in range(nj):
            y = (acc_s[c] - mu) * rstd * g_ref[:, c * tn:(c + 1) * tn] + b_ref[:, c * tn:(c + 1) * tn]
            hb_s[:, c * tn:(c + 1) * tn] = y.astype(BF16)
            h_ref[:, c * tn:(c + 1) * tn] = y

        lg = _dot(hb_s[...], wr_ref[...]) + br_ref[...]
        lane = lax.broadcasted_iota(I32, lg.shape, 1)
        lane_f = lane.astype(F32)
        far = float(LANES)
        gl = jnp.where(lane < N_GROUPS, lg, -jnp.inf)
        gmax = jnp.max(gl, axis=1, keepdims=True)
        gsel = jnp.min(jnp.where(gl == gmax, lane_f, far), axis=1, keepdims=True).astype(I32)
        gw = 1.0 / jnp.sum(jnp.where(lane < N_GROUPS, jnp.exp(gl - gmax), 0.0), axis=1, keepdims=True)
        e_lane = lane - N_GROUPS
        in_grp = (e_lane >= 0) & (e_lane < N_EXPERTS) & ((e_lane // EXPERTS_PER_GROUP) == gsel)
        el = jnp.where(in_grp, lg, -jnp.inf)
        t1 = jnp.max(el, axis=1, keepdims=True)
        i1 = jnp.min(jnp.where(el == t1, lane_f, far), axis=1, keepdims=True)
        el2 = jnp.where(lane_f == i1, -jnp.inf, el)
        t2 = jnp.max(el2, axis=1, keepdims=True)
        i2 = jnp.min(jnp.where(el2 == t2, lane_f, far), axis=1, keepdims=True)
        e21 = jnp.exp(t2 - t1)
        w1 = gw / (1.0 + e21)
        w2 = w1 * e21
        ids = jnp.where(lane == 0, i1, jnp.where(lane == 1, i2, float(N_GROUPS))) - float(N_GROUPS)
        ids_ref[...] = ids.astype(I32)
        wts_ref[...] = jnp.where(lane == 0, w1, jnp.where(lane == 1, w2, 0.0))


def _merge(att, mix, wob, x, ln_g, ln_b, wr, br, alpha):
    n, d = x.shape
    half = att.shape[1]
    tm = _tile(n, (320, 256, 128, 64))
    tn = _tile(d, (1024, 512))
    nj = d // tn
    rows = lambda w: pl.BlockSpec((1, w), lambda i, j: (0, 0))
    return pl.pallas_call(
        functools.partial(_merge_kernel, alpha),
        out_shape=(jax.ShapeDtypeStruct((n, d), F32),
                   jax.ShapeDtypeStruct((n, LANES), I32),
                   jax.ShapeDtypeStruct((n, LANES), F32)),
        grid=(n // tm, nj),
        in_specs=[pl.BlockSpec((tm, half), lambda i, j: (i, 0)),
                  pl.BlockSpec((tm, half), lambda i, j: (i, 0)),
                  pl.BlockSpec((half, tn), lambda i, j: (0, j)),
                  pl.BlockSpec((half, tn), lambda i, j: (1, j)),
                  pl.BlockSpec((tm, tn), lambda i, j: (i, j)),
                  rows(d), rows(d),
                  pl.BlockSpec((d, LANES), lambda i, j: (0, 0)),
                  rows(LANES)],
        out_specs=(pl.BlockSpec((tm, d), lambda i, j: (i, 0)),
                   pl.BlockSpec((tm, LANES), lambda i, j: (i, 0)),
                   pl.BlockSpec((tm, LANES), lambda i, j: (i, 0))),
        scratch_shapes=[pltpu.VMEM((nj, tm, tn), F32), pltpu.VMEM((tm, d), BF16)],
        compiler_params=_params(("parallel", "arbitrary"), 48),
        name="merge_ln_router",
    )(att, mix, wob, wob, x, ln_g, ln_b, wr, br)


def _moe_kernel(te_ref, ti_ref, nu_ref, src_ref, h_ref, wg_ref, wu_ref, wd_ref, o_ref,
                xf_s, xb_s, hg_s, hu_s, a_s, sem):
    del te_ref, ti_ref
    t = pl.program_id(0)
    ph = pl.program_id(1)
    nk, tm, kc = xb_s.shape
    nu = nu_ref[0]

    def row_copy(tile, r):
        slot = tile % 2
        return pltpu.make_async_copy(h_ref.at[pl.ds(src_ref[tile * tm + r], 1), :],
                                     xf_s.at[slot, pl.ds(r, 1), :], sem.at[slot])

    def for_rows(fn):
        def body(r, carry):
            fn(r)
            return carry
        lax.fori_loop(0, tm, body, 0, unroll=GATHER_UNROLL)

    @pl.when(t < nu)
    def _():
        @pl.when(ph == 0)
        def _():
            @pl.when(t == 0)
            def _():
                for_rows(lambda r: row_copy(t, r).start())

            for_rows(lambda r: row_copy(t, r).wait())
            for c in range(nk):
                xb_s[c] = xf_s[t % 2, :, c * kc:(c + 1) * kc].astype(BF16)

            @pl.when(t + 1 < nu)
            def _():
                for_rows(lambda r: row_copy(t + 1, r).start())

        @pl.when(ph < nk)
        def _():
            x = xb_s[ph]
            pg = _dot(x, wg_ref[0, 0].astype(BF16))
            pu = _dot(x, wu_ref[0, 0].astype(BF16))

            @pl.when(ph == 0)
            def _():
                hg_s[...] = pg
                hu_s[...] = pu

            @pl.when(ph != 0)
            def _():
                hg_s[...] += pg
                hu_s[...] += pu

        @pl.when(ph == nk - 1)
        def _():
            hg = hg_s[...]
            a_s[...] = (hg * jax.nn.sigmoid(hg) * hu_s[...]).astype(BF16)

        @pl.when(ph >= nk)
        def _():
            o_ref[...] = _dot(a_s[...], wd_ref[0, 0].astype(BF16))


def _moe_experts(h, src, w_gate, w_up, w_down, tile_expert, tile_index, n_used, layer):
    tm = MOE_TILE
    n_tiles = tile_expert.shape[0]
    d = w_gate.shape[2]
    f = w_gate.shape[3]
    kc = nc = MOE_CHUNK
    nk = d // kc
    nn = d // nc

    def used(t, nu, a, b):
        return jnp.where(t < nu[0], a, b)

    def k_idx(t, ph, nu):
        return used(t, nu, jnp.minimum(ph, nk - 1), nk - 1)

    def n_idx(t, ph, nu):
        return used(t, nu, jnp.maximum(ph - nk, 0), nn - 1)

    grid_spec = pltpu.PrefetchScalarGridSpec(
        num_scalar_prefetch=4, grid=(n_tiles, nk + nn),
        in_specs=[pl.BlockSpec(memory_space=pl.ANY),
                  pl.BlockSpec((1, 1, kc, f), lambda t, ph, te, ti, nu, src: (layer, te[t], k_idx(t, ph, nu), 0)),
                  pl.BlockSpec((1, 1, kc, f), lambda t, ph, te, ti, nu, src: (layer, te[t], k_idx(t, ph, nu), 0)),
                  pl.BlockSpec((1, 1, f, nc), lambda t, ph, te, ti, nu, src: (layer, te[t], 0, n_idx(t, ph, nu)))],
        out_specs=pl.BlockSpec((tm, nc), lambda t, ph, te, ti, nu, src: (ti[t], n_idx(t, ph, nu))),
        scratch_shapes=[pltpu.VMEM((2, tm, d), F32), pltpu.VMEM((nk, tm, kc), BF16),
                        pltpu.VMEM((tm, f), F32), pltpu.VMEM((tm, f), F32), pltpu.VMEM((tm, f), BF16),
                        pltpu.SemaphoreType.DMA((2,))])
    return pl.pallas_call(
        _moe_kernel,
        out_shape=jax.ShapeDtypeStruct((n_tiles * tm, d), F32),
        grid_spec=grid_spec,
        compiler_params=_params(("arbitrary", "arbitrary"), 48),
        name="moe_experts",
    )(tile_expert, tile_index, n_used, src, h, w_gate, w_up, w_down)


def _combine_kernel(alpha, row0, n_tok, pos_ref, h_ref, w_ref, g_ref, b_ref, ys_ref, o_ref, f_s, sem):
    i = pl.program_id(0)
    tm, d = o_ref.shape

    def row_copy(tile, s, r):
        slot = tile % 2
        return pltpu.make_async_copy(ys_ref.at[pl.ds(pos_ref[s * n_tok + row0 + tile * tm + r], 1), :],
                                     f_s.at[slot, s, pl.ds(r, 1), :], sem.at[slot])

    def for_rows(fn):
        def body(r, carry):
            fn(0, r)
            fn(1, r)
            return carry
        lax.fori_loop(0, tm, body, 0, unroll=GATHER_UNROLL)

    @pl.when(i == 0)
    def _():
        for_rows(lambda s, r: row_copy(i, s, r).start())

    @pl.when(i + 1 < pl.num_programs(0))
    def _():
        for_rows(lambda s, r: row_copy(i + 1, s, r).start())

    for_rows(lambda s, r: row_copy(i, s, r).wait())
    w = w_ref[...]
    v = alpha * h_ref[...] + w[:, 0:1] * f_s[i % 2, 0] + w[:, 1:2] * f_s[i % 2, 1]
    mu = jnp.mean(v, axis=1, keepdims=True)
    vc = v - mu
    var = jnp.sum(vc * vc, axis=1, keepdims=True) * (1.0 / d)
    o_ref[...] = vc * lax.rsqrt(var + LN_EPS) * g_ref[...] + b_ref[...]


def _combine(h, ys, pos, wts, ln_g, ln_b, alpha, row0, nrows):
    n, d = h.shape
    tm = next(c for c in (128, 64, 32, 16, 8) if nrows % c == 0 and row0 % c == 0)
    o0 = row0 // tm
    rows = pl.BlockSpec((1, d), lambda i, pos: (0, 0))
    grid_spec = pltpu.PrefetchScalarGridSpec(
        num_scalar_prefetch=1, grid=(nrows // tm,),
        in_specs=[pl.BlockSpec((tm, d), lambda i, pos: (i + o0, 0)),
                  pl.BlockSpec((tm, LANES), lambda i, pos: (i + o0, 0)), rows, rows,
                  pl.BlockSpec(memory_space=pl.ANY)],
        out_specs=pl.BlockSpec((tm, d), lambda i, pos: (i, 0)),
        scratch_shapes=[pltpu.VMEM((2, 2, tm, d), F32), pltpu.SemaphoreType.DMA((2,))])
    return pl.pallas_call(
        functools.partial(_combine_kernel, alpha, row0, n),
        out_shape=jax.ShapeDtypeStruct((nrows, d), F32),
        grid_spec=grid_spec,
        compiler_params=_params(("arbitrary",), 32),
        name="combine_ln",
    )(pos, h, wts, ln_g, ln_b, ys)


def _routing_tables(ids, n_tiles):
    n = ids.shape[0]
    tm = MOE_TILE
    e = ids[:, :2].reshape(-1)
    onehot = (e[:, None] == jnp.arange(N_EXPERTS, dtype=I32)[None, :]).astype(I32)
    csum = jnp.cumsum(onehot, axis=0)
    rank = jnp.sum((csum - onehot) * onehot, axis=1)
    counts = csum[-1]
    tiles = (counts + tm - 1) // tm
    tile_end = jnp.cumsum(tiles)
    n_used = tile_end[-1]
    pos = ((tile_end - tiles)[e] * tm + rank).astype(I32)
    src = (jnp.arange(n_tiles * tm, dtype=I32) % n).at[pos].set(jnp.arange(2 * n, dtype=I32) // 2)
    t = jnp.minimum(jnp.arange(n_tiles, dtype=I32), n_used - 1)
    tile_expert = jnp.minimum(jnp.sum((tile_end[None, :] <= t[:, None]).astype(I32), axis=1), N_EXPERTS - 1)
    pos2 = pos.reshape(n, 2)
    back = jnp.concatenate([pos2[:, 0], pos2[:, 1]])
    return src, back, tile_expert, t, n_used.reshape(1).astype(I32)


def _layer(l, x, bsz, seq, cache_k, cache_v, state_c, state_n, state_m, page_table,
           w_in, b_gates, norm_g, w_out, ln1_g, ln1_b, w_group, b_group, w_router, b_router,
           w_e_gate, w_e_up, w_e_down, ln2_g, ln2_b, alpha):
    n, d = x.shape
    n_p = bsz * seq
    n_s = n - n_p

    wz = w_in[l].astype(BF16)
    wg = jnp.pad(w_in[l, :, COL_G:], ((0, 0), (0, LANES - 2 * M_HEADS)))
    bg = jnp.pad(b_gates[l], (0, LANES - 2 * M_HEADS)).reshape(1, LANES)
    z = _in_proj(x.astype(BF16), wz, COL_G)
    g_all = _gates(x, wg, bg)

    ng = norm_g[l].reshape(1, M_V_W)
    att_p = _moba_prompt(z, bsz, seq)
    mix_p, c_p, n_pr, m_p = _mlstm_prompt(z, g_all, ng, bsz, seq)

    zs = z[n_p:]
    att_s = _moba_sample(zs, cache_k, cache_v, page_table, l)
    mix_s, c_s, n_sm, m_s = _mlstm_sample(zs, g_all[n_p:], state_c, state_n, state_m, ng, l)

    att = jnp.concatenate([att_p, att_s], axis=0)
    mix = jnp.concatenate([mix_p, mix_s.reshape(n_s, M_V_W)], axis=0)

    wr = jnp.pad(jnp.concatenate([w_group[l], w_router[l]], axis=1),
                 ((0, 0), (0, LANES - N_GROUPS - N_EXPERTS))).astype(BF16)
    br = jnp.pad(jnp.concatenate([b_group[l], b_router[l]]), (0, LANES - N_GROUPS - N_EXPERTS)).reshape(1, LANES)
    h2, ids, wts = _merge(att, mix, w_out[l].astype(BF16), x, ln1_g[l].reshape(1, d), ln1_b[l].reshape(1, d),
                          wr, br, alpha)

    n_tiles = (2 * n) // MOE_TILE + N_EXPERTS + 1
    src, back, tile_expert, tile_index, n_used = _routing_tables(ids, n_tiles)
    ys = _moe_experts(h2, src, w_e_gate, w_e_up, w_e_down, tile_expert, tile_index, n_used, l)
    g2, b2 = ln2_g[l].reshape(1, d), ln2_b[l].reshape(1, d)
    y_p = _combine(h2, ys, back, wts, g2, b2, alpha, 0, n_p)
    y_s = _combine(h2, ys, back, wts, g2, b2, alpha, n_p, n_s)

    kv = lambda rows, col: rows[:, col:col + ATT_KV_W]
    return (y_p, y_s,
            kv(z[:n_p], COL_K).reshape(bsz, seq, ATT_KV_HEADS, HEAD_DIM),
            kv(z[:n_p], COL_V).reshape(bsz, seq, ATT_KV_HEADS, HEAD_DIM),
            kv(zs, COL_K).reshape(n_s, 1, ATT_KV_HEADS, HEAD_DIM),
            kv(zs, COL_V).reshape(n_s, 1, ATT_KV_HEADS, HEAD_DIM),
            c_p, n_pr.reshape(bsz, M_HEADS, M_QK), m_p[:, :, 0, 0],
            c_s, n_sm.reshape(n_s, M_HEADS, M_QK), m_s[:, 0, :M_HEADS])


def kernel(x_prompt, x_sample, cache_k, cache_v, state_mlstm_c, state_mlstm_n, state_mlstm_m, page_table,
           w_in, b_gates, mlstm_norm_g, w_out, ln1_g, ln1_b, w_group, b_group, w_router, b_router,
           w_e_gate, w_e_up, w_e_down, ln2_g, ln2_b):
    bsz, seq, d = x_prompt.shape
    n_s, dec_seq, _ = x_sample.shape
    depth = w_in.shape[0]
    assert dec_seq == 1 and seq % MOBA_BLOCK == 0 and w_in.shape[2] == COL_G + 2 * M_HEADS
    assert (page_table.shape[1] * cache_k.shape[2]) % MOBA_BLOCK == 0 and MOBA_BLOCK % cache_k.shape[2] == 0
    alpha = (2 * depth) ** 0.25
    x = jnp.concatenate([x_prompt.reshape(bsz * seq, d), x_sample.reshape(n_s, d)], axis=0)
    per_layer = []
    for l in range(depth):
        outs = _layer(l, x, bsz, seq, cache_k, cache_v, state_mlstm_c, state_mlstm_n, state_mlstm_m, page_table,
                      w_in, b_gates, mlstm_norm_g, w_out, ln1_g, ln1_b, w_group, b_group, w_router, b_router,
                      w_e_gate, w_e_up, w_e_down, ln2_g, ln2_b, alpha)
        x = jnp.concatenate([outs[0], outs[1]], axis=0)
        per_layer.append(outs[2:])
    stacked = [jnp.stack([p[i] for p in per_layer]) for i in range(10)]
    return (outs[0].reshape(bsz, seq, d), outs[1].reshape(n_s, 1, d), *stacked)
```

```python
import functools

import jax
import jax.numpy as jnp
from jax import lax
from jax.experimental import pallas as pl
from jax.experimental.pallas import tpu as pltpu

F32, BF16, I32 = jnp.float32, jnp.bfloat16, jnp.int32

SUBLANES = 8
LANES = 128
MIB = 1024 * 1024

HEAD_DIM = 128
ATT_HEADS = 16
ATT_KV_HEADS = 8
ATT_GROUP = ATT_HEADS // ATT_KV_HEADS
ATT_SCALE = HEAD_DIM ** -0.5
LOG2_E = 1.4426950408889634
MOBA_BLOCK = 256
MOBA_TOPK = 3
M_HEADS = 4
M_QK = 256
M_V = 512
M_Q_SCALE = M_QK ** -0.5
N_GROUPS = 4
EXPERTS_PER_GROUP = 8
N_EXPERTS = N_GROUPS * EXPERTS_PER_GROUP
LN_EPS = 1e-5
NORM_EPS = 1e-6

ATT_Q_W = ATT_HEADS * HEAD_DIM
ATT_KV_W = ATT_KV_HEADS * HEAD_DIM
M_QK_W = M_HEADS * M_QK
M_V_W = M_HEADS * M_V
COL_Q = 0
COL_K = COL_Q + ATT_Q_W
COL_V = COL_K + ATT_KV_W
COL_QM = COL_V + ATT_KV_W
COL_KM = COL_QM + M_QK_W
COL_VM = COL_KM + M_QK_W
COL_OM = COL_VM + M_V_W
COL_G = COL_OM + M_V_W

MOE_TILE = 512
MOE_CHUNK = 1024
GATHER_UNROLL = 8


def _params(sem, vmem_mib):
    return pltpu.CompilerParams(dimension_semantics=sem, vmem_limit_bytes=vmem_mib * MIB)


def _dot(a, b):
    return jnp.dot(a, b, preferred_element_type=F32)


def _dot_nt(a, b):
    return lax.dot_general(a, b, (((1,), (1,)), ((), ())), preferred_element_type=F32)


def _log_sigmoid(x):
    return jnp.minimum(x, 0.0) - jnp.log1p(jnp.exp(-jnp.abs(x)))


def _tile(n, candidates):
    for c in candidates:
        if n % c == 0:
            return c
    raise ValueError(f"no tile for {n} in {candidates}")


def _matmul_nt_kernel(x_ref, w_ref, o_ref):
    o_ref[...] = _dot_nt(x_ref[...], w_ref[...])


def _in_proj(xb, wt, w):
    n, d = xb.shape
    tm = _tile(n, (1024, 832, 640, 512, 256, 128, 64))
    tn = _tile(w, (1024, 512, 256, 128))
    return pl.pallas_call(
        _matmul_nt_kernel,
        out_shape=jax.ShapeDtypeStruct((n, w), F32),
        grid=(w // tn, n // tm),
        in_specs=[pl.BlockSpec((tm, d), lambda j, i: (i, 0)),
                  pl.BlockSpec((tn, d), lambda j, i: (j, 0))],
        out_specs=pl.BlockSpec((tm, tn), lambda j, i: (i, j)),
        compiler_params=_params(("parallel", "parallel"), 48),
        name="in_proj",
    )(xb, wt)


def _gates_kernel(x_ref, w_ref, b_ref, o_ref):
    x = x_ref[...]
    w = w_ref[...]
    xh = x.astype(BF16)
    xl = (x - xh.astype(F32)).astype(BF16)
    wh = w.astype(BF16)
    wl = (w - wh.astype(F32)).astype(BF16)
    o_ref[...] = _dot(xh, wh) + _dot(xl, wh) + _dot(xh, wl) + b_ref[...]


def _gates(x, wg, bg):
    n, d = x.shape
    tm = _tile(n, (512, 416, 320, 256, 128, 64))
    return pl.pallas_call(
        _gates_kernel,
        out_shape=jax.ShapeDtypeStruct((n, LANES), F32),
        grid=(n // tm,),
        in_specs=[pl.BlockSpec((tm, d), lambda i: (i, 0)),
                  pl.BlockSpec((d, LANES), lambda i: (0, 0)),
                  pl.BlockSpec((1, LANES), lambda i: (0, 0))],
        out_specs=pl.BlockSpec((tm, LANES), lambda i: (i, 0)),
        compiler_params=_params(("parallel",), 40),
        name="gates",
    )(x, wg, bg)


def _moba_prompt_kernel(q_ref, k_ref, v_ref, o_ref, kb_s, vt_s, km_s):
    i = pl.program_id(2)
    nb = kb_s.shape[0]
    blk = MOBA_BLOCK
    nq = ATT_GROUP * blk

    @pl.when(i == 0)
    def _():
        for j in range(nb):
            kj = k_ref[j * blk:(j + 1) * blk, :]
            kb_s[j] = kj.astype(BF16)
            km_s[j:j + 1, :] = jnp.mean(kj, axis=0, keepdims=True)
            vt_s[j] = v_ref[j * blk:(j + 1) * blk, :].T.astype(BF16)
        km_s[nb:, :] = jnp.zeros((km_s.shape[0] - nb, HEAD_DIM), F32)

    q = q_ref[...]
    q2 = jnp.concatenate([q[:, g * HEAD_DIM:(g + 1) * HEAD_DIM] for g in range(ATT_GROUP)], axis=0).astype(BF16)

    gate = _dot_nt(km_s[...].astype(BF16), q2)
    jj = lax.broadcasted_iota(I32, gate.shape, 0)
    g = jnp.where(jj < i, gate, -jnp.inf)
    sel = jnp.zeros(gate.shape, F32)
    for _ in range(MOBA_TOPK):
        mx = jnp.max(g, axis=0, keepdims=True)
        first = jnp.min(jnp.where(g == mx, jj, 2 * nb), axis=0, keepdims=True)
        pick = (jj == first) & (mx > -jnp.inf)
        sel = jnp.where(pick, 1.0, sel)
        g = jnp.where(pick, -jnp.inf, g)
    ri = lax.broadcasted_iota(I32, (blk, nq), 0)
    ci = lax.broadcasted_iota(I32, (blk, nq), 1)
    causal = ri <= (ci & (blk - 1))
    c2 = ATT_SCALE * LOG2_E

    def attend_all(own):
        m = l = acc = None
        for j in (own,) + tuple(range(own)):
            s = _dot_nt(kb_s[j], q2)
            s = jnp.where(causal if j == own else sel[j:j + 1, :] > 0.5, s, -jnp.inf)
            m_blk = jnp.max(s, axis=0, keepdims=True)
            m_new = m_blk if m is None else jnp.maximum(m, m_blk)
            p = jnp.exp2((s - m_new) * c2)
            pv = _dot(vt_s[j], p.astype(BF16))
            if m is None:
                l, acc = jnp.sum(p, axis=0, keepdims=True), pv
            else:
                alpha = jnp.exp2((m - m_new) * c2)
                l = alpha * l + jnp.sum(p, axis=0, keepdims=True)
                acc = alpha * acc + pv
            m = m_new
        out = (acc / l).T
        for g_i in range(ATT_GROUP):
            o_ref[:, g_i * HEAD_DIM:(g_i + 1) * HEAD_DIM] = out[g_i * blk:(g_i + 1) * blk, :].astype(o_ref.dtype)

    for own in range(nb):
        pl.when(i == own)(functools.partial(attend_all, own))


def _moba_prompt(z, bsz, seq):
    nb = seq // MOBA_BLOCK
    qw = ATT_GROUP * HEAD_DIM
    return pl.pallas_call(
        _moba_prompt_kernel,
        out_shape=jax.ShapeDtypeStruct((bsz * seq, ATT_Q_W), BF16),
        grid=(bsz, ATT_KV_HEADS, nb),
        in_specs=[pl.BlockSpec((MOBA_BLOCK, qw), lambda b, h, i: (b * nb + i, COL_Q // qw + h)),
                  pl.BlockSpec((seq, HEAD_DIM), lambda b, h, i: (b, COL_K // HEAD_DIM + h)),
                  pl.BlockSpec((seq, HEAD_DIM), lambda b, h, i: (b, COL_V // HEAD_DIM + h))],
        out_specs=pl.BlockSpec((MOBA_BLOCK, qw), lambda b, h, i: (b * nb + i, h)),
        scratch_shapes=[pltpu.VMEM((nb, MOBA_BLOCK, HEAD_DIM), BF16),
                        pltpu.VMEM((nb, HEAD_DIM, MOBA_BLOCK), BF16),
                        pltpu.VMEM((2 * SUBLANES, HEAD_DIM), F32)],
        compiler_params=_params(("parallel", "parallel", "arbitrary"), 32),
        name="moba_prompt",
    )(z, z, z)


def _moba_sample_kernel(n_pages, pt_ref, q_ref, kn_ref, vn_ref, *rest):
    del pt_ref
    kp = rest[:n_pages]
    vp = rest[n_pages:2 * n_pages]
    o_ref = rest[2 * n_pages]
    page = kp[0].shape[2]
    per_blk = MOBA_BLOCK // page
    nb = n_pages // per_blk
    nrow = page * ATT_KV_HEADS
    both = lambda a: jnp.concatenate([a] * ATT_GROUP, axis=0)

    q = q_ref[0]
    qb = q.astype(BF16)
    ri = lax.broadcasted_iota(I32, (ATT_HEADS, nrow), 0)
    ci = lax.broadcasted_iota(I32, (ATT_HEADS, nrow), 1)
    own = (ci % ATT_KV_HEADS) == (ri % ATT_KV_HEADS)

    scores, ksum = [], []
    for p in range(n_pages):
        k3 = kp[p][0, 0]
        scores.append(_dot_nt(qb, k3.reshape(nrow, HEAD_DIM).astype(BF16)) * ATT_SCALE)
        ksum.append(jnp.sum(k3, axis=0))
    gates = []
    for j in range(nb):
        kmean = sum(ksum[j * per_blk:(j + 1) * per_blk]) * (1.0 / MOBA_BLOCK)
        gates.append(jnp.sum(q * both(kmean), axis=1, keepdims=True))
    sel = []
    for j in range(nb):
        rank = jnp.zeros(gates[j].shape, F32)
        for jp in range(nb):
            if jp < j:
                rank = rank + (gates[jp] >= gates[j]).astype(F32)
            elif jp > j:
                rank = rank + (gates[jp] > gates[j]).astype(F32)
        sel.append(rank < MOBA_TOPK)

    s_new = jnp.sum(q * both(kn_ref[0]), axis=1, keepdims=True) * ATT_SCALE
    m = s_new
    for p in range(n_pages):
        scores[p] = jnp.where(sel[p // per_blk], jnp.where(own, scores[p], -jnp.inf), -jnp.inf)
        m = jnp.maximum(m, jnp.max(scores[p], axis=1, keepdims=True))
    p_new = jnp.exp(s_new - m)
    l = p_new
    acc = p_new * both(vn_ref[0])
    for p in range(n_pages):
        pp = jnp.exp(scores[p] - m)
        l = l + jnp.sum(pp, axis=1, keepdims=True)
        acc = acc + _dot(pp.astype(BF16), vp[p][0, 0].reshape(nrow, HEAD_DIM).astype(BF16))
    o_ref[0] = (acc / l).astype(o_ref.dtype)


def _moba_sample(zs, cache_k, cache_v, page_table, layer):
    ns = zs.shape[0]
    n_pages = page_table.shape[1]
    page = cache_k.shape[2]
    pt_flat = page_table.reshape(-1).astype(I32)
    qg = jnp.transpose(zs[:, COL_Q:COL_Q + ATT_Q_W].reshape(ns, ATT_KV_HEADS, ATT_GROUP, HEAD_DIM), (0, 2, 1, 3))
    kv_new = lambda col: zs[:, col:col + ATT_KV_W].reshape(ns, ATT_KV_HEADS, HEAD_DIM)

    def page_spec(p):
        return pl.BlockSpec((1, 1, page, ATT_KV_HEADS, HEAD_DIM),
                            lambda s, pt: (layer, pt[s * n_pages + p], 0, 0, 0))

    rows = lambda r: pl.BlockSpec((1, r, HEAD_DIM), lambda s, pt: (s, 0, 0))
    grid_spec = pltpu.PrefetchScalarGridSpec(
        num_scalar_prefetch=1, grid=(ns,),
        in_specs=[rows(ATT_HEADS), rows(ATT_KV_HEADS), rows(ATT_KV_HEADS)] + [page_spec(p) for p in range(n_pages)] * 2,
        out_specs=rows(ATT_HEADS))
    out = pl.pallas_call(
        functools.partial(_moba_sample_kernel, n_pages),
        out_shape=jax.ShapeDtypeStruct((ns, ATT_HEADS, HEAD_DIM), BF16),
        grid_spec=grid_spec,
        compiler_params=_params(("parallel",), 48),
        name="moba_sample",
    )(pt_flat, qg.reshape(ns, ATT_HEADS, HEAD_DIM), kv_new(COL_K), kv_new(COL_V),
      *([cache_k] * n_pages), *([cache_v] * n_pages))
    return jnp.transpose(out.reshape(ns, ATT_GROUP, ATT_KV_HEADS, HEAD_DIM), (0, 2, 1, 3)).reshape(ns, ATT_Q_W)


def _mlstm_prompt_kernel(q_ref, k_ref, v_ref, og_ref, igr_ref, fgr_ref, igc_ref, fgc_ref, ng_ref,
                         y_ref, c_ref, n_ref, m_ref):
    @pl.when(pl.program_id(2) == 0)
    def _():
        c_ref[...] = jnp.zeros(c_ref.shape, F32)
        n_ref[...] = jnp.zeros(n_ref.shape, F32)
        m_ref[...] = jnp.zeros(m_ref.shape, F32)

    L = q_ref.shape[0]
    q = q_ref[...] * M_Q_SCALE
    k = k_ref[...]
    ig_r = igr_ref[0, 0]
    lf_r = _log_sigmoid(fgr_ref[0, 0])
    ig_c = igc_ref[0, 0]
    lf_c = _log_sigmoid(fgc_ref[0, 0])
    ri = lax.broadcasted_iota(I32, (L, L), 0)
    ci = lax.broadcasted_iota(I32, (L, L), 1)
    tril = ci <= ri
    b_c = jnp.sum(jnp.where(tril, lf_r, 0.0), axis=1, keepdims=True)
    b_r = jnp.sum(jnp.where(ri <= ci, lf_c, 0.0), axis=0, keepdims=True)
    r_r = ig_r - b_r
    m0 = m_ref[0, 0][:, 0:1]
    m_c = b_c + jnp.maximum(m0, jnp.max(jnp.where(tril, r_r, -jnp.inf), axis=1, keepdims=True))
    dmat = jnp.exp(jnp.where(tril, b_c - m_c + r_r, -jnp.inf))
    qb = q.astype(BF16)
    vb = v_ref[...].astype(BF16)
    s = _dot_nt(qb, k.astype(BF16)) * dmat
    inter = jnp.exp(b_c + m0 - m_c)
    c0 = c_ref[0, 0]
    n0 = n_ref[0, 0]
    num = inter * _dot(qb, c0.astype(BF16)) + _dot(s.astype(BF16), vb)
    den = inter * jnp.sum(q * n0, axis=1, keepdims=True) + jnp.sum(s, axis=1, keepdims=True)
    h = num / jnp.maximum(jnp.abs(den), jnp.exp(-m_c))
    hn = h * lax.rsqrt(jnp.mean(h * h, axis=1, keepdims=True) + NORM_EPS)
    y_ref[...] = (hn * ng_ref[...] * jax.nn.sigmoid(og_ref[...])).astype(y_ref.dtype)

    b_last = b_c[L - 1:L, :]
    m_last = m_c[L - 1:L, :]
    kw = k * jnp.exp(b_last - b_c + ig_c - m_last)
    decay = jnp.exp(b_last + m0 - m_last)
    c_ref[0, 0] = decay * c0 + _dot(kw.T.astype(BF16), vb)
    n_ref[0, 0] = decay * n0 + jnp.sum(kw, axis=0, keepdims=True)
    m_ref[0, 0] = jnp.broadcast_to(m_last, (1, LANES))


def _mlstm_prompt(z, g_all, norm_g, bsz, seq):
    L = _tile(seq, (256, 128))
    nc = seq // L
    gp = jnp.transpose(g_all[:, :2 * M_HEADS].reshape(bsz, seq, 2 * M_HEADS), (0, 2, 1))
    g_row = gp.reshape(bsz, 2 * M_HEADS, 1, seq)
    g_col = gp.reshape(bsz, 2 * M_HEADS, seq, 1)
    row = lambda off: pl.BlockSpec((1, 1, 1, L), lambda b, h, c: (b, off + h, 0, c))
    col = lambda off: pl.BlockSpec((1, 1, L, 1), lambda b, h, c: (b, off + h, c, 0))
    state = lambda shape: pl.BlockSpec((1, 1) + shape, lambda b, h, c: (b, h, 0, 0))
    return pl.pallas_call(
        _mlstm_prompt_kernel,
        out_shape=(jax.ShapeDtypeStruct((bsz * seq, M_V_W), BF16),
                   jax.ShapeDtypeStruct((bsz, M_HEADS, M_QK, M_V), F32),
                   jax.ShapeDtypeStruct((bsz, M_HEADS, 1, M_QK), F32),
                   jax.ShapeDtypeStruct((bsz, M_HEADS, 1, LANES), F32)),
        grid=(bsz, M_HEADS, nc),
        in_specs=[pl.BlockSpec((L, M_QK), lambda b, h, c: (b * nc + c, COL_QM // M_QK + h)),
                  pl.BlockSpec((L, M_QK), lambda b, h, c: (b * nc + c, COL_KM // M_QK + h)),
                  pl.BlockSpec((L, M_V), lambda b, h, c: (b * nc + c, COL_VM // M_V + h)),
                  pl.BlockSpec((L, M_V), lambda b, h, c: (b * nc + c, COL_OM // M_V + h)),
                  row(0), row(M_HEADS), col(0), col(M_HEADS),
                  pl.BlockSpec((1, M_V), lambda b, h, c: (0, h))],
        out_specs=(pl.BlockSpec((L, M_V), lambda b, h, c: (b * nc + c, h)),
                   state((M_QK, M_V)), state((1, M_QK)), state((1, LANES))),
        compiler_params=_params(("parallel", "parallel", "arbitrary"), 32),
        name="mlstm_prompt",
    )(z, z, z, z, g_row, g_row, g_col, g_col, norm_g)


def _mlstm_sample_kernel(qr_ref, kr_ref, vr_ref, or_ref, qc_ref, kc_ref, g_ref, m0_ref, n0_ref, c0_ref, ng_ref,
                         y_ref, c_ref, n_ref, m_ref):
    g = g_ref[0]
    m0_all = m0_ref[0]
    lane = lax.broadcasted_iota(I32, (1, LANES), 1)
    m_out = jnp.zeros((1, LANES), F32)
    for h in range(M_HEADS):
        ig = g[:, h:h + 1]
        lf = _log_sigmoid(g[:, M_HEADS + h:M_HEADS + h + 1])
        m0 = m0_all[:, h:h + 1]
        m = jnp.maximum(lf + m0, ig)
        w = jnp.exp(ig - m)
        decay = jnp.exp(lf + m0 - m)
        q_r = qr_ref[0, h] * M_Q_SCALE
        k_r = kr_ref[0, h]
        v_r = vr_ref[0, h]
        q_c = qc_ref[0, h] * M_Q_SCALE
        c0 = c0_ref[0, 0, h]
        n0 = n0_ref[0, 0, h]
        s = jnp.sum(q_r * k_r, axis=1, keepdims=True) * w
        num = decay * jnp.sum(q_c * c0, axis=0, keepdims=True) + s * v_r
        den = decay * jnp.sum(q_r * n0, axis=1, keepdims=True) + s
        hh = num / jnp.maximum(jnp.abs(den), jnp.exp(-m))
        hn = hh * lax.rsqrt(jnp.mean(hh * hh, axis=1, keepdims=True) + NORM_EPS)
        y = hn * ng_ref[:, h * M_V:(h + 1) * M_V] * jax.nn.sigmoid(or_ref[0, h])
        y_ref[0, :, h * M_V:(h + 1) * M_V] = y.astype(y_ref.dtype)
        c_ref[0, h] = decay * c0 + (w * kc_ref[0, h]) * v_r
        n_ref[0, h] = decay * n0 + w * k_r
        m_out = jnp.where(lane == h, m, m_out)
    m_ref[0] = m_out


def _mlstm_sample(zs, gs, state_c, state_n, state_m, norm_g, layer):
    ns = zs.shape[0]
    qm = zs[:, COL_QM:COL_QM + M_QK_W]
    km = zs[:, COL_KM:COL_KM + M_QK_W]
    rows = lambda a, w: a.reshape(ns, M_HEADS, 1, w)
    m0 = jnp.pad(state_m[layer], ((0, 0), (0, LANES - M_HEADS))).reshape(ns, 1, LANES)
    row_spec = lambda w: pl.BlockSpec((1, M_HEADS, 1, w), lambda s: (s, 0, 0, 0))
    lane_spec = pl.BlockSpec((1, 1, LANES), lambda s: (s, 0, 0))
    col_spec = pl.BlockSpec((1, M_HEADS, M_QK, 1), lambda s: (s, 0, 0, 0))
    return pl.pallas_call(
        _mlstm_sample_kernel,
        out_shape=(jax.ShapeDtypeStruct((ns, 1, M_V_W), BF16),
                   jax.ShapeDtypeStruct((ns, M_HEADS, M_QK, M_V), F32),
                   jax.ShapeDtypeStruct((ns, M_HEADS, 1, M_QK), F32),
                   jax.ShapeDtypeStruct((ns, 1, LANES), F32)),
        grid=(ns,),
        in_specs=[row_spec(M_QK), row_spec(M_QK), row_spec(M_V), row_spec(M_V), col_spec, col_spec,
                  lane_spec, lane_spec,
                  pl.BlockSpec((1, 1, M_HEADS, 1, M_QK), lambda s: (layer, s, 0, 0, 0)),
                  pl.BlockSpec((1, 1, M_HEADS, M_QK, M_V), lambda s: (layer, s, 0, 0, 0)),
                  pl.BlockSpec((1, M_V_W), lambda s: (0, 0))],
        out_specs=(pl.BlockSpec((1, 1, M_V_W), lambda s: (s, 0, 0)),
                   pl.BlockSpec((1, M_HEADS, M_QK, M_V), lambda s: (s, 0, 0, 0)),
                   row_spec(M_QK), lane_spec),
        compiler_params=_params(("parallel",), 32),
        name="mlstm_sample",
    )(rows(qm, M_QK), rows(km, M_QK), rows(zs[:, COL_VM:COL_VM + M_V_W], M_V), rows(zs[:, COL_OM:COL_OM + M_V_W], M_V),
      qm.reshape(ns, M_HEADS, M_QK, 1), km.reshape(ns, M_HEADS, M_QK, 1),
      gs.reshape(ns, 1, LANES), m0,
      state_n.reshape(state_n.shape[0], ns, M_HEADS, 1, M_QK), state_c, norm_g)


def _layernorm_chunks(chunks, d):
    total = sum(jnp.sum(c, axis=1, keepdims=True) for c in chunks)
    mu = total * (1.0 / d)
    var = sum(jnp.sum(jnp.square(c - mu), axis=1, keepdims=True) for c in chunks) * (1.0 / d)
    return mu, lax.rsqrt(var + LN_EPS)


def _merge_kernel(alpha, *refs):
    att_ref, mix_ref, wa_ref, wb_ref, x_ref, g_ref, b_ref, wr_ref, br_ref = refs[:9]
    h_ref, ids_ref, wts_ref = refs[-3:]
    j = pl.program_id(1)
    tn = x_ref.shape[1]
    d = h_ref.shape[1]
    nj = d // tn
    cols = lambda c: slice(c * tn, (c + 1) * tn)
    pre = alpha * x_ref[...] + _dot(att_ref[...], wa_ref[...]) + _dot(mix_ref[...], wb_ref[...])
    for c in range(nj):
        @pl.when(j == c)
        def _(c=c):
            h_ref[:, cols(c)] = pre

    @pl.when(j == nj - 1)
    def _():
        mu, rstd = _layernorm_chunks([h_ref[:, cols(c)] for c in range(nj)], d)
        lg = br_ref[...]
        for c in range(nj):
            y = (h_ref[:, cols(c)] - mu) * rstd * g_ref[:, cols(c)] + b_ref[:, cols(c)]
            h_ref[:, cols(c)] = y
            lg = lg + _dot(y.astype(BF16), wr_ref[cols(c), :])
        lane = lax.broadcasted_iota(I32, lg.shape, 1)
        lane_f = lane.astype(F32)
        far = float(LANES)
        gl = jnp.where(lane < N_GROUPS, lg, -jnp.inf)
        gmax = jnp.max(gl, axis=1, keepdims=True)
        gsel = jnp.min(jnp.where(gl == gmax, lane_f, far), axis=1, keepdims=True).astype(I32)
        gw = 1.0 / jnp.sum(jnp.where(lane < N_GROUPS, jnp.exp(gl - gmax), 0.0), axis=1, keepdims=True)
        e_lane = lane - N_GROUPS
        in_grp = (e_lane >= 0) & (e_lane < N_EXPERTS) & ((e_lane // EXPERTS_PER_GROUP) == gsel)
        el = jnp.where(in_grp, lg, -jnp.inf)
        t1 = jnp.max(el, axis=1, keepdims=True)
        i1 = jnp.min(jnp.where(el == t1, lane_f, far), axis=1, keepdims=True)
        el2 = jnp.where(lane_f == i1, -jnp.inf, el)
        t2 = jnp.max(el2, axis=1, keepdims=True)
        i2 = jnp.min(jnp.where(el2 == t2, lane_f, far), axis=1, keepdims=True)
        e21 = jnp.exp(t2 - t1)
        w1 = gw / (1.0 + e21)
        w2 = w1 * e21
        ids = jnp.where(lane == 0, i1, jnp.where(lane == 1, i2, float(N_GROUPS))) - float(N_GROUPS)
        ids_ref[...] = ids.astype(I32)
        wts_ref[...] = jnp.where(lane == 0, w1, jnp.where(lane == 1, w2, 0.0))


def _merge(att, mix, wob, x, ln_g, ln_b, wr, br, alpha, n_total, row0, prev=None):
    r, d = x.shape
    half = att.shape[1]
    tm = _tile(r, (512, 256, 128, 64))
    assert row0 % tm == 0
    o0 = row0 // tm
    tn = _tile(d, (1024, 512))
    rows = lambda w: pl.BlockSpec((1, w), lambda i, j: (0, 0))
    prev = () if prev is None else tuple(prev)
    n_in = 9
    return pl.pallas_call(
        functools.partial(_merge_kernel, alpha),
        out_shape=(jax.ShapeDtypeStruct((n_total, d), F32),
                   jax.ShapeDtypeStruct((n_total, LANES), I32),
                   jax.ShapeDtypeStruct((n_total, LANES), F32)),
        grid=(r // tm, d // tn),
        in_specs=[pl.BlockSpec((tm, half), lambda i, j: (i, 0)),
                  pl.BlockSpec((tm, half), lambda i, j: (i, 0)),
                  pl.BlockSpec((half, tn), lambda i, j: (0, j)),
                  pl.BlockSpec((half, tn), lambda i, j: (1, j)),
                  pl.BlockSpec((tm, tn), lambda i, j: (i, j)),
                  rows(d), rows(d),
                  pl.BlockSpec((d, LANES), lambda i, j: (0, 0)),
                  rows(LANES)] + [pl.BlockSpec(memory_space=pl.ANY)] * len(prev),
        out_specs=(pl.BlockSpec((tm, d), lambda i, j: (i + o0, 0)),
                   pl.BlockSpec((tm, LANES), lambda i, j: (i + o0, 0)),
                   pl.BlockSpec((tm, LANES), lambda i, j: (i + o0, 0))),
        input_output_aliases={n_in + k: k for k in range(len(prev))},
        compiler_params=_params(("parallel", "arbitrary"), 52),
        name="merge_ln_router",
    )(att, mix, wob, wob, x, ln_g, ln_b, wr, br, *prev)


def _moe_kernel(te_ref, ti_ref, nu_ref, src_ref, h_ref, wg_ref, wu_ref, wd_ref, o_ref,
                xf_s, xb_s, hg_s, hu_s, a_s, sem):
    del te_ref, ti_ref
    t = pl.program_id(0)
    ph = pl.program_id(1)
    nk, tm, kc = xb_s.shape
    nu = nu_ref[0]

    def row_copy(tile, r):
        slot = tile % 2
        return pltpu.make_async_copy(h_ref.at[pl.ds(src_ref[tile * tm + r], 1), :],
                                     xf_s.at[slot, pl.ds(r, 1), :], sem.at[slot])

    def for_rows(fn):
        def body(r, carry):
            fn(r)
            return carry
        lax.fori_loop(0, tm, body, 0, unroll=GATHER_UNROLL)

    @pl.when(t < nu)
    def _():
        @pl.when(ph == 0)
        def _():
            @pl.when(t == 0)
            def _():
                for_rows(lambda r: row_copy(t, r).start())

            for_rows(lambda r: row_copy(t, r).wait())
            for c in range(nk):
                xb_s[c] = xf_s[t % 2, :, c * kc:(c + 1) * kc].astype(BF16)

            @pl.when(t + 1 < nu)
            def _():
                for_rows(lambda r: row_copy(t + 1, r).start())

        @pl.when(ph < nk)
        def _():
            x = xb_s[ph]
            pg = _dot(x, wg_ref[0, 0].astype(BF16))
            pu = _dot(x, wu_ref[0, 0].astype(BF16))

            @pl.when(ph == 0)
            def _():
                hg_s[...] = pg
                hu_s[...] = pu

            @pl.when(ph != 0)
            def _():
                hg_s[...] += pg
                hu_s[...] += pu

        @pl.when(ph == nk - 1)
        def _():
            hg = hg_s[...]
            a_s[...] = (hg * jax.nn.sigmoid(hg) * hu_s[...]).astype(BF16)

        @pl.when(ph >= nk)
        def _():
            o_ref[...] = _dot(a_s[...], wd_ref[0, 0].astype(BF16))


def _moe_experts(h, src, w_gate, w_up, w_down, tile_expert, tile_index, n_used, layer):
    tm = MOE_TILE
    n_tiles = tile_expert.shape[0]
    d = w_gate.shape[2]
    f = w_gate.shape[3]
    kc = nc = MOE_CHUNK
    nk = d // kc
    nn = d // nc

    def used(t, nu, a, b):
        return jnp.where(t < nu[0], a, b)

    def k_idx(t, ph, nu):
        return used(t, nu, jnp.minimum(ph, nk - 1), nk - 1)

    def n_idx(t, ph, nu):
        return used(t, nu, jnp.maximum(ph - nk, 0), nn - 1)

    grid_spec = pltpu.PrefetchScalarGridSpec(
        num_scalar_prefetch=4, grid=(n_tiles, nk + nn),
        in_specs=[pl.BlockSpec(memory_space=pl.ANY),
                  pl.BlockSpec((1, 1, kc, f), lambda t, ph, te, ti, nu, src: (layer, te[t], k_idx(t, ph, nu), 0)),
                  pl.BlockSpec((1, 1, kc, f), lambda t, ph, te, ti, nu, src: (layer, te[t], k_idx(t, ph, nu), 0)),
                  pl.BlockSpec((1, 1, f, nc), lambda t, ph, te, ti, nu, src: (layer, te[t], 0, n_idx(t, ph, nu)))],
        out_specs=pl.BlockSpec((tm, nc), lambda t, ph, te, ti, nu, src: (ti[t], n_idx(t, ph, nu))),
        scratch_shapes=[pltpu.VMEM((2, tm, d), F32), pltpu.VMEM((nk, tm, kc), BF16),
                        pltpu.VMEM((tm, f), F32), pltpu.VMEM((tm, f), F32), pltpu.VMEM((tm, f), BF16),
                        pltpu.SemaphoreType.DMA((2,))])
    return pl.pallas_call(
        _moe_kernel,
        out_shape=jax.ShapeDtypeStruct((n_tiles * tm, d), F32),
        grid_spec=grid_spec,
        compiler_params=_params(("arbitrary", "arbitrary"), 48),
        name="moe_experts",
    )(tile_expert, tile_index, n_used, src, h, w_gate, w_up, w_down)


def _combine_kernel(alpha, row0, n_tok, pos_ref, h_ref, w_ref, g_ref, b_ref, ys_ref, o_ref, f_s, sem):
    i = pl.program_id(0)
    tm, d = o_ref.shape

    def row_copy(tile, s, r):
        slot = tile % 2
        return pltpu.make_async_copy(ys_ref.at[pl.ds(pos_ref[s * n_tok + row0 + tile * tm + r], 1), :],
                                     f_s.at[slot, s, pl.ds(r, 1), :], sem.at[slot])

    def for_rows(fn):
        def body(r, carry):
            fn(0, r)
            fn(1, r)
            return carry
        lax.fori_loop(0, tm, body, 0, unroll=GATHER_UNROLL)

    @pl.when(i == 0)
    def _():
        for_rows(lambda s, r: row_copy(i, s, r).start())

    @pl.when(i + 1 < pl.num_programs(0))
    def _():
        for_rows(lambda s, r: row_copy(i + 1, s, r).start())

    for_rows(lambda s, r: row_copy(i, s, r).wait())
    w = w_ref[...]
    v = alpha * h_ref[...] + w[:, 0:1] * f_s[i % 2, 0] + w[:, 1:2] * f_s[i % 2, 1]
    mu = jnp.mean(v, axis=1, keepdims=True)
    vc = v - mu
    var = jnp.sum(vc * vc, axis=1, keepdims=True) * (1.0 / d)
    o_ref[...] = vc * lax.rsqrt(var + LN_EPS) * g_ref[...] + b_ref[...]


def _combine(h, ys, pos, wts, ln_g, ln_b, alpha, row0, nrows):
    n, d = h.shape
    tm = next(c for c in (128, 64, 32, 16, 8) if nrows % c == 0 and row0 % c == 0)
    o0 = row0 // tm
    rows = pl.BlockSpec((1, d), lambda i, pos: (0, 0))
    grid_spec = pltpu.PrefetchScalarGridSpec(
        num_scalar_prefetch=1, grid=(nrows // tm,),
        in_specs=[pl.BlockSpec((tm, d), lambda i, pos: (i + o0, 0)),
                  pl.BlockSpec((tm, LANES), lambda i, pos: (i + o0, 0)), rows, rows,
                  pl.BlockSpec(memory_space=pl.ANY)],
        out_specs=pl.BlockSpec((tm, d), lambda i, pos: (i, 0)),
        scratch_shapes=[pltpu.VMEM((2, 2, tm, d), F32), pltpu.SemaphoreType.DMA((2,))])
    return pl.pallas_call(
        functools.partial(_combine_kernel, alpha, row0, n),
        out_shape=jax.ShapeDtypeStruct((nrows, d), F32),
        grid_spec=grid_spec,
        compiler_params=_params(("arbitrary",), 32),
        name="combine_ln",
    )(pos, h, wts, ln_g, ln_b, ys)


def _routing_tables(ids, n_tiles):
    n = ids.shape[0]
    tm = MOE_TILE
    e = ids[:, :2].reshape(-1)
    onehot = (e[:, None] == jnp.arange(N_EXPERTS, dtype=I32)[None, :]).astype(I32)
    csum = jnp.cumsum(onehot, axis=0)
    rank = jnp.sum((csum - onehot) * onehot, axis=1)
    counts = csum[-1]
    tiles = (counts + tm - 1) // tm
    tile_end = jnp.cumsum(tiles)
    n_used = tile_end[-1]
    pos = ((tile_end - tiles)[e] * tm + rank).astype(I32)
    src = (jnp.arange(n_tiles * tm, dtype=I32) % n).at[pos].set(jnp.arange(2 * n, dtype=I32) // 2)
    t = jnp.minimum(jnp.arange(n_tiles, dtype=I32), n_used - 1)
    tile_expert = jnp.minimum(jnp.sum((tile_end[None, :] <= t[:, None]).astype(I32), axis=1), N_EXPERTS - 1)
    pos2 = pos.reshape(n, 2)
    back = jnp.concatenate([pos2[:, 0], pos2[:, 1]])
    return src, back, tile_expert, t, n_used.reshape(1).astype(I32)


def _layer(l, x_p, x_s, bsz, seq, cache_k, cache_v, state_c, state_n, state_m, page_table,
           w_in, b_gates, norm_g, w_out, ln1_g, ln1_b, w_group, b_group, w_router, b_router,
           w_e_gate, w_e_up, w_e_down, ln2_g, ln2_b, alpha):
    n_p, d = x_p.shape
    n_s = x_s.shape[0]
    n = n_p + n_s

    wt = jnp.transpose(w_in[l]).astype(BF16)
    wg = jnp.pad(w_in[l, :, COL_G:], ((0, 0), (0, LANES - 2 * M_HEADS)))
    bg = jnp.pad(b_gates[l], (0, LANES - 2 * M_HEADS)).reshape(1, LANES)
    z_p = _in_proj(x_p.astype(BF16), wt, COL_G)
    z_s = _in_proj(x_s.astype(BF16), wt, COL_G)
    g_p = _gates(x_p, wg, bg)
    g_s = _gates(x_s, wg, bg)

    ng = norm_g[l].reshape(1, M_V_W)
    att_p = _moba_prompt(z_p, bsz, seq)
    mix_p, c_p, n_pr, m_p = _mlstm_prompt(z_p, g_p, ng, bsz, seq)
    att_s = _moba_sample(z_s, cache_k, cache_v, page_table, l)
    mix_s, c_s, n_sm, m_s = _mlstm_sample(z_s, g_s, state_c, state_n, state_m, ng, l)

    wr = jnp.pad(jnp.concatenate([w_group[l], w_router[l]], axis=1),
                 ((0, 0), (0, LANES - N_GROUPS - N_EXPERTS))).astype(BF16)
    br = jnp.pad(jnp.concatenate([b_group[l], b_router[l]]), (0, LANES - N_GROUPS - N_EXPERTS)).reshape(1, LANES)
    wob = w_out[l].astype(BF16)
    g1, b1 = ln1_g[l].reshape(1, d), ln1_b[l].reshape(1, d)
    done = _merge(att_p, mix_p, wob, x_p, g1, b1, wr, br, alpha, n, 0)
    h2, ids, wts = _merge(att_s, mix_s.reshape(n_s, M_V_W), wob, x_s, g1, b1, wr, br, alpha, n, n_p, prev=done)

    n_tiles = (2 * n) // MOE_TILE + N_EXPERTS + 1
    src, back, tile_expert, tile_index, n_used = _routing_tables(ids, n_tiles)
    ys = _moe_experts(h2, src, w_e_gate, w_e_up, w_e_down, tile_expert, tile_index, n_used, l)
    g2, b2 = ln2_g[l].reshape(1, d), ln2_b[l].reshape(1, d)
    y_p = _combine(h2, ys, back, wts, g2, b2, alpha, 0, n_p)
    y_s = _combine(h2, ys, back, wts, g2, b2, alpha, n_p, n_s)

    kv = lambda rows, col: rows[:, col:col + ATT_KV_W]
    return (y_p, y_s,
            kv(z_p, COL_K).reshape(bsz, seq, ATT_KV_HEADS, HEAD_DIM),
            kv(z_p, COL_V).reshape(bsz, seq, ATT_KV_HEADS, HEAD_DIM),
            kv(z_s, COL_K).reshape(n_s, 1, ATT_KV_HEADS, HEAD_DIM),
            kv(z_s, COL_V).reshape(n_s, 1, ATT_KV_HEADS, HEAD_DIM),
            c_p, n_pr.reshape(bsz, M_HEADS, M_QK), m_p[:, :, 0, 0],
            c_s, n_sm.reshape(n_s, M_HEADS, M_QK), m_s[:, 0, :M_HEADS])


def kernel(x_prompt, x_sample, cache_k, cache_v, state_mlstm_c, state_mlstm_n, state_mlstm_m, page_table,
           w_in, b_gates, mlstm_norm_g, w_out, ln1_g, ln1_b, w_group, b_group, w_router, b_router,
           w_e_gate, w_e_up, w_e_down, ln2_g, ln2_b):
    bsz, seq, d = x_prompt.shape
    n_s, dec_seq, _ = x_sample.shape
    depth = w_in.shape[0]
    assert dec_seq == 1 and seq % MOBA_BLOCK == 0 and w_in.shape[2] == COL_G + 2 * M_HEADS
    assert (page_table.shape[1] * cache_k.shape[2]) % MOBA_BLOCK == 0 and MOBA_BLOCK % cache_k.shape[2] == 0
    alpha = (2 * depth) ** 0.25
    x_p = x_prompt.reshape(bsz * seq, d)
    x_s = x_sample.reshape(n_s, d)
    per_layer = []
    for l in range(depth):
        outs = _layer(l, x_p, x_s, bsz, seq, cache_k, cache_v, state_mlstm_c, state_mlstm_n, state_mlstm_m,
                      page_table, w_in, b_gates, mlstm_norm_g, w_out, ln1_g, ln1_b, w_group, b_group, w_router,
                      b_router, w_e_gate, w_e_up, w_e_down, ln2_g, ln2_b, alpha)
        x_p, x_s = outs[0], outs[1]
        per_layer.append(outs[2:])
    stacked = [jnp.stack([p[i] for p in per_layer]) for i in range(10)]
    return (x_p.reshape(bsz, seq, d), x_s.reshape(n_s, 1, d), *stacked)
```

```python
import functools

import jax
import jax.numpy as jnp
from jax import lax
from jax.experimental import pallas as pl
from jax.experimental.pallas import tpu as pltpu

F32, BF16, I32 = jnp.float32, jnp.bfloat16, jnp.int32

SUBLANES = 8
LANES = 128
MIB = 1024 * 1024

HEAD_DIM = 128
ATT_HEADS = 16
ATT_KV_HEADS = 8
ATT_GROUP = ATT_HEADS // ATT_KV_HEADS
ATT_SCALE = HEAD_DIM ** -0.5
LOG2_E = 1.4426950408889634
MOBA_BLOCK = 256
MOBA_TOPK = 3
M_HEADS = 4
M_QK = 256
M_V = 512
M_Q_SCALE = M_QK ** -0.5
N_GROUPS = 4
EXPERTS_PER_GROUP = 8
N_EXPERTS = N_GROUPS * EXPERTS_PER_GROUP
LN_EPS = 1e-5
NORM_EPS = 1e-6

ATT_Q_W = ATT_HEADS * HEAD_DIM
ATT_KV_W = ATT_KV_HEADS * HEAD_DIM
M_QK_W = M_HEADS * M_QK
M_V_W = M_HEADS * M_V
COL_Q = 0
COL_K = COL_Q + ATT_Q_W
COL_V = COL_K + ATT_KV_W
COL_QM = COL_V + ATT_KV_W
COL_KM = COL_QM + M_QK_W
COL_VM = COL_KM + M_QK_W
COL_OM = COL_VM + M_V_W
COL_G = COL_OM + M_V_W

MOE_TILE = 512
MOE_CHUNK = 1024
GATHER_UNROLL = 8
GATHER_PRIORITY = 1


def _params(sem, vmem_mib):
    return pltpu.CompilerParams(dimension_semantics=sem, vmem_limit_bytes=vmem_mib * MIB)


def _dot(a, b):
    return jnp.dot(a, b, preferred_element_type=F32)


def _dot_nt(a, b):
    return lax.dot_general(a, b, (((1,), (1,)), ((), ())), preferred_element_type=F32)


def _log_sigmoid(x):
    return jnp.minimum(x, 0.0) - jnp.log1p(jnp.exp(-jnp.abs(x)))


def _tile(n, candidates):
    for c in candidates:
        if n % c == 0:
            return c
    raise ValueError(f"no tile for {n} in {candidates}")


def _matmul_nt_kernel(x_ref, w_ref, o_ref):
    o_ref[...] = _dot_nt(x_ref[...], w_ref[...])


def _in_proj(xb, wt, w):
    n, d = xb.shape
    tm = _tile(n, (1024, 832, 640, 512, 256, 128, 64))
    tn = _tile(w, (1024, 512, 256, 128))
    return pl.pallas_call(
        _matmul_nt_kernel,
        out_shape=jax.ShapeDtypeStruct((n, w), F32),
        grid=(w // tn, n // tm),
        in_specs=[pl.BlockSpec((tm, d), lambda j, i: (i, 0)),
                  pl.BlockSpec((tn, d), lambda j, i: (j, 0))],
        out_specs=pl.BlockSpec((tm, tn), lambda j, i: (i, j)),
        compiler_params=_params(("parallel", "parallel"), 48),
        name="in_proj",
    )(xb, wt)


def _gates_kernel(x_ref, w_ref, b_ref, o_ref):
    x = x_ref[...]
    w = w_ref[...]
    xh = x.astype(BF16)
    xl = (x - xh.astype(F32)).astype(BF16)
    wh = w.astype(BF16)
    wl = (w - wh.astype(F32)).astype(BF16)
    o_ref[...] = _dot(xh, wh) + _dot(xl, wh) + _dot(xh, wl) + b_ref[...]


def _gates(x, wg, bg):
    n, d = x.shape
    tm = _tile(n, (512, 416, 320, 256, 128, 64))
    return pl.pallas_call(
        _gates_kernel,
        out_shape=jax.ShapeDtypeStruct((n, LANES), F32),
        grid=(n // tm,),
        in_specs=[pl.BlockSpec((tm, d), lambda i: (i, 0)),
                  pl.BlockSpec((d, LANES), lambda i: (0, 0)),
                  pl.BlockSpec((1, LANES), lambda i: (0, 0))],
        out_specs=pl.BlockSpec((tm, LANES), lambda i: (i, 0)),
        compiler_params=_params(("parallel",), 40),
        name="gates",
    )(x, wg, bg)


def _moba_prompt_kernel(q_ref, k_ref, v_ref, o_ref, kb_s, vt_s, km_s):
    i = pl.program_id(2)
    nb = kb_s.shape[0]
    blk = MOBA_BLOCK
    nq = ATT_GROUP * blk

    @pl.when(i == 0)
    def _():
        for j in range(nb):
            kj = k_ref[j * blk:(j + 1) * blk, :]
            kb_s[j] = kj.astype(BF16)
            km_s[j:j + 1, :] = jnp.mean(kj, axis=0, keepdims=True)
            vt_s[j] = v_ref[j * blk:(j + 1) * blk, :].T.astype(BF16)
        km_s[nb:, :] = jnp.zeros((km_s.shape[0] - nb, HEAD_DIM), F32)

    q = q_ref[...]
    q2 = jnp.concatenate([q[:, g * HEAD_DIM:(g + 1) * HEAD_DIM] for g in range(ATT_GROUP)], axis=0).astype(BF16)

    gate = _dot_nt(km_s[...].astype(BF16), q2)
    jj = lax.broadcasted_iota(I32, gate.shape, 0)
    g = jnp.where(jj < i, gate, -jnp.inf)
    sel = jnp.zeros(gate.shape, F32)
    for _ in range(MOBA_TOPK):
        mx = jnp.max(g, axis=0, keepdims=True)
        first = jnp.min(jnp.where(g == mx, jj, 2 * nb), axis=0, keepdims=True)
        pick = (jj == first) & (mx > -jnp.inf)
        sel = jnp.where(pick, 1.0, sel)
        g = jnp.where(pick, -jnp.inf, g)
    ri = lax.broadcasted_iota(I32, (blk, nq), 0)
    ci = lax.broadcasted_iota(I32, (blk, nq), 1)
    causal = ri <= (ci & (blk - 1))
    c2 = ATT_SCALE * LOG2_E

    def attend_all(own):
        m = l = acc = None
        for j in (own,) + tuple(range(own)):
            s = _dot_nt(kb_s[j], q2)
            s = jnp.where(causal if j == own else sel[j:j + 1, :] > 0.5, s, -jnp.inf)
            m_blk = jnp.max(s, axis=0, keepdims=True)
            m_new = m_blk if m is None else jnp.maximum(m, m_blk)
            p = jnp.exp2((s - m_new) * c2)
            pv = _dot(vt_s[j], p.astype(BF16))
            if m is None:
                l, acc = jnp.sum(p, axis=0, keepdims=True), pv
            else:
                alpha = jnp.exp2((m - m_new) * c2)
                l = alpha * l + jnp.sum(p, axis=0, keepdims=True)
                acc = alpha * acc + pv
            m = m_new
        out = (acc / l).T
        for g_i in range(ATT_GROUP):
            o_ref[:, g_i * HEAD_DIM:(g_i + 1) * HEAD_DIM] = out[g_i * blk:(g_i + 1) * blk, :].astype(o_ref.dtype)

    for own in range(nb):
        pl.when(i == own)(functools.partial(attend_all, own))


def _moba_prompt(z, bsz, seq):
    nb = seq // MOBA_BLOCK
    qw = ATT_GROUP * HEAD_DIM
    return pl.pallas_call(
        _moba_prompt_kernel,
        out_shape=jax.ShapeDtypeStruct((bsz * seq, ATT_Q_W), BF16),
        grid=(bsz, ATT_KV_HEADS, nb),
        in_specs=[pl.BlockSpec((MOBA_BLOCK, qw), lambda b, h, i: (b * nb + i, COL_Q // qw + h)),
                  pl.BlockSpec((seq, HEAD_DIM), lambda b, h, i: (b, COL_K // HEAD_DIM + h)),
                  pl.BlockSpec((seq, HEAD_DIM), lambda b, h, i: (b, COL_V // HEAD_DIM + h))],
        out_specs=pl.BlockSpec((MOBA_BLOCK, qw), lambda b, h, i: (b * nb + i, h)),
        scratch_shapes=[pltpu.VMEM((nb, MOBA_BLOCK, HEAD_DIM), BF16),
                        pltpu.VMEM((nb, HEAD_DIM, MOBA_BLOCK), BF16),
                        pltpu.VMEM((2 * SUBLANES, HEAD_DIM), F32)],
        compiler_params=_params(("parallel", "parallel", "arbitrary"), 32),
        name="moba_prompt",
    )(z, z, z)


def _moba_sample_kernel(n_pages, pt_ref, q_ref, kn_ref, vn_ref, *rest):
    del pt_ref
    kp = rest[:n_pages]
    vp = rest[n_pages:2 * n_pages]
    o_ref = rest[2 * n_pages]
    page = kp[0].shape[2]
    per_blk = MOBA_BLOCK // page
    nb = n_pages // per_blk
    nrow = page * ATT_KV_HEADS
    both = lambda a: jnp.concatenate([a] * ATT_GROUP, axis=0)

    q = q_ref[0]
    qb = q.astype(BF16)
    ri = lax.broadcasted_iota(I32, (ATT_HEADS, nrow), 0)
    ci = lax.broadcasted_iota(I32, (ATT_HEADS, nrow), 1)
    own = (ci % ATT_KV_HEADS) == (ri % ATT_KV_HEADS)

    scores, ksum = [], []
    for p in range(n_pages):
        k3 = kp[p][0, 0]
        scores.append(_dot_nt(qb, k3.reshape(nrow, HEAD_DIM).astype(BF16)) * ATT_SCALE)
        ksum.append(jnp.sum(k3, axis=0))
    gates = []
    for j in range(nb):
        kmean = sum(ksum[j * per_blk:(j + 1) * per_blk]) * (1.0 / MOBA_BLOCK)
        gates.append(jnp.sum(q * both(kmean), axis=1, keepdims=True))
    sel = []
    for j in range(nb):
        rank = jnp.zeros(gates[j].shape, F32)
        for jp in range(nb):
            if jp < j:
                rank = rank + (gates[jp] >= gates[j]).astype(F32)
            elif jp > j:
                rank = rank + (gates[jp] > gates[j]).astype(F32)
        sel.append(rank < MOBA_TOPK)

    s_new = jnp.sum(q * both(kn_ref[0]), axis=1, keepdims=True) * ATT_SCALE
    m = s_new
    for p in range(n_pages):
        scores[p] = jnp.where(sel[p // per_blk], jnp.where(own, scores[p], -jnp.inf), -jnp.inf)
        m = jnp.maximum(m, jnp.max(scores[p], axis=1, keepdims=True))
    p_new = jnp.exp(s_new - m)
    l = p_new
    acc = p_new * both(vn_ref[0])
    for p in range(n_pages):
        pp = jnp.exp(scores[p] - m)
        l = l + jnp.sum(pp, axis=1, keepdims=True)
        acc = acc + _dot(pp.astype(BF16), vp[p][0, 0].reshape(nrow, HEAD_DIM).astype(BF16))
    o_ref[0] = (acc / l).astype(o_ref.dtype)


def _moba_sample(zs, cache_k, cache_v, page_table, layer):
    ns = zs.shape[0]
    n_pages = page_table.shape[1]
    page = cache_k.shape[2]
    pt_flat = page_table.reshape(-1).astype(I32)
    qg = jnp.transpose(zs[:, COL_Q:COL_Q + ATT_Q_W].reshape(ns, ATT_KV_HEADS, ATT_GROUP, HEAD_DIM), (0, 2, 1, 3))
    kv_new = lambda col: zs[:, col:col + ATT_KV_W].reshape(ns, ATT_KV_HEADS, HEAD_DIM)

    def page_spec(p):
        return pl.BlockSpec((1, 1, page, ATT_KV_HEADS, HEAD_DIM),
                            lambda s, pt: (layer, pt[s * n_pages + p], 0, 0, 0))

    rows = lambda r: pl.BlockSpec((1, r, HEAD_DIM), lambda s, pt: (s, 0, 0))
    grid_spec = pltpu.PrefetchScalarGridSpec(
        num_scalar_prefetch=1, grid=(ns,),
        in_specs=[rows(ATT_HEADS), rows(ATT_KV_HEADS), rows(ATT_KV_HEADS)] + [page_spec(p) for p in range(n_pages)] * 2,
        out_specs=rows(ATT_HEADS))
    out = pl.pallas_call(
        functools.partial(_moba_sample_kernel, n_pages),
        out_shape=jax.ShapeDtypeStruct((ns, ATT_HEADS, HEAD_DIM), BF16),
        grid_spec=grid_spec,
        compiler_params=_params(("parallel",), 48),
        name="moba_sample",
    )(pt_flat, qg.reshape(ns, ATT_HEADS, HEAD_DIM), kv_new(COL_K), kv_new(COL_V),
      *([cache_k] * n_pages), *([cache_v] * n_pages))
    return jnp.transpose(out.reshape(ns, ATT_GROUP, ATT_KV_HEADS, HEAD_DIM), (0, 2, 1, 3)).reshape(ns, ATT_Q_W)


def _mlstm_prompt_kernel(q_ref, k_ref, v_ref, og_ref, igr_ref, fgr_ref, igc_ref, fgc_ref, ng_ref,
                         y_ref, c_ref, n_ref, m_ref):
    @pl.when(pl.program_id(2) == 0)
    def _():
        c_ref[...] = jnp.zeros(c_ref.shape, F32)
        n_ref[...] = jnp.zeros(n_ref.shape, F32)
        m_ref[...] = jnp.zeros(m_ref.shape, F32)

    L = q_ref.shape[0]
    q = q_ref[...] * M_Q_SCALE
    k = k_ref[...]
    ig_r = igr_ref[0, 0]
    lf_r = _log_sigmoid(fgr_ref[0, 0])
    ig_c = igc_ref[0, 0]
    lf_c = _log_sigmoid(fgc_ref[0, 0])
    ri = lax.broadcasted_iota(I32, (L, L), 0)
    ci = lax.broadcasted_iota(I32, (L, L), 1)
    tril = ci <= ri
    b_c = jnp.sum(jnp.where(tril, lf_r, 0.0), axis=1, keepdims=True)
    b_r = jnp.sum(jnp.where(ri <= ci, lf_c, 0.0), axis=0, keepdims=True)
    r_r = ig_r - b_r
    m0 = m_ref[0, 0][:, 0:1]
    m_c = b_c + jnp.maximum(m0, jnp.max(jnp.where(tril, r_r, -jnp.inf), axis=1, keepdims=True))
    dmat = jnp.exp(jnp.where(tril, b_c - m_c + r_r, -jnp.inf))
    qb = q.astype(BF16)
    vb = v_ref[...].astype(BF16)
    s = _dot_nt(qb, k.astype(BF16)) * dmat
    inter = jnp.exp(b_c + m0 - m_c)
    c0 = c_ref[0, 0]
    n0 = n_ref[0, 0]
    num = inter * _dot(qb, c0.astype(BF16)) + _dot(s.astype(BF16), vb)
    den = inter * jnp.sum(q * n0, axis=1, keepdims=True) + jnp.sum(s, axis=1, keepdims=True)
    h = num / jnp.maximum(jnp.abs(den), jnp.exp(-m_c))
    hn = h * lax.rsqrt(jnp.mean(h * h, axis=1, keepdims=True) + NORM_EPS)
    y_ref[...] = (hn * ng_ref[...] * jax.nn.sigmoid(og_ref[...])).astype(y_ref.dtype)

    b_last = b_c[L - 1:L, :]
    m_last = m_c[L - 1:L, :]
    kw = k * jnp.exp(b_last - b_c + ig_c - m_last)
    decay = jnp.exp(b_last + m0 - m_last)
    c_ref[0, 0] = decay * c0 + _dot(kw.T.astype(BF16), vb)
    n_ref[0, 0] = decay * n0 + jnp.sum(kw, axis=0, keepdims=True)
    m_ref[0, 0] = jnp.broadcast_to(m_last, (1, LANES))


def _mlstm_prompt(z, g_all, norm_g, bsz, seq):
    L = _tile(seq, (256, 128))
    nc = seq // L
    gp = jnp.transpose(g_all[:, :2 * M_HEADS].reshape(bsz, seq, 2 * M_HEADS), (0, 2, 1))
    g_row = gp.reshape(bsz, 2 * M_HEADS, 1, seq)
    g_col = gp.reshape(bsz, 2 * M_HEADS, seq, 1)
    row = lambda off: pl.BlockSpec((1, 1, 1, L), lambda b, h, c: (b, off + h, 0, c))
    col = lambda off: pl.BlockSpec((1, 1, L, 1), lambda b, h, c: (b, off + h, c, 0))
    state = lambda shape: pl.BlockSpec((1, 1) + shape, lambda b, h, c: (b, h, 0, 0))
    return pl.pallas_call(
        _mlstm_prompt_kernel,
        out_shape=(jax.ShapeDtypeStruct((bsz * seq, M_V_W), BF16),
                   jax.ShapeDtypeStruct((bsz, M_HEADS, M_QK, M_V), F32),
                   jax.ShapeDtypeStruct((bsz, M_HEADS, 1, M_QK), F32),
                   jax.ShapeDtypeStruct((bsz, M_HEADS, 1, LANES), F32)),
        grid=(bsz, M_HEADS, nc),
        in_specs=[pl.BlockSpec((L, M_QK), lambda b, h, c: (b * nc + c, COL_QM // M_QK + h)),
                  pl.BlockSpec((L, M_QK), lambda b, h, c: (b * nc + c, COL_KM // M_QK + h)),
                  pl.BlockSpec((L, M_V), lambda b, h, c: (b * nc + c, COL_VM // M_V + h)),
                  pl.BlockSpec((L, M_V), lambda b, h, c: (b * nc + c, COL_OM // M_V + h)),
                  row(0), row(M_HEADS), col(0), col(M_HEADS),
                  pl.BlockSpec((1, M_V), lambda b, h, c: (0, h))],
        out_specs=(pl.BlockSpec((L, M_V), lambda b, h, c: (b * nc + c, h)),
                   state((M_QK, M_V)), state((1, M_QK)), state((1, LANES))),
        compiler_params=_params(("parallel", "parallel", "arbitrary"), 32),
        name="mlstm_prompt",
    )(z, z, z, z, g_row, g_row, g_col, g_col, norm_g)


def _mlstm_sample_kernel(qr_ref, kr_ref, vr_ref, or_ref, qc_ref, kc_ref, g_ref, m0_ref, n0_ref, c0_ref, ng_ref,
                         y_ref, c_ref, n_ref, m_ref):
    g = g_ref[0]
    m0_all = m0_ref[0]
    lane = lax.broadcasted_iota(I32, (1, LANES), 1)
    m_out = jnp.zeros((1, LANES), F32)
    for h in range(M_HEADS):
        ig = g[:, h:h + 1]
        lf = _log_sigmoid(g[:, M_HEADS + h:M_HEADS + h + 1])
        m0 = m0_all[:, h:h + 1]
        m = jnp.maximum(lf + m0, ig)
        w = jnp.exp(ig - m)
        decay = jnp.exp(lf + m0 - m)
        q_r = qr_ref[0, h] * M_Q_SCALE
        k_r = kr_ref[0, h]
        v_r = vr_ref[0, h]
        q_c = qc_ref[0, h] * M_Q_SCALE
        c0 = c0_ref[0, 0, h]
        n0 = n0_ref[0, 0, h]
        s = jnp.sum(q_r * k_r, axis=1, keepdims=True) * w
        num = decay * jnp.sum(q_c * c0, axis=0, keepdims=True) + s * v_r
        den = decay * jnp.sum(q_r * n0, axis=1, keepdims=True) + s
        hh = num / jnp.maximum(jnp.abs(den), jnp.exp(-m))
        hn = hh * lax.rsqrt(jnp.mean(hh * hh, axis=1, keepdims=True) + NORM_EPS)
        y = hn * ng_ref[:, h * M_V:(h + 1) * M_V] * jax.nn.sigmoid(or_ref[0, h])
        y_ref[0, :, h * M_V:(h + 1) * M_V] = y.astype(y_ref.dtype)
        c_ref[0, h] = decay * c0 + (w * kc_ref[0, h]) * v_r
        n_ref[0, h] = decay * n0 + w * k_r
        m_out = jnp.where(lane == h, m, m_out)
    m_ref[0] = m_out


def _mlstm_sample(zs, gs, state_c, state_n, state_m, norm_g, layer):
    ns = zs.shape[0]
    qm = zs[:, COL_QM:COL_QM + M_QK_W]
    km = zs[:, COL_KM:COL_KM + M_QK_W]
    rows = lambda a, w: a.reshape(ns, M_HEADS, 1, w)
    m0 = jnp.pad(state_m[layer], ((0, 0), (0, LANES - M_HEADS))).reshape(ns, 1, LANES)
    row_spec = lambda w: pl.BlockSpec((1, M_HEADS, 1, w), lambda s: (s, 0, 0, 0))
    lane_spec = pl.BlockSpec((1, 1, LANES), lambda s: (s, 0, 0))
    col_spec = pl.BlockSpec((1, M_HEADS, M_QK, 1), lambda s: (s, 0, 0, 0))
    return pl.pallas_call(
        _mlstm_sample_kernel,
        out_shape=(jax.ShapeDtypeStruct((ns, 1, M_V_W), BF16),
                   jax.ShapeDtypeStruct((ns, M_HEADS, M_QK, M_V), F32),
                   jax.ShapeDtypeStruct((ns, M_HEADS, 1, M_QK), F32),
                   jax.ShapeDtypeStruct((ns, 1, LANES), F32)),
        grid=(ns,),
        in_specs=[row_spec(M_QK), row_spec(M_QK), row_spec(M_V), row_spec(M_V), col_spec, col_spec,
                  lane_spec, lane_spec,
                  pl.BlockSpec((1, 1, M_HEADS, 1, M_QK), lambda s: (layer, s, 0, 0, 0)),
                  pl.BlockSpec((1, 1, M_HEADS, M_QK, M_V), lambda s: (layer, s, 0, 0, 0)),
                  pl.BlockSpec((1, M_V_W), lambda s: (0, 0))],
        out_specs=(pl.BlockSpec((1, 1, M_V_W), lambda s: (s, 0, 0)),
                   pl.BlockSpec((1, M_HEADS, M_QK, M_V), lambda s: (s, 0, 0, 0)),
                   row_spec(M_QK), lane_spec),
        compiler_params=_params(("parallel",), 32),
        name="mlstm_sample",
    )(rows(qm, M_QK), rows(km, M_QK), rows(zs[:, COL_VM:COL_VM + M_V_W], M_V), rows(zs[:, COL_OM:COL_OM + M_V_W], M_V),
      qm.reshape(ns, M_HEADS, M_QK, 1), km.reshape(ns, M_HEADS, M_QK, 1),
      gs.reshape(ns, 1, LANES), m0,
      state_n.reshape(state_n.shape[0], ns, M_HEADS, 1, M_QK), state_c, norm_g)


def _layernorm_chunks(chunks, d):
    total = sum(jnp.sum(c, axis=1, keepdims=True) for c in chunks)
    mu = total * (1.0 / d)
    var = sum(jnp.sum(jnp.square(c - mu), axis=1, keepdims=True) for c in chunks) * (1.0 / d)
    return mu, lax.rsqrt(var + LN_EPS)


def _merge_kernel(alpha, *refs):
    att_ref, mix_ref, wa_ref, wb_ref, x_ref, g_ref, b_ref, wr_ref, br_ref = refs[:9]
    h_ref, ids_ref, wts_ref = refs[-3:]
    j = pl.program_id(1)
    tn = x_ref.shape[1]
    d = h_ref.shape[1]
    nj = d // tn
    cols = lambda c: slice(c * tn, (c + 1) * tn)
    pre = alpha * x_ref[...] + _dot(att_ref[...], wa_ref[...]) + _dot(mix_ref[...], wb_ref[...])
    for c in range(nj):
        @pl.when(j == c)
        def _(c=c):
            h_ref[:, cols(c)] = pre

    @pl.when(j == nj - 1)
    def _():
        mu, rstd = _layernorm_chunks([h_ref[:, cols(c)] for c in range(nj)], d)
        lg = br_ref[...]
        for c in range(nj):
            y = (h_ref[:, cols(c)] - mu) * rstd * g_ref[:, cols(c)] + b_ref[:, cols(c)]
            h_ref[:, cols(c)] = y
            lg = lg + _dot(y.astype(BF16), wr_ref[cols(c), :])
        lane = lax.broadcasted_iota(I32, lg.shape, 1)
        lane_f = lane.astype(F32)
        far = float(LANES)
        gl = jnp.where(lane < N_GROUPS, lg, -jnp.inf)
        gmax = jnp.max(gl, axis=1, keepdims=True)
        gsel = jnp.min(jnp.where(gl == gmax, lane_f, far), axis=1, keepdims=True).astype(I32)
        gw = 1.0 / jnp.sum(jnp.where(lane < N_GROUPS, jnp.exp(gl - gmax), 0.0), axis=1, keepdims=True)
        e_lane = lane - N_GROUPS
        in_grp = (e_lane >= 0) & (e_lane < N_EXPERTS) & ((e_lane // EXPERTS_PER_GROUP) == gsel)
        el = jnp.where(in_grp, lg, -jnp.inf)
        t1 = jnp.max(el, axis=1, keepdims=True)
        i1 = jnp.min(jnp.where(el == t1, lane_f, far), axis=1, keepdims=True)
        el2 = jnp.where(lane_f == i1, -jnp.inf, el)
        t2 = jnp.max(el2, axis=1, keepdims=True)
        i2 = jnp.min(jnp.where(el2 == t2, lane_f, far), axis=1, keepdims=True)
        e21 = jnp.exp(t2 - t1)
        w1 = gw / (1.0 + e21)
        w2 = w1 * e21
        ids = jnp.where(lane == 0, i1, jnp.where(lane == 1, i2, float(N_GROUPS))) - float(N_GROUPS)
        ids_ref[...] = ids.astype(I32)
        wts_ref[...] = jnp.where(lane == 0, w1, jnp.where(lane == 1, w2, 0.0))


def _merge(att, mix, wob, x, ln_g, ln_b, wr, br, alpha, n_total, row0, prev=None):
    r, d = x.shape
    half = att.shape[1]
    tm = _tile(r, (512, 256, 128, 64))
    assert row0 % tm == 0
    o0 = row0 // tm
    tn = _tile(d, (1024, 512))
    rows = lambda w: pl.BlockSpec((1, w), lambda i, j: (0, 0))
    prev = () if prev is None else tuple(prev)
    n_in = 9
    return pl.pallas_call(
        functools.partial(_merge_kernel, alpha),
        out_shape=(jax.ShapeDtypeStruct((n_total, d), F32),
                   jax.ShapeDtypeStruct((n_total, LANES), I32),
                   jax.ShapeDtypeStruct((n_total, LANES), F32)),
        grid=(r // tm, d // tn),
        in_specs=[pl.BlockSpec((tm, half), lambda i, j: (i, 0)),
                  pl.BlockSpec((tm, half), lambda i, j: (i, 0)),
                  pl.BlockSpec((half, tn), lambda i, j: (0, j)),
                  pl.BlockSpec((half, tn), lambda i, j: (1, j)),
                  pl.BlockSpec((tm, tn), lambda i, j: (i, j)),
                  rows(d), rows(d),
                  pl.BlockSpec((d, LANES), lambda i, j: (0, 0)),
                  rows(LANES)] + [pl.BlockSpec(memory_space=pl.ANY)] * len(prev),
        out_specs=(pl.BlockSpec((tm, d), lambda i, j: (i + o0, 0)),
                   pl.BlockSpec((tm, LANES), lambda i, j: (i + o0, 0)),
                   pl.BlockSpec((tm, LANES), lambda i, j: (i + o0, 0))),
        input_output_aliases={n_in + k: k for k in range(len(prev))},
        compiler_params=_params(("parallel", "arbitrary"), 52),
        name="merge_ln_router",
    )(att, mix, wob, wob, x, ln_g, ln_b, wr, br, *prev)


def _moe_kernel(te_ref, ti_ref, nu_ref, src_ref, h_ref, wg_ref, wu_ref, wd_ref, o_ref,
                xf_s, xb_s, hg_s, hu_s, a_s, sem):
    del te_ref, ti_ref
    t = pl.program_id(0)
    ph = pl.program_id(1)
    nk, tm, kc = xb_s.shape
    n_phase = nk + xf_s.shape[2] // o_ref.shape[1]
    share = tm // n_phase
    assert share * n_phase == tm
    nu = nu_ref[0]

    def row_copy(tile, r):
        slot = tile % 2
        return pltpu.make_async_copy(h_ref.at[pl.ds(src_ref[tile * tm + r], 1), :],
                                     xf_s.at[slot, pl.ds(r, 1), :], sem.at[slot])

    def for_rows(lo, count, fn):
        def body(r, carry):
            fn(lo + r)
            return carry
        lax.fori_loop(0, count, body, 0, unroll=GATHER_UNROLL)

    @pl.when(t < nu)
    def _():
        @pl.when(t + 1 < nu)
        def _():
            for_rows(ph * share, share, lambda r: row_copy(t + 1, r).start(priority=GATHER_PRIORITY))

        @pl.when(ph == 0)
        def _():
            @pl.when(t == 0)
            def _():
                for_rows(0, tm, lambda r: row_copy(t, r).start(priority=GATHER_PRIORITY))

            for_rows(0, tm, lambda r: row_copy(t, r).wait())
            for c in range(nk):
                xb_s[c] = xf_s[t % 2, :, c * kc:(c + 1) * kc].astype(BF16)

        @pl.when(ph < nk)
        def _():
            x = xb_s[ph]
            pg = _dot(x, wg_ref[0, 0].astype(BF16))
            pu = _dot(x, wu_ref[0, 0].astype(BF16))

            @pl.when(ph == 0)
            def _():
                hg_s[...] = pg
                hu_s[...] = pu

            @pl.when(ph != 0)
            def _():
                hg_s[...] += pg
                hu_s[...] += pu

        @pl.when(ph == nk - 1)
        def _():
            hg = hg_s[...]
            a_s[...] = (hg * jax.nn.sigmoid(hg) * hu_s[...]).astype(BF16)

        @pl.when(ph >= nk)
        def _():
            o_ref[...] = _dot(a_s[...], wd_ref[0, 0].astype(BF16))


def _moe_experts(h, src, w_gate, w_up, w_down, tile_expert, tile_index, n_used, layer):
    tm = MOE_TILE
    n_tiles = tile_expert.shape[0]
    d = w_gate.shape[2]
    f = w_gate.shape[3]
    kc = nc = MOE_CHUNK
    nk = d // kc
    nn = d // nc

    def used(t, nu, a, b):
        return jnp.where(t < nu[0], a, b)

    def k_idx(t, ph, nu):
        return used(t, nu, jnp.minimum(ph, nk - 1), nk - 1)

    def n_idx(t, ph, nu):
        return used(t, nu, jnp.maximum(ph - nk, 0), nn - 1)

    grid_spec = pltpu.PrefetchScalarGridSpec(
        num_scalar_prefetch=4, grid=(n_tiles, nk + nn),
        in_specs=[pl.BlockSpec(memory_space=pl.ANY),
                  pl.BlockSpec((1, 1, kc, f), lambda t, ph, te, ti, nu, src: (layer, te[t], k_idx(t, ph, nu), 0)),
                  pl.BlockSpec((1, 1, kc, f), lambda t, ph, te, ti, nu, src: (layer, te[t], k_idx(t, ph, nu), 0)),
                  pl.BlockSpec((1, 1, f, nc), lambda t, ph, te, ti, nu, src: (layer, te[t], 0, n_idx(t, ph, nu)))],
        out_specs=pl.BlockSpec((tm, nc), lambda t, ph, te, ti, nu, src: (ti[t], n_idx(t, ph, nu))),
        scratch_shapes=[pltpu.VMEM((2, tm, d), F32), pltpu.VMEM((nk, tm, kc), BF16),
                        pltpu.VMEM((tm, f), F32), pltpu.VMEM((tm, f), F32), pltpu.VMEM((tm, f), BF16),
                        pltpu.SemaphoreType.DMA((2,))])
    return pl.pallas_call(
        _moe_kernel,
        out_shape=jax.ShapeDtypeStruct((n_tiles * tm, d), F32),
        grid_spec=grid_spec,
        compiler_params=_params(("arbitrary", "arbitrary"), 48),
        name="moe_experts",
    )(tile_expert, tile_index, n_used, src, h, w_gate, w_up, w_down)


def _combine_kernel(alpha, row0, n_tok, pos_ref, h_ref, w_ref, g_ref, b_ref, ys_ref, o_ref, f_s, sem):
    i = pl.program_id(0)
    tm, d = o_ref.shape

    def row_copy(tile, s, r):
        slot = tile % 2
        return pltpu.make_async_copy(ys_ref.at[pl.ds(pos_ref[s * n_tok + row0 + tile * tm + r], 1), :],
                                     f_s.at[slot, s, pl.ds(r, 1), :], sem.at[slot])

    def for_rows(fn):
        def body(r, carry):
            fn(0, r)
            fn(1, r)
            return carry
        lax.fori_loop(0, tm, body, 0, unroll=GATHER_UNROLL)

    @pl.when(i == 0)
    def _():
        for_rows(lambda s, r: row_copy(i, s, r).start(priority=GATHER_PRIORITY))

    @pl.when(i + 1 < pl.num_programs(0))
    def _():
        for_rows(lambda s, r: row_copy(i + 1, s, r).start(priority=GATHER_PRIORITY))

    for_rows(lambda s, r: row_copy(i, s, r).wait())
    w = w_ref[...]
    v = alpha * h_ref[...] + w[:, 0:1] * f_s[i % 2, 0] + w[:, 1:2] * f_s[i % 2, 1]
    mu = jnp.mean(v, axis=1, keepdims=True)
    vc = v - mu
    var = jnp.sum(vc * vc, axis=1, keepdims=True) * (1.0 / d)
    o_ref[...] = vc * lax.rsqrt(var + LN_EPS) * g_ref[...] + b_ref[...]


def _combine(h, ys, pos, wts, ln_g, ln_b, alpha, row0, nrows):
    n, d = h.shape
    tm = next(c for c in (128, 64, 32, 16, 8) if nrows % c == 0 and row0 % c == 0)
    o0 = row0 // tm
    rows = pl.BlockSpec((1, d), lambda i, pos: (0, 0))
    grid_spec = pltpu.PrefetchScalarGridSpec(
        num_scalar_prefetch=1, grid=(nrows // tm,),
        in_specs=[pl.BlockSpec((tm, d), lambda i, pos: (i + o0, 0)),
                  pl.BlockSpec((tm, LANES), lambda i, pos: (i + o0, 0)), rows, rows,
                  pl.BlockSpec(memory_space=pl.ANY)],
        out_specs=pl.BlockSpec((tm, d), lambda i, pos: (i, 0)),
        scratch_shapes=[pltpu.VMEM((2, 2, tm, d), F32), pltpu.SemaphoreType.DMA((2,))])
    return pl.pallas_call(
        functools.partial(_combine_kernel, alpha, row0, n),
        out_shape=jax.ShapeDtypeStruct((nrows, d), F32),
        grid_spec=grid_spec,
        compiler_params=_params(("arbitrary",), 32),
        name="combine_ln",
    )(pos, h, wts, ln_g, ln_b, ys)


def _routing_tables(ids, n_tiles):
    n = ids.shape[0]
    tm = MOE_TILE
    e = ids[:, :2].reshape(-1)
    onehot = (e[:, None] == jnp.arange(N_EXPERTS, dtype=I32)[None, :]).astype(I32)
    csum = jnp.cumsum(onehot, axis=0)
    rank = jnp.sum((csum - onehot) * onehot, axis=1)
    counts = csum[-1]
    tiles = (counts + tm - 1) // tm
    tile_end = jnp.cumsum(tiles)
    n_used = tile_end[-1]
    pos = ((tile_end - tiles)[e] * tm + rank).astype(I32)
    src = (jnp.arange(n_tiles * tm, dtype=I32) % n).at[pos].set(jnp.arange(2 * n, dtype=I32) // 2)
    t = jnp.minimum(jnp.arange(n_tiles, dtype=I32), n_used - 1)
    tile_expert = jnp.minimum(jnp.sum((tile_end[None, :] <= t[:, None]).astype(I32), axis=1), N_EXPERTS - 1)
    pos2 = pos.reshape(n, 2)
    back = jnp.concatenate([pos2[:, 0], pos2[:, 1]])
    return src, back, tile_expert, t, n_used.reshape(1).astype(I32)


def _layer(l, x_p, x_s, bsz, seq, cache_k, cache_v, state_c, state_n, state_m, page_table,
           w_in, b_gates, norm_g, w_out, ln1_g, ln1_b, w_group, b_group, w_router, b_router,
           w_e_gate, w_e_up, w_e_down, ln2_g, ln2_b, alpha):
    n_p, d = x_p.shape
    n_s = x_s.shape[0]
    n = n_p + n_s

    wt = jnp.transpose(w_in[l]).astype(BF16)
    wg = jnp.pad(w_in[l, :, COL_G:], ((0, 0), (0, LANES - 2 * M_HEADS)))
    bg = jnp.pad(b_gates[l], (0, LANES - 2 * M_HEADS)).reshape(1, LANES)
    z_p = _in_proj(x_p.astype(BF16), wt, COL_G)
    z_s = _in_proj(x_s.astype(BF16), wt, COL_G)
    g_p = _gates(x_p, wg, bg)
    g_s = _gates(x_s, wg, bg)

    ng = norm_g[l].reshape(1, M_V_W)
    att_p = _moba_prompt(z_p, bsz, seq)
    mix_p, c_p, n_pr, m_p = _mlstm_prompt(z_p, g_p, ng, bsz, seq)
    att_s = _moba_sample(z_s, cache_k, cache_v, page_table, l)
    mix_s, c_s, n_sm, m_s = _mlstm_sample(z_s, g_s, state_c, state_n, state_m, ng, l)

    wr = jnp.pad(jnp.concatenate([w_group[l], w_router[l]], axis=1),
                 ((0, 0), (0, LANES - N_GROUPS - N_EXPERTS))).astype(BF16)
    br = jnp.pad(jnp.concatenate([b_group[l], b_router[l]]), (0, LANES - N_GROUPS - N_EXPERTS)).reshape(1, LANES)
    wob = w_out[l].astype(BF16)
    g1, b1 = ln1_g[l].reshape(1, d), ln1_b[l].reshape(1, d)
    done = _merge(att_p, mix_p, wob, x_p, g1, b1, wr, br, alpha, n, 0)
    h2, ids, wts = _merge(att_s, mix_s.reshape(n_s, M_V_W), wob, x_s, g1, b1, wr, br, alpha, n, n_p, prev=done)

    n_tiles = (2 * n) // MOE_TILE + N_EXPERTS + 1
    src, back, tile_expert, tile_index, n_used = _routing_tables(ids, n_tiles)
    ys = _moe_experts(h2, src, w_e_gate, w_e_up, w_e_down, tile_expert, tile_index, n_used, l)
    g2, b2 = ln2_g[l].reshape(1, d), ln2_b[l].reshape(1, d)
    y_p = _combine(h2, ys, back, wts, g2, b2, alpha, 0, n_p)
    y_s = _combine(h2, ys, back, wts, g2, b2, alpha, n_p, n_s)

    kv = lambda rows, col: rows[:, col:col + ATT_KV_W]
    return (y_p, y_s,
            kv(z_p, COL_K).reshape(bsz, seq, ATT_KV_HEADS, HEAD_DIM),
            kv(z_p, COL_V).reshape(bsz, seq, ATT_KV_HEADS, HEAD_DIM),
            kv(z_s, COL_K).reshape(n_s, 1, ATT_KV_HEADS, HEAD_DIM),
            kv(z_s, COL_V).reshape(n_s, 1, ATT_KV_HEADS, HEAD_DIM),
            c_p, n_pr.reshape(bsz, M_HEADS, M_QK), m_p[:, :, 0, 0],
            c_s, n_sm.reshape(n_s, M_HEADS, M_QK), m_s[:, 0, :M_HEADS])


def kernel(x_prompt, x_sample, cache_k, cache_v, state_mlstm_c, state_mlstm_n, state_mlstm_m, page_table,
           w_in, b_gates, mlstm_norm_g, w_out, ln1_g, ln1_b, w_group, b_group, w_router, b_router,
           w_e_gate, w_e_up, w_e_down, ln2_g, ln2_b):
    bsz, seq, d = x_prompt.shape
    n_s, dec_seq, _ = x_sample.shape
    depth = w_in.shape[0]
    assert dec_seq == 1 and seq % MOBA_BLOCK == 0 and w_in.shape[2] == COL_G + 2 * M_HEADS
    assert (page_table.shape[1] * cache_k.shape[2]) % MOBA_BLOCK == 0 and MOBA_BLOCK % cache_k.shape[2] == 0
    alpha = (2 * depth) ** 0.25
    x_p = x_prompt.reshape(bsz * seq, d)
    x_s = x_sample.reshape(n_s, d)
    per_layer = []
    for l in range(depth):
        outs = _layer(l, x_p, x_s, bsz, seq, cache_k, cache_v, state_mlstm_c, state_mlstm_n, state_mlstm_m,
                      page_table, w_in, b_gates, mlstm_norm_g, w_out, ln1_g, ln1_b, w_group, b_group, w_router,
                      b_router, w_e_gate, w_e_up, w_e_down, ln2_g, ln2_b, alpha)
        x_p, x_s = outs[0], outs[1]
        per_layer.append(outs[2:])
    stacked = [jnp.stack([p[i] for p in per_layer]) for i in range(10)]
    return (x_p.reshape(bsz, seq, d), x_s.reshape(n_s, 1, d), *stacked)
```

```python
import functools

import jax
import jax.numpy as jnp
from jax import lax
from jax.experimental import pallas as pl
from jax.experimental.pallas import tpu as pltpu

F32, BF16, I32 = jnp.float32, jnp.bfloat16, jnp.int32

SUBLANES = 8
LANES = 128
MIB = 1024 * 1024

HEAD_DIM = 128
ATT_HEADS = 16
ATT_KV_HEADS = 8
ATT_GROUP = ATT_HEADS // ATT_KV_HEADS
ATT_SCALE = HEAD_DIM ** -0.5
LOG2_E = 1.4426950408889634
MOBA_BLOCK = 256
MOBA_TOPK = 3
M_HEADS = 4
M_QK = 256
M_V = 512
M_Q_SCALE = M_QK ** -0.5
N_GROUPS = 4
EXPERTS_PER_GROUP = 8
N_EXPERTS = N_GROUPS * EXPERTS_PER_GROUP
LN_EPS = 1e-5
NORM_EPS = 1e-6

ATT_Q_W = ATT_HEADS * HEAD_DIM
ATT_KV_W = ATT_KV_HEADS * HEAD_DIM
M_QK_W = M_HEADS * M_QK
M_V_W = M_HEADS * M_V
COL_Q = 0
COL_K = COL_Q + ATT_Q_W
COL_V = COL_K + ATT_KV_W
COL_QM = COL_V + ATT_KV_W
COL_KM = COL_QM + M_QK_W
COL_VM = COL_KM + M_QK_W
COL_OM = COL_VM + M_V_W
COL_G = COL_OM + M_V_W

MOE_TILE = 512
MOE_CHUNK = 1024
GATHER_PRIORITY = 1


def _params(sem, vmem_mib):
    return pltpu.CompilerParams(dimension_semantics=sem, vmem_limit_bytes=vmem_mib * MIB)


def _dot(a, b):
    return jnp.dot(a, b, preferred_element_type=F32)


def _dot_nt(a, b):
    return lax.dot_general(a, b, (((1,), (1,)), ((), ())), preferred_element_type=F32)


def _log_sigmoid(x):
    return jnp.minimum(x, 0.0) - jnp.log1p(jnp.exp(-jnp.abs(x)))


def _tile(n, candidates):
    for c in candidates:
        if n % c == 0:
            return c
    raise ValueError(f"no tile for {n} in {candidates}")


def _matmul_nt_kernel(x_ref, w_ref, o_ref):
    o_ref[...] = _dot_nt(x_ref[...], w_ref[...])


def _in_proj(xb, wt, w):
    n, d = xb.shape
    tm = _tile(n, (1024, 832, 640, 512, 256, 128, 64))
    tn = _tile(w, (1024, 512, 256, 128))
    return pl.pallas_call(
        _matmul_nt_kernel,
        out_shape=jax.ShapeDtypeStruct((n, w), F32),
        grid=(w // tn, n // tm),
        in_specs=[pl.BlockSpec((tm, d), lambda j, i: (i, 0)),
                  pl.BlockSpec((tn, d), lambda j, i: (j, 0))],
        out_specs=pl.BlockSpec((tm, tn), lambda j, i: (i, j)),
        compiler_params=_params(("parallel", "parallel"), 48),
        name="in_proj",
    )(xb, wt)


def _gates_kernel(x_ref, w_ref, b_ref, o_ref):
    x = x_ref[...]
    w = w_ref[...]
    xh = x.astype(BF16)
    xl = (x - xh.astype(F32)).astype(BF16)
    wh = w.astype(BF16)
    wl = (w - wh.astype(F32)).astype(BF16)
    o_ref[...] = _dot(xh, wh) + _dot(xl, wh) + _dot(xh, wl) + b_ref[...]


def _gates(x, wg, bg):
    n, d = x.shape
    tm = _tile(n, (512, 416, 320, 256, 128, 64))
    return pl.pallas_call(
        _gates_kernel,
        out_shape=jax.ShapeDtypeStruct((n, LANES), F32),
        grid=(n // tm,),
        in_specs=[pl.BlockSpec((tm, d), lambda i: (i, 0)),
                  pl.BlockSpec((d, LANES), lambda i: (0, 0)),
                  pl.BlockSpec((1, LANES), lambda i: (0, 0))],
        out_specs=pl.BlockSpec((tm, LANES), lambda i: (i, 0)),
        compiler_params=_params(("parallel",), 40),
        name="gates",
    )(x, wg, bg)


def _moba_prompt_kernel(q_ref, k_ref, v_ref, o_ref, kb_s, vt_s, km_s):
    i = pl.program_id(2)
    nb = kb_s.shape[0]
    blk = MOBA_BLOCK
    nq = ATT_GROUP * blk

    @pl.when(i == 0)
    def _():
        for j in range(nb):
            kj = k_ref[j * blk:(j + 1) * blk, :]
            kb_s[j] = kj.astype(BF16)
            km_s[j:j + 1, :] = jnp.mean(kj, axis=0, keepdims=True)
            vt_s[j] = v_ref[j * blk:(j + 1) * blk, :].T.astype(BF16)
        km_s[nb:, :] = jnp.zeros((km_s.shape[0] - nb, HEAD_DIM), F32)

    q = q_ref[...]
    q2 = jnp.concatenate([q[:, g * HEAD_DIM:(g + 1) * HEAD_DIM] for g in range(ATT_GROUP)], axis=0).astype(BF16)

    gate = _dot_nt(km_s[...].astype(BF16), q2)
    jj = lax.broadcasted_iota(I32, gate.shape, 0)
    g = jnp.where(jj < i, gate, -jnp.inf)
    sel = jnp.zeros(gate.shape, F32)
    for _ in range(MOBA_TOPK):
        mx = jnp.max(g, axis=0, keepdims=True)
        first = jnp.min(jnp.where(g == mx, jj, 2 * nb), axis=0, keepdims=True)
        pick = (jj == first) & (mx > -jnp.inf)
        sel = jnp.where(pick, 1.0, sel)
        g = jnp.where(pick, -jnp.inf, g)
    ri = lax.broadcasted_iota(I32, (blk, nq), 0)
    ci = lax.broadcasted_iota(I32, (blk, nq), 1)
    causal = ri <= (ci & (blk - 1))
    c2 = ATT_SCALE * LOG2_E

    def attend_all(own):
        m = l = acc = None
        for j in (own,) + tuple(range(own)):
            s = _dot_nt(kb_s[j], q2)
            s = jnp.where(causal if j == own else sel[j:j + 1, :] > 0.5, s, -jnp.inf)
            m_blk = jnp.max(s, axis=0, keepdims=True)
            m_new = m_blk if m is None else jnp.maximum(m, m_blk)
            p = jnp.exp2((s - m_new) * c2)
            pv = _dot(vt_s[j], p.astype(BF16))
            if m is None:
                l, acc = jnp.sum(p, axis=0, keepdims=True), pv
            else:
                alpha = jnp.exp2((m - m_new) * c2)
                l = alpha * l + jnp.sum(p, axis=0, keepdims=True)
                acc = alpha * acc + pv
            m = m_new
        out = (acc / l).T
        for g_i in range(ATT_GROUP):
            o_ref[:, g_i * HEAD_DIM:(g_i + 1) * HEAD_DIM] = out[g_i * blk:(g_i + 1) * blk, :].astype(o_ref.dtype)

    for own in range(nb):
        pl.when(i == own)(functools.partial(attend_all, own))


def _moba_prompt(z, bsz, seq):
    nb = seq // MOBA_BLOCK
    qw = ATT_GROUP * HEAD_DIM
    return pl.pallas_call(
        _moba_prompt_kernel,
        out_shape=jax.ShapeDtypeStruct((bsz * seq, ATT_Q_W), BF16),
        grid=(bsz, ATT_KV_HEADS, nb),
        in_specs=[pl.BlockSpec((MOBA_BLOCK, qw), lambda b, h, i: (b * nb + i, COL_Q // qw + h)),
                  pl.BlockSpec((seq, HEAD_DIM), lambda b, h, i: (b, COL_K // HEAD_DIM + h)),
                  pl.BlockSpec((seq, HEAD_DIM), lambda b, h, i: (b, COL_V // HEAD_DIM + h))],
        out_specs=pl.BlockSpec((MOBA_BLOCK, qw), lambda b, h, i: (b * nb + i, h)),
        scratch_shapes=[pltpu.VMEM((nb, MOBA_BLOCK, HEAD_DIM), BF16),
                        pltpu.VMEM((nb, HEAD_DIM, MOBA_BLOCK), BF16),
                        pltpu.VMEM((2 * SUBLANES, HEAD_DIM), F32)],
        compiler_params=_params(("parallel", "parallel", "arbitrary"), 32),
        name="moba_prompt",
    )(z, z, z)


def _moba_sample_kernel(n_pages, pt_ref, q_ref, kn_ref, vn_ref, *rest):
    del pt_ref
    kp = rest[:n_pages]
    vp = rest[n_pages:2 * n_pages]
    o_ref = rest[2 * n_pages]
    page = kp[0].shape[2]
    per_blk = MOBA_BLOCK // page
    nb = n_pages // per_blk
    nrow = page * ATT_KV_HEADS
    both = lambda a: jnp.concatenate([a] * ATT_GROUP, axis=0)

    q = q_ref[0]
    qb = q.astype(BF16)
    ri = lax.broadcasted_iota(I32, (ATT_HEADS, nrow), 0)
    ci = lax.broadcasted_iota(I32, (ATT_HEADS, nrow), 1)
    own = (ci % ATT_KV_HEADS) == (ri % ATT_KV_HEADS)

    scores, ksum = [], []
    for p in range(n_pages):
        k3 = kp[p][0, 0]
        scores.append(_dot_nt(qb, k3.reshape(nrow, HEAD_DIM).astype(BF16)) * ATT_SCALE)
        ksum.append(jnp.sum(k3, axis=0))
    gates = []
    for j in range(nb):
        kmean = sum(ksum[j * per_blk:(j + 1) * per_blk]) * (1.0 / MOBA_BLOCK)
        gates.append(jnp.sum(q * both(kmean), axis=1, keepdims=True))
    sel = []
    for j in range(nb):
        rank = jnp.zeros(gates[j].shape, F32)
        for jp in range(nb):
            if jp < j:
                rank = rank + (gates[jp] >= gates[j]).astype(F32)
            elif jp > j:
                rank = rank + (gates[jp] > gates[j]).astype(F32)
        sel.append(rank < MOBA_TOPK)

    s_new = jnp.sum(q * both(kn_ref[0]), axis=1, keepdims=True) * ATT_SCALE
    m = s_new
    for p in range(n_pages):
        scores[p] = jnp.where(sel[p // per_blk], jnp.where(own, scores[p], -jnp.inf), -jnp.inf)
        m = jnp.maximum(m, jnp.max(scores[p], axis=1, keepdims=True))
    p_new = jnp.exp(s_new - m)
    l = p_new
    acc = p_new * both(vn_ref[0])
    for p in range(n_pages):
        pp = jnp.exp(scores[p] - m)
        l = l + jnp.sum(pp, axis=1, keepdims=True)
        acc = acc + _dot(pp.astype(BF16), vp[p][0, 0].reshape(nrow, HEAD_DIM).astype(BF16))
    o_ref[0] = (acc / l).astype(o_ref.dtype)


def _moba_sample(zs, cache_k, cache_v, page_table, layer):
    ns = zs.shape[0]
    n_pages = page_table.shape[1]
    page = cache_k.shape[2]
    pt_flat = page_table.reshape(-1).astype(I32)
    qg = jnp.transpose(zs[:, COL_Q:COL_Q + ATT_Q_W].reshape(ns, ATT_KV_HEADS, ATT_GROUP, HEAD_DIM), (0, 2, 1, 3))
    kv_new = lambda col: zs[:, col:col + ATT_KV_W].reshape(ns, ATT_KV_HEADS, HEAD_DIM)

    def page_spec(p):
        return pl.BlockSpec((1, 1, page, ATT_KV_HEADS, HEAD_DIM),
                            lambda s, pt: (layer, pt[s * n_pages + p], 0, 0, 0))

    rows = lambda r: pl.BlockSpec((1, r, HEAD_DIM), lambda s, pt: (s, 0, 0))
    grid_spec = pltpu.PrefetchScalarGridSpec(
        num_scalar_prefetch=1, grid=(ns,),
        in_specs=[rows(ATT_HEADS), rows(ATT_KV_HEADS), rows(ATT_KV_HEADS)] + [page_spec(p) for p in range(n_pages)] * 2,
        out_specs=rows(ATT_HEADS))
    out = pl.pallas_call(
        functools.partial(_moba_sample_kernel, n_pages),
        out_shape=jax.ShapeDtypeStruct((ns, ATT_HEADS, HEAD_DIM), BF16),
        grid_spec=grid_spec,
        compiler_params=_params(("parallel",), 48),
        name="moba_sample",
    )(pt_flat, qg.reshape(ns, ATT_HEADS, HEAD_DIM), kv_new(COL_K), kv_new(COL_V),
      *([cache_k] * n_pages), *([cache_v] * n_pages))
    return jnp.transpose(out.reshape(ns, ATT_GROUP, ATT_KV_HEADS, HEAD_DIM), (0, 2, 1, 3)).reshape(ns, ATT_Q_W)


def _mlstm_prompt_kernel(q_ref, k_ref, v_ref, og_ref, igr_ref, fgr_ref, igc_ref, fgc_ref, ng_ref,
                         y_ref, c_ref, n_ref, m_ref):
    @pl.when(pl.program_id(2) == 0)
    def _():
        c_ref[...] = jnp.zeros(c_ref.shape, F32)
        n_ref[...] = jnp.zeros(n_ref.shape, F32)
        m_ref[...] = jnp.zeros(m_ref.shape, F32)

    L = q_ref.shape[0]
    q = q_ref[...] * M_Q_SCALE
    k = k_ref[...]
    ig_r = igr_ref[0, 0]
    lf_r = _log_sigmoid(fgr_ref[0, 0])
    ig_c = igc_ref[0, 0]
    lf_c = _log_sigmoid(fgc_ref[0, 0])
    ri = lax.broadcasted_iota(I32, (L, L), 0)
    ci = lax.broadcasted_iota(I32, (L, L), 1)
    tril = ci <= ri
    b_c = jnp.sum(jnp.where(tril, lf_r, 0.0), axis=1, keepdims=True)
    b_r = jnp.sum(jnp.where(ri <= ci, lf_c, 0.0), axis=0, keepdims=True)
    r_r = ig_r - b_r
    m0 = m_ref[0, 0][:, 0:1]
    m_c = b_c + jnp.maximum(m0, jnp.max(jnp.where(tril, r_r, -jnp.inf), axis=1, keepdims=True))
    dmat = jnp.exp(jnp.where(tril, b_c - m_c + r_r, -jnp.inf))
    qb = q.astype(BF16)
    vb = v_ref[...].astype(BF16)
    s = _dot_nt(qb, k.astype(BF16)) * dmat
    inter = jnp.exp(b_c + m0 - m_c)
    c0 = c_ref[0, 0]
    n0 = n_ref[0, 0]
    num = inter * _dot(qb, c0.astype(BF16)) + _dot(s.astype(BF16), vb)
    den = inter * jnp.sum(q * n0, axis=1, keepdims=True) + jnp.sum(s, axis=1, keepdims=True)
    h = num / jnp.maximum(jnp.abs(den), jnp.exp(-m_c))
    hn = h * lax.rsqrt(jnp.mean(h * h, axis=1, keepdims=True) + NORM_EPS)
    y_ref[...] = (hn * ng_ref[...] * jax.nn.sigmoid(og_ref[...])).astype(y_ref.dtype)

    b_last = b_c[L - 1:L, :]
    m_last = m_c[L - 1:L, :]
    kw = k * jnp.exp(b_last - b_c + ig_c - m_last)
    decay = jnp.exp(b_last + m0 - m_last)
    c_ref[0, 0] = decay * c0 + _dot(kw.T.astype(BF16), vb)
    n_ref[0, 0] = decay * n0 + jnp.sum(kw, axis=0, keepdims=True)
    m_ref[0, 0] = jnp.broadcast_to(m_last, (1, LANES))


def _mlstm_prompt(z, g_all, norm_g, bsz, seq):
    L = _tile(seq, (256, 128))
    nc = seq // L
    gp = jnp.transpose(g_all[:, :2 * M_HEADS].reshape(bsz, seq, 2 * M_HEADS), (0, 2, 1))
    g_row = gp.reshape(bsz, 2 * M_HEADS, 1, seq)
    g_col = gp.reshape(bsz, 2 * M_HEADS, seq, 1)
    row = lambda off: pl.BlockSpec((1, 1, 1, L), lambda b, h, c: (b, off + h, 0, c))
    col = lambda off: pl.BlockSpec((1, 1, L, 1), lambda b, h, c: (b, off + h, c, 0))
    state = lambda shape: pl.BlockSpec((1, 1) + shape, lambda b, h, c: (b, h, 0, 0))
    return pl.pallas_call(
        _mlstm_prompt_kernel,
        out_shape=(jax.ShapeDtypeStruct((bsz * seq, M_V_W), BF16),
                   jax.ShapeDtypeStruct((bsz, M_HEADS, M_QK, M_V), F32),
                   jax.ShapeDtypeStruct((bsz, M_HEADS, 1, M_QK), F32),
                   jax.ShapeDtypeStruct((bsz, M_HEADS, 1, LANES), F32)),
        grid=(bsz, M_HEADS, nc),
        in_specs=[pl.BlockSpec((L, M_QK), lambda b, h, c: (b * nc + c, COL_QM // M_QK + h)),
                  pl.BlockSpec((L, M_QK), lambda b, h, c: (b * nc + c, COL_KM // M_QK + h)),
                  pl.BlockSpec((L, M_V), lambda b, h, c: (b * nc + c, COL_VM // M_V + h)),
                  pl.BlockSpec((L, M_V), lambda b, h, c: (b * nc + c, COL_OM // M_V + h)),
                  row(0), row(M_HEADS), col(0), col(M_HEADS),
                  pl.BlockSpec((1, M_V), lambda b, h, c: (0, h))],
        out_specs=(pl.BlockSpec((L, M_V), lambda b, h, c: (b * nc + c, h)),
                   state((M_QK, M_V)), state((1, M_QK)), state((1, LANES))),
        compiler_params=_params(("parallel", "parallel", "arbitrary"), 32),
        name="mlstm_prompt",
    )(z, z, z, z, g_row, g_row, g_col, g_col, norm_g)


def _mlstm_sample_kernel(qr_ref, kr_ref, vr_ref, or_ref, qc_ref, kc_ref, g_ref, m0_ref, n0_ref, c0_ref, ng_ref,
                         y_ref, c_ref, n_ref, m_ref):
    g = g_ref[0]
    m0_all = m0_ref[0]
    lane = lax.broadcasted_iota(I32, (1, LANES), 1)
    m_out = jnp.zeros((1, LANES), F32)
    for h in range(M_HEADS):
        ig = g[:, h:h + 1]
        lf = _log_sigmoid(g[:, M_HEADS + h:M_HEADS + h + 1])
        m0 = m0_all[:, h:h + 1]
        m = jnp.maximum(lf + m0, ig)
        w = jnp.exp(ig - m)
        decay = jnp.exp(lf + m0 - m)
        q_r = qr_ref[0, h] * M_Q_SCALE
        k_r = kr_ref[0, h]
        v_r = vr_ref[0, h]
        q_c = qc_ref[0, h] * M_Q_SCALE
        c0 = c0_ref[0, 0, h]
        n0 = n0_ref[0, 0, h]
        s = jnp.sum(q_r * k_r, axis=1, keepdims=True) * w
        num = decay * jnp.sum(q_c * c0, axis=0, keepdims=True) + s * v_r
        den = decay * jnp.sum(q_r * n0, axis=1, keepdims=True) + s
        hh = num / jnp.maximum(jnp.abs(den), jnp.exp(-m))
        hn = hh * lax.rsqrt(jnp.mean(hh * hh, axis=1, keepdims=True) + NORM_EPS)
        y = hn * ng_ref[:, h * M_V:(h + 1) * M_V] * jax.nn.sigmoid(or_ref[0, h])
        y_ref[0, :, h * M_V:(h + 1) * M_V] = y.astype(y_ref.dtype)
        c_ref[0, h] = decay * c0 + (w * kc_ref[0, h]) * v_r
        n_ref[0, h] = decay * n0 + w * k_r
        m_out = jnp.where(lane == h, m, m_out)
    m_ref[0] = m_out


def _mlstm_sample(zs, gs, state_c, state_n, state_m, norm_g, layer):
    ns = zs.shape[0]
    qm = zs[:, COL_QM:COL_QM + M_QK_W]
    km = zs[:, COL_KM:COL_KM + M_QK_W]
    rows = lambda a, w: a.reshape(ns, M_HEADS, 1, w)
    m0 = jnp.pad(state_m[layer], ((0, 0), (0, LANES - M_HEADS))).reshape(ns, 1, LANES)
    row_spec = lambda w: pl.BlockSpec((1, M_HEADS, 1, w), lambda s: (s, 0, 0, 0))
    lane_spec = pl.BlockSpec((1, 1, LANES), lambda s: (s, 0, 0))
    col_spec = pl.BlockSpec((1, M_HEADS, M_QK, 1), lambda s: (s, 0, 0, 0))
    return pl.pallas_call(
        _mlstm_sample_kernel,
        out_shape=(jax.ShapeDtypeStruct((ns, 1, M_V_W), BF16),
                   jax.ShapeDtypeStruct((ns, M_HEADS, M_QK, M_V), F32),
                   jax.ShapeDtypeStruct((ns, M_HEADS, 1, M_QK), F32),
                   jax.ShapeDtypeStruct((ns, 1, LANES), F32)),
        grid=(ns,),
        in_specs=[row_spec(M_QK), row_spec(M_QK), row_spec(M_V), row_spec(M_V), col_spec, col_spec,
                  lane_spec, lane_spec,
                  pl.BlockSpec((1, 1, M_HEADS, 1, M_QK), lambda s: (layer, s, 0, 0, 0)),
                  pl.BlockSpec((1, 1, M_HEADS, M_QK, M_V), lambda s: (layer, s, 0, 0, 0)),
                  pl.BlockSpec((1, M_V_W), lambda s: (0, 0))],
        out_specs=(pl.BlockSpec((1, 1, M_V_W), lambda s: (s, 0, 0)),
                   pl.BlockSpec((1, M_HEADS, M_QK, M_V), lambda s: (s, 0, 0, 0)),
                   row_spec(M_QK), lane_spec),
        compiler_params=_params(("parallel",), 32),
        name="mlstm_sample",
    )(rows(qm, M_QK), rows(km, M_QK), rows(zs[:, COL_VM:COL_VM + M_V_W], M_V), rows(zs[:, COL_OM:COL_OM + M_V_W], M_V),
      qm.reshape(ns, M_HEADS, M_QK, 1), km.reshape(ns, M_HEADS, M_QK, 1),
      gs.reshape(ns, 1, LANES), m0,
      state_n.reshape(state_n.shape[0], ns, M_HEADS, 1, M_QK), state_c, norm_g)


def _layernorm_chunks(chunks, d):
    total = sum(jnp.sum(c, axis=1, keepdims=True) for c in chunks)
    mu = total * (1.0 / d)
    var = sum(jnp.sum(jnp.square(c - mu), axis=1, keepdims=True) for c in chunks) * (1.0 / d)
    return mu, lax.rsqrt(var + LN_EPS)


def _merge_kernel(alpha, *refs):
    att_ref, mix_ref, wa_ref, wb_ref, x_ref, g_ref, b_ref, wr_ref, br_ref = refs[:9]
    h_ref, ids_ref, wts_ref = refs[-3:]
    j = pl.program_id(1)
    tn = x_ref.shape[1]
    d = h_ref.shape[1]
    nj = d // tn
    cols = lambda c: slice(c * tn, (c + 1) * tn)
    pre = alpha * x_ref[...] + _dot(att_ref[...], wa_ref[...]) + _dot(mix_ref[...], wb_ref[...])
    for c in range(nj):
        @pl.when(j == c)
        def _(c=c):
            h_ref[:, cols(c)] = pre

    @pl.when(j == nj - 1)
    def _():
        mu, rstd = _layernorm_chunks([h_ref[:, cols(c)] for c in range(nj)], d)
        lg = br_ref[...]
        for c in range(nj):
            y = (h_ref[:, cols(c)] - mu) * rstd * g_ref[:, cols(c)] + b_ref[:, cols(c)]
            h_ref[:, cols(c)] = y
            lg = lg + _dot(y.astype(BF16), wr_ref[cols(c), :])
        lane = lax.broadcasted_iota(I32, lg.shape, 1)
        lane_f = lane.astype(F32)
        far = float(LANES)
        gl = jnp.where(lane < N_GROUPS, lg, -jnp.inf)
        gmax = jnp.max(gl, axis=1, keepdims=True)
        gsel = jnp.min(jnp.where(gl == gmax, lane_f, far), axis=1, keepdims=True).astype(I32)
        gw = 1.0 / jnp.sum(jnp.where(lane < N_GROUPS, jnp.exp(gl - gmax), 0.0), axis=1, keepdims=True)
        e_lane = lane - N_GROUPS
        in_grp = (e_lane >= 0) & (e_lane < N_EXPERTS) & ((e_lane // EXPERTS_PER_GROUP) == gsel)
        el = jnp.where(in_grp, lg, -jnp.inf)
        t1 = jnp.max(el, axis=1, keepdims=True)
        i1 = jnp.min(jnp.where(el == t1, lane_f, far), axis=1, keepdims=True)
        el2 = jnp.where(lane_f == i1, -jnp.inf, el)
        t2 = jnp.max(el2, axis=1, keepdims=True)
        i2 = jnp.min(jnp.where(el2 == t2, lane_f, far), axis=1, keepdims=True)
        e21 = jnp.exp(t2 - t1)
        w1 = gw / (1.0 + e21)
        w2 = w1 * e21
        ids = jnp.where(lane == 0, i1, jnp.where(lane == 1, i2, float(N_GROUPS))) - float(N_GROUPS)
        ids_ref[...] = ids.astype(I32)
        wts_ref[...] = jnp.where(lane == 0, w1, jnp.where(lane == 1, w2, 0.0))


def _merge(att, mix, wob, x, ln_g, ln_b, wr, br, alpha, n_total, row0, prev=None):
    r, d = x.shape
    half = att.shape[1]
    tm = _tile(r, (512, 256, 128, 64))
    assert row0 % tm == 0
    o0 = row0 // tm
    tn = _tile(d, (1024, 512))
    rows = lambda w: pl.BlockSpec((1, w), lambda i, j: (0, 0))
    prev = () if prev is None else tuple(prev)
    n_in = 9
    return pl.pallas_call(
        functools.partial(_merge_kernel, alpha),
        out_shape=(jax.ShapeDtypeStruct((n_total, d), F32),
                   jax.ShapeDtypeStruct((n_total, LANES), I32),
                   jax.ShapeDtypeStruct((n_total, LANES), F32)),
        grid=(r // tm, d // tn),
        in_specs=[pl.BlockSpec((tm, half), lambda i, j: (i, 0)),
                  pl.BlockSpec((tm, half), lambda i, j: (i, 0)),
                  pl.BlockSpec((half, tn), lambda i, j: (0, j)),
                  pl.BlockSpec((half, tn), lambda i, j: (1, j)),
                  pl.BlockSpec((tm, tn), lambda i, j: (i, j)),
                  rows(d), rows(d),
                  pl.BlockSpec((d, LANES), lambda i, j: (0, 0)),
                  rows(LANES)] + [pl.BlockSpec(memory_space=pl.ANY)] * len(prev),
        out_specs=(pl.BlockSpec((tm, d), lambda i, j: (i + o0, 0)),
                   pl.BlockSpec((tm, LANES), lambda i, j: (i + o0, 0)),
                   pl.BlockSpec((tm, LANES), lambda i, j: (i + o0, 0))),
        input_output_aliases={n_in + k: k for k in range(len(prev))},
        compiler_params=_params(("parallel", "arbitrary"), 52),
        name="merge_ln_router",
    )(att, mix, wob, wob, x, ln_g, ln_b, wr, br, *prev)


def _moe_kernel(te_ref, ti_ref, nu_ref, src_ref, h_ref, wg_ref, wu_ref, wd_ref, o_ref,
                xf_s, xb_s, hg_s, hu_s, a_s, sem):
    del te_ref, ti_ref
    t = pl.program_id(0)
    ph = pl.program_id(1)
    nk, tm, kc = xb_s.shape
    n_phase = nk + xf_s.shape[2] // o_ref.shape[1]
    share = tm // n_phase
    assert share * n_phase == tm
    nu = nu_ref[0]

    def start_rows(tile, lo, count):
        slot = tile % 2

        def body(g, carry):
            base = pl.multiple_of(lo + g * SUBLANES, SUBLANES)
            for u in range(SUBLANES):
                pltpu.make_async_copy(h_ref.at[pl.ds(src_ref[tile * tm + base + u], 1), :],
                                      xf_s.at[slot, pl.ds(base + u, 1), :],
                                      sem.at[slot]).start(priority=GATHER_PRIORITY)
            return carry
        lax.fori_loop(0, count // SUBLANES, body, 0)

    def wait_tile(tile):
        slot = tile % 2
        pltpu.make_async_copy(h_ref.at[pl.ds(0, tm), :], xf_s.at[slot], sem.at[slot]).wait()

    @pl.when(t < nu)
    def _():
        @pl.when(t + 1 < nu)
        def _():
            start_rows(t + 1, ph * share, share)

        @pl.when(ph == 0)
        def _():
            @pl.when(t == 0)
            def _():
                start_rows(t, 0, tm)

            wait_tile(t)
            for c in range(nk):
                xb_s[c] = xf_s[t % 2, :, c * kc:(c + 1) * kc].astype(BF16)

        @pl.when(ph < nk)
        def _():
            x = xb_s[ph]
            pg = _dot(x, wg_ref[0, 0].astype(BF16))
            pu = _dot(x, wu_ref[0, 0].astype(BF16))

            @pl.when(ph == 0)
            def _():
                hg_s[...] = pg
                hu_s[...] = pu

            @pl.when(ph != 0)
            def _():
                hg_s[...] += pg
                hu_s[...] += pu

        @pl.when(ph == nk - 1)
        def _():
            hg = hg_s[...]
            a_s[...] = (hg * jax.nn.sigmoid(hg) * hu_s[...]).astype(BF16)

        @pl.when(ph >= nk)
        def _():
            o_ref[...] = _dot(a_s[...], wd_ref[0, 0].astype(BF16))


def _moe_experts(h, src, w_gate, w_up, w_down, tile_expert, tile_index, n_used, layer):
    tm = MOE_TILE
    n_tiles = tile_expert.shape[0]
    d = w_gate.shape[2]
    f = w_gate.shape[3]
    kc = nc = MOE_CHUNK
    nk = d // kc
    nn = d // nc

    def used(t, nu, a, b):
        return jnp.where(t < nu[0], a, b)

    def k_idx(t, ph, nu):
        return used(t, nu, jnp.minimum(ph, nk - 1), nk - 1)

    def n_idx(t, ph, nu):
        return used(t, nu, jnp.maximum(ph - nk, 0), nn - 1)

    grid_spec = pltpu.PrefetchScalarGridSpec(
        num_scalar_prefetch=4, grid=(n_tiles, nk + nn),
        in_specs=[pl.BlockSpec(memory_space=pl.ANY),
                  pl.BlockSpec((1, 1, kc, f), lambda t, ph, te, ti, nu, src: (layer, te[t], k_idx(t, ph, nu), 0)),
                  pl.BlockSpec((1, 1, kc, f), lambda t, ph, te, ti, nu, src: (layer, te[t], k_idx(t, ph, nu), 0)),
                  pl.BlockSpec((1, 1, f, nc), lambda t, ph, te, ti, nu, src: (layer, te[t], 0, n_idx(t, ph, nu)))],
        out_specs=pl.BlockSpec((tm, nc), lambda t, ph, te, ti, nu, src: (ti[t], n_idx(t, ph, nu))),
        scratch_shapes=[pltpu.VMEM((2, tm, d), F32), pltpu.VMEM((nk, tm, kc), BF16),
                        pltpu.VMEM((tm, f), F32), pltpu.VMEM((tm, f), F32), pltpu.VMEM((tm, f), BF16),
                        pltpu.SemaphoreType.DMA((2,))])
    return pl.pallas_call(
        _moe_kernel,
        out_shape=jax.ShapeDtypeStruct((n_tiles * tm, d), F32),
        grid_spec=grid_spec,
        compiler_params=_params(("arbitrary", "arbitrary"), 48),
        name="moe_experts",
    )(tile_expert, tile_index, n_used, src, h, w_gate, w_up, w_down)


def _combine_kernel(alpha, row0, n_tok, pos_ref, h_ref, w_ref, g_ref, b_ref, ys_ref, o_ref, f_s, sem):
    i = pl.program_id(0)
    tm, d = o_ref.shape

    def start_tile(tile):
        slot = tile % 2

        def body(g, carry):
            base = pl.multiple_of(g * SUBLANES, SUBLANES)
            for u in range(SUBLANES):
                for s in range(2):
                    pltpu.make_async_copy(
                        ys_ref.at[pl.ds(pos_ref[s * n_tok + row0 + tile * tm + base + u], 1), :],
                        f_s.at[slot, s, pl.ds(base + u, 1), :], sem.at[slot]).start(priority=GATHER_PRIORITY)
            return carry
        lax.fori_loop(0, tm // SUBLANES, body, 0)

    def wait_tile(tile):
        slot = tile % 2
        for s in range(2):
            pltpu.make_async_copy(ys_ref.at[pl.ds(0, tm), :], f_s.at[slot, s], sem.at[slot]).wait()

    @pl.when(i == 0)
    def _():
        start_tile(i)

    @pl.when(i + 1 < pl.num_programs(0))
    def _():
        start_tile(i + 1)

    wait_tile(i)
    w = w_ref[...]
    v = alpha * h_ref[...] + w[:, 0:1] * f_s[i % 2, 0] + w[:, 1:2] * f_s[i % 2, 1]
    mu = jnp.mean(v, axis=1, keepdims=True)
    vc = v - mu
    var = jnp.sum(vc * vc, axis=1, keepdims=True) * (1.0 / d)
    o_ref[...] = vc * lax.rsqrt(var + LN_EPS) * g_ref[...] + b_ref[...]


def _combine(h, ys, pos, wts, ln_g, ln_b, alpha, row0, nrows):
    n, d = h.shape
    tm = next(c for c in (128, 64, 32, 16, 8) if nrows % c == 0 and row0 % c == 0)
    o0 = row0 // tm
    rows = pl.BlockSpec((1, d), lambda i, pos: (0, 0))
    grid_spec = pltpu.PrefetchScalarGridSpec(
        num_scalar_prefetch=1, grid=(nrows // tm,),
        in_specs=[pl.BlockSpec((tm, d), lambda i, pos: (i + o0, 0)),
                  pl.BlockSpec((tm, LANES), lambda i, pos: (i + o0, 0)), rows, rows,
                  pl.BlockSpec(memory_space=pl.ANY)],
        out_specs=pl.BlockSpec((tm, d), lambda i, pos: (i, 0)),
        scratch_shapes=[pltpu.VMEM((2, 2, tm, d), F32), pltpu.SemaphoreType.DMA((2,))])
    return pl.pallas_call(
        functools.partial(_combine_kernel, alpha, row0, n),
        out_shape=jax.ShapeDtypeStruct((nrows, d), F32),
        grid_spec=grid_spec,
        compiler_params=_params(("arbitrary",), 32),
        name="combine_ln",
    )(pos, h, wts, ln_g, ln_b, ys)


def _routing_tables(ids, n_tiles):
    n = ids.shape[0]
    tm = MOE_TILE
    e = ids[:, :2].reshape(-1)
    onehot = (e[:, None] == jnp.arange(N_EXPERTS, dtype=I32)[None, :]).astype(I32)
    csum = jnp.cumsum(onehot, axis=0)
    rank = jnp.sum((csum - onehot) * onehot, axis=1)
    counts = csum[-1]
    tiles = (counts + tm - 1) // tm
    tile_end = jnp.cumsum(tiles)
    n_used = tile_end[-1]
    pos = ((tile_end - tiles)[e] * tm + rank).astype(I32)
    src = (jnp.arange(n_tiles * tm, dtype=I32) % n).at[pos].set(jnp.arange(2 * n, dtype=I32) // 2)
    t = jnp.minimum(jnp.arange(n_tiles, dtype=I32), n_used - 1)
    tile_expert = jnp.minimum(jnp.sum((tile_end[None, :] <= t[:, None]).astype(I32), axis=1), N_EXPERTS - 1)
    pos2 = pos.reshape(n, 2)
    back = jnp.concatenate([pos2[:, 0], pos2[:, 1]])
    return src, back, tile_expert, t, n_used.reshape(1).astype(I32)


def _layer(l, x_p, x_s, bsz, seq, cache_k, cache_v, state_c, state_n, state_m, page_table,
           w_in, b_gates, norm_g, w_out, ln1_g, ln1_b, w_group, b_group, w_router, b_router,
           w_e_gate, w_e_up, w_e_down, ln2_g, ln2_b, alpha):
    n_p, d = x_p.shape
    n_s = x_s.shape[0]
    n = n_p + n_s

    wt = jnp.transpose(w_in[l]).astype(BF16)
    wg = jnp.pad(w_in[l, :, COL_G:], ((0, 0), (0, LANES - 2 * M_HEADS)))
    bg = jnp.pad(b_gates[l], (0, LANES - 2 * M_HEADS)).reshape(1, LANES)
    z_p = _in_proj(x_p.astype(BF16), wt, COL_G)
    z_s = _in_proj(x_s.astype(BF16), wt, COL_G)
    g_p = _gates(x_p, wg, bg)
    g_s = _gates(x_s, wg, bg)

    ng = norm_g[l].reshape(1, M_V_W)
    att_p = _moba_prompt(z_p, bsz, seq)
    mix_p, c_p, n_pr, m_p = _mlstm_prompt(z_p, g_p, ng, bsz, seq)
    att_s = _moba_sample(z_s, cache_k, cache_v, page_table, l)
    mix_s, c_s, n_sm, m_s = _mlstm_sample(z_s, g_s, state_c, state_n, state_m, ng, l)

    wr = jnp.pad(jnp.concatenate([w_group[l], w_router[l]], axis=1),
                 ((0, 0), (0, LANES - N_GROUPS - N_EXPERTS))).astype(BF16)
    br = jnp.pad(jnp.concatenate([b_group[l], b_router[l]]), (0, LANES - N_GROUPS - N_EXPERTS)).reshape(1, LANES)
    wob = w_out[l].astype(BF16)
    g1, b1 = ln1_g[l].reshape(1, d), ln1_b[l].reshape(1, d)
    done = _merge(att_p, mix_p, wob, x_p, g1, b1, wr, br, alpha, n, 0)
    h2, ids, wts = _merge(att_s, mix_s.reshape(n_s, M_V_W), wob, x_s, g1, b1, wr, br, alpha, n, n_p, prev=done)

    n_tiles = (2 * n) // MOE_TILE + N_EXPERTS + 1
    src, back, tile_expert, tile_index, n_used = _routing_tables(ids, n_tiles)
    ys = _moe_experts(h2, src, w_e_gate, w_e_up, w_e_down, tile_expert, tile_index, n_used, l)
    g2, b2 = ln2_g[l].reshape(1, d), ln2_b[l].reshape(1, d)
    y_p = _combine(h2, ys, back, wts, g2, b2, alpha, 0, n_p)
    y_s = _combine(h2, ys, back, wts, g2, b2, alpha, n_p, n_s)

    kv = lambda rows, col: rows[:, col:col + ATT_KV_W]
    return (y_p, y_s,
            kv(z_p, COL_K).reshape(bsz, seq, ATT_KV_HEADS, HEAD_DIM),
            kv(z_p, COL_V).reshape(bsz, seq, ATT_KV_HEADS, HEAD_DIM),
            kv(z_s, COL_K).reshape(n_s, 1, ATT_KV_HEADS, HEAD_DIM),
            kv(z_s, COL_V).reshape(n_s, 1, ATT_KV_HEADS, HEAD_DIM),
            c_p, n_pr.reshape(bsz, M_HEADS, M_QK), m_p[:, :, 0, 0],
            c_s, n_sm.reshape(n_s, M_HEADS, M_QK), m_s[:, 0, :M_HEADS])


def kernel(x_prompt, x_sample, cache_k, cache_v, state_mlstm_c, state_mlstm_n, state_mlstm_m, page_table,
           w_in, b_gates, mlstm_norm_g, w_out, ln1_g, ln1_b, w_group, b_group, w_router, b_router,
           w_e_gate, w_e_up, w_e_down, ln2_g, ln2_b):
    bsz, seq, d = x_prompt.shape
    n_s, dec_seq, _ = x_sample.shape
    depth = w_in.shape[0]
    assert dec_seq == 1 and seq % MOBA_BLOCK == 0 and w_in.shape[2] == COL_G + 2 * M_HEADS
    assert (page_table.shape[1] * cache_k.shape[2]) % MOBA_BLOCK == 0 and MOBA_BLOCK % cache_k.shape[2] == 0
    alpha = (2 * depth) ** 0.25
    x_p = x_prompt.reshape(bsz * seq, d)
    x_s = x_sample.reshape(n_s, d)
    per_layer = []
    for l in range(depth):
        outs = _layer(l, x_p, x_s, bsz, seq, cache_k, cache_v, state_mlstm_c, state_mlstm_n, state_mlstm_m,
                      page_table, w_in, b_gates, mlstm_norm_g, w_out, ln1_g, ln1_b, w_group, b_group, w_router,
                      b_router, w_e_gate, w_e_up, w_e_down, ln2_g, ln2_b, alpha)
        x_p, x_s = outs[0], outs[1]
        per_layer.append(outs[2:])
    stacked = [jnp.stack([p[i] for p in per_layer]) for i in range(10)]
    return (x_p.reshape(bsz, seq, d), x_s.reshape(n_s, 1, d), *stacked)
```

```python
import functools

import jax
import jax.numpy as jnp
from jax import lax
from jax.experimental import pallas as pl
from jax.experimental.pallas import tpu as pltpu

F32, BF16, I32, U32 = jnp.float32, jnp.bfloat16, jnp.int32, jnp.uint32

SUBLANES = 8
LANES = 128
MIB = 1024 * 1024

HEAD_DIM = 128
ATT_HEADS = 16
ATT_KV_HEADS = 8
ATT_GROUP = ATT_HEADS // ATT_KV_HEADS
ATT_SCALE = HEAD_DIM ** -0.5
LOG2_E = 1.4426950408889634
MOBA_BLOCK = 256
MOBA_TOPK = 3
M_HEADS = 4
M_QK = 256
M_V = 512
M_Q_SCALE = M_QK ** -0.5
N_GROUPS = 4
EXPERTS_PER_GROUP = 8
N_EXPERTS = N_GROUPS * EXPERTS_PER_GROUP
LN_EPS = 1e-5
NORM_EPS = 1e-6

ATT_Q_W = ATT_HEADS * HEAD_DIM
ATT_KV_W = ATT_KV_HEADS * HEAD_DIM
M_QK_W = M_HEADS * M_QK
M_V_W = M_HEADS * M_V
COL_Q = 0
COL_K = COL_Q + ATT_Q_W
COL_V = COL_K + ATT_KV_W
COL_QM = COL_V + ATT_KV_W
COL_KM = COL_QM + M_QK_W
COL_VM = COL_KM + M_QK_W
COL_OM = COL_VM + M_V_W
COL_G = COL_OM + M_V_W

MOE_TILE = 512
MOE_CHUNK = 1024
MERGE_CHUNK = 512
GATHER_PRIORITY = 1


def _params(sem, vmem_mib):
    return pltpu.CompilerParams(dimension_semantics=sem, vmem_limit_bytes=vmem_mib * MIB)


def _dot(a, b):
    return jnp.dot(a, b, preferred_element_type=F32)


def _dot_nt(a, b):
    return lax.dot_general(a, b, (((1,), (1,)), ((), ())), preferred_element_type=F32)


def _log_sigmoid(x):
    return jnp.minimum(x, 0.0) - jnp.log1p(jnp.exp(-jnp.abs(x)))


def _pack_bf16_pair(lo, hi):
    lo_bits = lax.bitcast_convert_type(lo.astype(BF16).astype(F32), U32)
    hi_bits = lax.bitcast_convert_type(hi.astype(BF16).astype(F32), U32)
    return (lo_bits >> 16) | (hi_bits & jnp.uint32(0xFFFF0000))


def _unpack_bf16_lo(w):
    return lax.bitcast_convert_type(w << 16, F32)


def _unpack_bf16_hi(w):
    return lax.bitcast_convert_type(w & jnp.uint32(0xFFFF0000), F32)


def _tile(n, candidates):
    for c in candidates:
        if n % c == 0:
            return c
    raise ValueError(f"no tile for {n} in {candidates}")


def _matmul_nt_kernel(x_ref, w_ref, o_ref):
    o_ref[...] = _dot_nt(x_ref[...], w_ref[...])


def _in_proj(xb, wt, w):
    n, d = xb.shape
    tm = _tile(n, (1024, 832, 640, 512, 256, 128, 64))
    tn = _tile(w, (1024, 512, 256, 128))
    return pl.pallas_call(
        _matmul_nt_kernel,
        out_shape=jax.ShapeDtypeStruct((n, w), F32),
        grid=(w // tn, n // tm),
        in_specs=[pl.BlockSpec((tm, d), lambda j, i: (i, 0)),
                  pl.BlockSpec((tn, d), lambda j, i: (j, 0))],
        out_specs=pl.BlockSpec((tm, tn), lambda j, i: (i, j)),
        compiler_params=_params(("parallel", "parallel"), 48),
        name="in_proj",
    )(xb, wt)


def _gates_kernel(x_ref, w_ref, b_ref, o_ref):
    x = x_ref[...]
    w = w_ref[...]
    xh = x.astype(BF16)
    xl = (x - xh.astype(F32)).astype(BF16)
    wh = w.astype(BF16)
    wl = (w - wh.astype(F32)).astype(BF16)
    o_ref[...] = _dot(xh, wh) + _dot(xl, wh) + _dot(xh, wl) + b_ref[...]


def _gates(x, wg, bg):
    n, d = x.shape
    tm = _tile(n, (512, 416, 320, 256, 128, 64))
    return pl.pallas_call(
        _gates_kernel,
        out_shape=jax.ShapeDtypeStruct((n, LANES), F32),
        grid=(n // tm,),
        in_specs=[pl.BlockSpec((tm, d), lambda i: (i, 0)),
                  pl.BlockSpec((d, LANES), lambda i: (0, 0)),
                  pl.BlockSpec((1, LANES), lambda i: (0, 0))],
        out_specs=pl.BlockSpec((tm, LANES), lambda i: (i, 0)),
        compiler_params=_params(("parallel",), 40),
        name="gates",
    )(x, wg, bg)


def _moba_prompt_kernel(q_ref, k_ref, v_ref, o_ref, kb_s, vt_s, km_s):
    i = pl.program_id(2)
    nb = kb_s.shape[0]
    blk = MOBA_BLOCK
    nq = ATT_GROUP * blk

    @pl.when(i == 0)
    def _():
        for j in range(nb):
            kj = k_ref[j * blk:(j + 1) * blk, :]
            kb_s[j] = kj.astype(BF16)
            km_s[j:j + 1, :] = jnp.mean(kj, axis=0, keepdims=True)
            vt_s[j] = v_ref[j * blk:(j + 1) * blk, :].T.astype(BF16)
        km_s[nb:, :] = jnp.zeros((km_s.shape[0] - nb, HEAD_DIM), F32)

    q = q_ref[...]
    q2 = jnp.concatenate([q[:, g * HEAD_DIM:(g + 1) * HEAD_DIM] for g in range(ATT_GROUP)], axis=0).astype(BF16)

    gate = _dot_nt(km_s[...].astype(BF16), q2)
    jj = lax.broadcasted_iota(I32, gate.shape, 0)
    g = jnp.where(jj < i, gate, -jnp.inf)
    sel = jnp.zeros(gate.shape, F32)
    for _ in range(MOBA_TOPK):
        mx = jnp.max(g, axis=0, keepdims=True)
        first = jnp.min(jnp.where(g == mx, jj, 2 * nb), axis=0, keepdims=True)
        pick = (jj == first) & (mx > -jnp.inf)
        sel = jnp.where(pick, 1.0, sel)
        g = jnp.where(pick, -jnp.inf, g)
    ri = lax.broadcasted_iota(I32, (blk, nq), 0)
    ci = lax.broadcasted_iota(I32, (blk, nq), 1)
    causal = ri <= (ci & (blk - 1))
    c2 = ATT_SCALE * LOG2_E

    def attend_all(own):
        m = l = acc = None
        for j in (own,) + tuple(range(own)):
            s = _dot_nt(kb_s[j], q2)
            s = jnp.where(causal if j == own else sel[j:j + 1, :] > 0.5, s, -jnp.inf)
            m_blk = jnp.max(s, axis=0, keepdims=True)
            m_new = m_blk if m is None else jnp.maximum(m, m_blk)
            p = jnp.exp2((s - m_new) * c2)
            pv = _dot(vt_s[j], p.astype(BF16))
            if m is None:
                l, acc = jnp.sum(p, axis=0, keepdims=True), pv
            else:
                alpha = jnp.exp2((m - m_new) * c2)
                l = alpha * l + jnp.sum(p, axis=0, keepdims=True)
                acc = alpha * acc + pv
            m = m_new
        out = (acc / l).T
        for g_i in range(ATT_GROUP):
            o_ref[:, g_i * HEAD_DIM:(g_i + 1) * HEAD_DIM] = out[g_i * blk:(g_i + 1) * blk, :].astype(o_ref.dtype)

    for own in range(nb):
        pl.when(i == own)(functools.partial(attend_all, own))


def _moba_prompt(z, bsz, seq):
    nb = seq // MOBA_BLOCK
    qw = ATT_GROUP * HEAD_DIM
    return pl.pallas_call(
        _moba_prompt_kernel,
        out_shape=jax.ShapeDtypeStruct((bsz * seq, ATT_Q_W), BF16),
        grid=(bsz, ATT_KV_HEADS, nb),
        in_specs=[pl.BlockSpec((MOBA_BLOCK, qw), lambda b, h, i: (b * nb + i, COL_Q // qw + h)),
                  pl.BlockSpec((seq, HEAD_DIM), lambda b, h, i: (b, COL_K // HEAD_DIM + h)),
                  pl.BlockSpec((seq, HEAD_DIM), lambda b, h, i: (b, COL_V // HEAD_DIM + h))],
        out_specs=pl.BlockSpec((MOBA_BLOCK, qw), lambda b, h, i: (b * nb + i, h)),
        scratch_shapes=[pltpu.VMEM((nb, MOBA_BLOCK, HEAD_DIM), BF16),
                        pltpu.VMEM((nb, HEAD_DIM, MOBA_BLOCK), BF16),
                        pltpu.VMEM((2 * SUBLANES, HEAD_DIM), F32)],
        compiler_params=_params(("parallel", "parallel", "arbitrary"), 32),
        name="moba_prompt",
    )(z, z, z)


def _moba_sample_kernel(n_pages, pt_ref, q_ref, kn_ref, vn_ref, *rest):
    del pt_ref
    kp = rest[:n_pages]
    vp = rest[n_pages:2 * n_pages]
    o_ref = rest[2 * n_pages]
    page = kp[0].shape[2]
    per_blk = MOBA_BLOCK // page
    nb = n_pages // per_blk
    nrow = page * ATT_KV_HEADS
    both = lambda a: jnp.concatenate([a] * ATT_GROUP, axis=0)

    q = q_ref[0]
    qb = q.astype(BF16)
    ri = lax.broadcasted_iota(I32, (ATT_HEADS, nrow), 0)
    ci = lax.broadcasted_iota(I32, (ATT_HEADS, nrow), 1)
    own = (ci % ATT_KV_HEADS) == (ri % ATT_KV_HEADS)

    scores, ksum = [], []
    for p in range(n_pages):
        k3 = kp[p][0, 0]
        scores.append(_dot_nt(qb, k3.reshape(nrow, HEAD_DIM).astype(BF16)) * ATT_SCALE)
        ksum.append(jnp.sum(k3, axis=0))
    gates = []
    for j in range(nb):
        kmean = sum(ksum[j * per_blk:(j + 1) * per_blk]) * (1.0 / MOBA_BLOCK)
        gates.append(jnp.sum(q * both(kmean), axis=1, keepdims=True))
    sel = []
    for j in range(nb):
        rank = jnp.zeros(gates[j].shape, F32)
        for jp in range(nb):
            if jp < j:
                rank = rank + (gates[jp] >= gates[j]).astype(F32)
            elif jp > j:
                rank = rank + (gates[jp] > gates[j]).astype(F32)
        sel.append(rank < MOBA_TOPK)

    s_new = jnp.sum(q * both(kn_ref[0]), axis=1, keepdims=True) * ATT_SCALE
    m = s_new
    for p in range(n_pages):
        scores[p] = jnp.where(sel[p // per_blk], jnp.where(own, scores[p], -jnp.inf), -jnp.inf)
        m = jnp.maximum(m, jnp.max(scores[p], axis=1, keepdims=True))
    p_new = jnp.exp(s_new - m)
    l = p_new
    acc = p_new * both(vn_ref[0])
    for p in range(n_pages):
        pp = jnp.exp(scores[p] - m)
        l = l + jnp.sum(pp, axis=1, keepdims=True)
        acc = acc + _dot(pp.astype(BF16), vp[p][0, 0].reshape(nrow, HEAD_DIM).astype(BF16))
    o_ref[0] = (acc / l).astype(o_ref.dtype)


def _moba_sample(zs, cache_k, cache_v, page_table, layer):
    ns = zs.shape[0]
    n_pages = page_table.shape[1]
    page = cache_k.shape[2]
    pt_flat = page_table.reshape(-1).astype(I32)
    qg = jnp.transpose(zs[:, COL_Q:COL_Q + ATT_Q_W].reshape(ns, ATT_KV_HEADS, ATT_GROUP, HEAD_DIM), (0, 2, 1, 3))
    kv_new = lambda col: zs[:, col:col + ATT_KV_W].reshape(ns, ATT_KV_HEADS, HEAD_DIM)

    def page_spec(p):
        return pl.BlockSpec((1, 1, page, ATT_KV_HEADS, HEAD_DIM),
                            lambda s, pt: (layer, pt[s * n_pages + p], 0, 0, 0))

    rows = lambda r: pl.BlockSpec((1, r, HEAD_DIM), lambda s, pt: (s, 0, 0))
    grid_spec = pltpu.PrefetchScalarGridSpec(
        num_scalar_prefetch=1, grid=(ns,),
        in_specs=[rows(ATT_HEADS), rows(ATT_KV_HEADS), rows(ATT_KV_HEADS)] + [page_spec(p) for p in range(n_pages)] * 2,
        out_specs=rows(ATT_HEADS))
    out = pl.pallas_call(
        functools.partial(_moba_sample_kernel, n_pages),
        out_shape=jax.ShapeDtypeStruct((ns, ATT_HEADS, HEAD_DIM), BF16),
        grid_spec=grid_spec,
        compiler_params=_params(("parallel",), 48),
        name="moba_sample",
    )(pt_flat, qg.reshape(ns, ATT_HEADS, HEAD_DIM), kv_new(COL_K), kv_new(COL_V),
      *([cache_k] * n_pages), *([cache_v] * n_pages))
    return jnp.transpose(out.reshape(ns, ATT_GROUP, ATT_KV_HEADS, HEAD_DIM), (0, 2, 1, 3)).reshape(ns, ATT_Q_W)


def _mlstm_prompt_kernel(q_ref, k_ref, v_ref, og_ref, igr_ref, fgr_ref, igc_ref, fgc_ref, ng_ref,
                         y_ref, c_ref, n_ref, m_ref):
    @pl.when(pl.program_id(2) == 0)
    def _():
        c_ref[...] = jnp.zeros(c_ref.shape, F32)
        n_ref[...] = jnp.zeros(n_ref.shape, F32)
        m_ref[...] = jnp.zeros(m_ref.shape, F32)

    L = q_ref.shape[0]
    q = q_ref[...] * M_Q_SCALE
    k = k_ref[...]
    ig_r = igr_ref[0, 0]
    lf_r = _log_sigmoid(fgr_ref[0, 0])
    ig_c = igc_ref[0, 0]
    lf_c = _log_sigmoid(fgc_ref[0, 0])
    ri = lax.broadcasted_iota(I32, (L, L), 0)
    ci = lax.broadcasted_iota(I32, (L, L), 1)
    tril = ci <= ri
    b_c = jnp.sum(jnp.where(tril, lf_r, 0.0), axis=1, keepdims=True)
    b_r = jnp.sum(jnp.where(ri <= ci, lf_c, 0.0), axis=0, keepdims=True)
    r_r = ig_r - b_r
    m0 = m_ref[0, 0][:, 0:1]
    m_c = b_c + jnp.maximum(m0, jnp.max(jnp.where(tril, r_r, -jnp.inf), axis=1, keepdims=True))
    dmat = jnp.exp(jnp.where(tril, b_c - m_c + r_r, -jnp.inf))
    qb = q.astype(BF16)
    vb = v_ref[...].astype(BF16)
    s = _dot_nt(qb, k.astype(BF16)) * dmat
    inter = jnp.exp(b_c + m0 - m_c)
    c0 = c_ref[0, 0]
    n0 = n_ref[0, 0]
    num = inter * _dot(qb, c0.astype(BF16)) + _dot(s.astype(BF16), vb)
    den = inter * jnp.sum(q * n0, axis=1, keepdims=True) + jnp.sum(s, axis=1, keepdims=True)
    h = num / jnp.maximum(jnp.abs(den), jnp.exp(-m_c))
    hn = h * lax.rsqrt(jnp.mean(h * h, axis=1, keepdims=True) + NORM_EPS)
    y_ref[...] = (hn * ng_ref[...] * jax.nn.sigmoid(og_ref[...])).astype(y_ref.dtype)

    b_last = b_c[L - 1:L, :]
    m_last = m_c[L - 1:L, :]
    kw = k * jnp.exp(b_last - b_c + ig_c - m_last)
    decay = jnp.exp(b_last + m0 - m_last)
    c_ref[0, 0] = decay * c0 + _dot(kw.T.astype(BF16), vb)
    n_ref[0, 0] = decay * n0 + jnp.sum(kw, axis=0, keepdims=True)
    m_ref[0, 0] = jnp.broadcast_to(m_last, (1, LANES))


def _mlstm_prompt(z, g_all, norm_g, bsz, seq):
    L = _tile(seq, (256, 128))
    nc = seq // L
    gp = jnp.transpose(g_all[:, :2 * M_HEADS].reshape(bsz, seq, 2 * M_HEADS), (0, 2, 1))
    g_row = gp.reshape(bsz, 2 * M_HEADS, 1, seq)
    g_col = gp.reshape(bsz, 2 * M_HEADS, seq, 1)
    row = lambda off: pl.BlockSpec((1, 1, 1, L), lambda b, h, c: (b, off + h, 0, c))
    col = lambda off: pl.BlockSpec((1, 1, L, 1), lambda b, h, c: (b, off + h, c, 0))
    state = lambda shape: pl.BlockSpec((1, 1) + shape, lambda b, h, c: (b, h, 0, 0))
    return pl.pallas_call(
        _mlstm_prompt_kernel,
        out_shape=(jax.ShapeDtypeStruct((bsz * seq, M_V_W), BF16),
                   jax.ShapeDtypeStruct((bsz, M_HEADS, M_QK, M_V), F32),
                   jax.ShapeDtypeStruct((bsz, M_HEADS, 1, M_QK), F32),
                   jax.ShapeDtypeStruct((bsz, M_HEADS, 1, LANES), F32)),
        grid=(bsz, M_HEADS, nc),
        in_specs=[pl.BlockSpec((L, M_QK), lambda b, h, c: (b * nc + c, COL_QM // M_QK + h)),
                  pl.BlockSpec((L, M_QK), lambda b, h, c: (b * nc + c, COL_KM // M_QK + h)),
                  pl.BlockSpec((L, M_V), lambda b, h, c: (b * nc + c, COL_VM // M_V + h)),
                  pl.BlockSpec((L, M_V), lambda b, h, c: (b * nc + c, COL_OM // M_V + h)),
                  row(0), row(M_HEADS), col(0), col(M_HEADS),
                  pl.BlockSpec((1, M_V), lambda b, h, c: (0, h))],
        out_specs=(pl.BlockSpec((L, M_V), lambda b, h, c: (b * nc + c, h)),
                   state((M_QK, M_V)), state((1, M_QK)), state((1, LANES))),
        compiler_params=_params(("parallel", "parallel", "arbitrary"), 32),
        name="mlstm_prompt",
    )(z, z, z, z, g_row, g_row, g_col, g_col, norm_g)


def _mlstm_sample_kernel(qr_ref, kr_ref, vr_ref, or_ref, qc_ref, kc_ref, g_ref, m0_ref, n0_ref, c0_ref, ng_ref,
                         y_ref, c_ref, n_ref, m_ref):
    g = g_ref[0]
    m0_all = m0_ref[0]
    lane = lax.broadcasted_iota(I32, (1, LANES), 1)
    m_out = jnp.zeros((1, LANES), F32)
    for h in range(M_HEADS):
        ig = g[:, h:h + 1]
        lf = _log_sigmoid(g[:, M_HEADS + h:M_HEADS + h + 1])
        m0 = m0_all[:, h:h + 1]
        m = jnp.maximum(lf + m0, ig)
        w = jnp.exp(ig - m)
        decay = jnp.exp(lf + m0 - m)
        q_r = qr_ref[0, h] * M_Q_SCALE
        k_r = kr_ref[0, h]
        v_r = vr_ref[0, h]
        q_c = qc_ref[0, h] * M_Q_SCALE
        c0 = c0_ref[0, 0, h]
        n0 = n0_ref[0, 0, h]
        s = jnp.sum(q_r * k_r, axis=1, keepdims=True) * w
        num = decay * jnp.sum(q_c * c0, axis=0, keepdims=True) + s * v_r
        den = decay * jnp.sum(q_r * n0, axis=1, keepdims=True) + s
        hh = num / jnp.maximum(jnp.abs(den), jnp.exp(-m))
        hn = hh * lax.rsqrt(jnp.mean(hh * hh, axis=1, keepdims=True) + NORM_EPS)
        y = hn * ng_ref[:, h * M_V:(h + 1) * M_V] * jax.nn.sigmoid(or_ref[0, h])
        y_ref[0, :, h * M_V:(h + 1) * M_V] = y.astype(y_ref.dtype)
        c_ref[0, h] = decay * c0 + (w * kc_ref[0, h]) * v_r
        n_ref[0, h] = decay * n0 + w * k_r
        m_out = jnp.where(lane == h, m, m_out)
    m_ref[0] = m_out


def _mlstm_sample(zs, gs, state_c, state_n, state_m, norm_g, layer):
    ns = zs.shape[0]
    qm = zs[:, COL_QM:COL_QM + M_QK_W]
    km = zs[:, COL_KM:COL_KM + M_QK_W]
    rows = lambda a, w: a.reshape(ns, M_HEADS, 1, w)
    m0 = jnp.pad(state_m[layer], ((0, 0), (0, LANES - M_HEADS))).reshape(ns, 1, LANES)
    row_spec = lambda w: pl.BlockSpec((1, M_HEADS, 1, w), lambda s: (s, 0, 0, 0))
    lane_spec = pl.BlockSpec((1, 1, LANES), lambda s: (s, 0, 0))
    col_spec = pl.BlockSpec((1, M_HEADS, M_QK, 1), lambda s: (s, 0, 0, 0))
    return pl.pallas_call(
        _mlstm_sample_kernel,
        out_shape=(jax.ShapeDtypeStruct((ns, 1, M_V_W), BF16),
                   jax.ShapeDtypeStruct((ns, M_HEADS, M_QK, M_V), F32),
                   jax.ShapeDtypeStruct((ns, M_HEADS, 1, M_QK), F32),
                   jax.ShapeDtypeStruct((ns, 1, LANES), F32)),
        grid=(ns,),
        in_specs=[row_spec(M_QK), row_spec(M_QK), row_spec(M_V), row_spec(M_V), col_spec, col_spec,
                  lane_spec, lane_spec,
                  pl.BlockSpec((1, 1, M_HEADS, 1, M_QK), lambda s: (layer, s, 0, 0, 0)),
                  pl.BlockSpec((1, 1, M_HEADS, M_QK, M_V), lambda s: (layer, s, 0, 0, 0)),
                  pl.BlockSpec((1, M_V_W), lambda s: (0, 0))],
        out_specs=(pl.BlockSpec((1, 1, M_V_W), lambda s: (s, 0, 0)),
                   pl.BlockSpec((1, M_HEADS, M_QK, M_V), lambda s: (s, 0, 0, 0)),
                   row_spec(M_QK), lane_spec),
        compiler_params=_params(("parallel",), 32),
        name="mlstm_sample",
    )(rows(qm, M_QK), rows(km, M_QK), rows(zs[:, COL_VM:COL_VM + M_V_W], M_V), rows(zs[:, COL_OM:COL_OM + M_V_W], M_V),
      qm.reshape(ns, M_HEADS, M_QK, 1), km.reshape(ns, M_HEADS, M_QK, 1),
      gs.reshape(ns, 1, LANES), m0,
      state_n.reshape(state_n.shape[0], ns, M_HEADS, 1, M_QK), state_c, norm_g)


def _layernorm_chunks(chunks, d):
    total = sum(jnp.sum(c, axis=1, keepdims=True) for c in chunks)
    mu = total * (1.0 / d)
    var = sum(jnp.sum(jnp.square(c - mu), axis=1, keepdims=True) for c in chunks) * (1.0 / d)
    return mu, lax.rsqrt(var + LN_EPS)


def _merge_kernel(alpha, *refs):
    att_ref, mix_ref, wa_ref, wb_ref, x_ref, g_ref, b_ref, wr_ref, br_ref = refs[:9]
    h_ref, hp_ref, ids_ref, wts_ref = refs[-4:]
    j = pl.program_id(1)
    tn = x_ref.shape[1]
    d = h_ref.shape[1]
    nj = d // tn
    cols = lambda c: slice(c * tn, (c + 1) * tn)
    pre = alpha * x_ref[...] + _dot(att_ref[...], wa_ref[...]) + _dot(mix_ref[...], wb_ref[...])
    for c in range(nj):
        @pl.when(j == c)
        def _(c=c):
            h_ref[:, cols(c)] = pre

    @pl.when(j == nj - 1)
    def _():
        mu, rstd = _layernorm_chunks([h_ref[:, cols(c)] for c in range(nj)], d)
        lg = br_ref[...]
        for c in range(nj):
            y = (h_ref[:, cols(c)] - mu) * rstd * g_ref[:, cols(c)] + b_ref[:, cols(c)]
            h_ref[:, cols(c)] = y
            lg = lg + _dot(y.astype(BF16), wr_ref[cols(c), :])
        for c in range(nj // 2):
            hp_ref[:, cols(c)] = _pack_bf16_pair(h_ref[:, cols(c)], h_ref[:, cols(c + nj // 2)])
        lane = lax.broadcasted_iota(I32, lg.shape, 1)
        lane_f = lane.astype(F32)
        far = float(LANES)
        gl = jnp.where(lane < N_GROUPS, lg, -jnp.inf)
        gmax = jnp.max(gl, axis=1, keepdims=True)
        gsel = jnp.min(jnp.where(gl == gmax, lane_f, far), axis=1, keepdims=True).astype(I32)
        gw = 1.0 / jnp.sum(jnp.where(lane < N_GROUPS, jnp.exp(gl - gmax), 0.0), axis=1, keepdims=True)
        e_lane = lane - N_GROUPS
        in_grp = (e_lane >= 0) & (e_lane < N_EXPERTS) & ((e_lane // EXPERTS_PER_GROUP) == gsel)
        el = jnp.where(in_grp, lg, -jnp.inf)
        t1 = jnp.max(el, axis=1, keepdims=True)
        i1 = jnp.min(jnp.where(el == t1, lane_f, far), axis=1, keepdims=True)
        el2 = jnp.where(lane_f == i1, -jnp.inf, el)
        t2 = jnp.max(el2, axis=1, keepdims=True)
        i2 = jnp.min(jnp.where(el2 == t2, lane_f, far), axis=1, keepdims=True)
        e21 = jnp.exp(t2 - t1)
        w1 = gw / (1.0 + e21)
        w2 = w1 * e21
        ids = jnp.where(lane == 0, i1, jnp.where(lane == 1, i2, float(N_GROUPS))) - float(N_GROUPS)
        ids_ref[...] = ids.astype(I32)
        wts_ref[...] = jnp.where(lane == 0, w1, jnp.where(lane == 1, w2, 0.0))


def _merge(att, mix, wob, x, ln_g, ln_b, wr, br, alpha, n_total, row0, prev=None):
    r, d = x.shape
    half = att.shape[1]
    tm = _tile(r, (512, 256, 128, 64))
    assert row0 % tm == 0
    o0 = row0 // tm
    tn = MERGE_CHUNK
    rows = lambda w: pl.BlockSpec((1, w), lambda i, j: (0, 0))
    prev = () if prev is None else tuple(prev)
    n_in = 9
    return pl.pallas_call(
        functools.partial(_merge_kernel, alpha),
        out_shape=(jax.ShapeDtypeStruct((n_total, d), F32),
                   jax.ShapeDtypeStruct((n_total, d // 2), U32),
                   jax.ShapeDtypeStruct((n_total, LANES), I32),
                   jax.ShapeDtypeStruct((n_total, LANES), F32)),
        grid=(r // tm, d // tn),
        in_specs=[pl.BlockSpec((tm, half), lambda i, j: (i, 0)),
                  pl.BlockSpec((tm, half), lambda i, j: (i, 0)),
                  pl.BlockSpec((half, tn), lambda i, j: (0, j)),
                  pl.BlockSpec((half, tn), lambda i, j: (1, j)),
                  pl.BlockSpec((tm, tn), lambda i, j: (i, j)),
                  rows(d), rows(d),
                  pl.BlockSpec((d, LANES), lambda i, j: (0, 0)),
                  rows(LANES)] + [pl.BlockSpec(memory_space=pl.ANY)] * len(prev),
        out_specs=(pl.BlockSpec((tm, d), lambda i, j: (i + o0, 0)),
                   pl.BlockSpec((tm, d // 2), lambda i, j: (i + o0, 0)),
                   pl.BlockSpec((tm, LANES), lambda i, j: (i + o0, 0)),
                   pl.BlockSpec((tm, LANES), lambda i, j: (i + o0, 0))),
        input_output_aliases={n_in + k: k for k in range(len(prev))},
        compiler_params=_params(("parallel", "arbitrary"), 52),
        name="merge_ln_router",
    )(att, mix, wob, wob, x, ln_g, ln_b, wr, br, *prev)


def _moe_kernel(te_ref, ti_ref, nu_ref, src_ref, hp_ref, wg_ref, wu_ref, wd_ref, o_ref,
                xp_s, xb_s, hg_s, hu_s, a_s, lo_s, sem):
    del te_ref, ti_ref
    t = pl.program_id(0)
    ph = pl.program_id(1)
    nk, tm, kc = xb_s.shape
    nn = 2 * lo_s.shape[0]
    share = tm // (nk + nn)
    assert share * (nk + nn) == tm and share % SUBLANES == 0 and nk % 2 == 0
    nu = nu_ref[0]

    def start_rows(tile, lo, count):
        slot = tile % 2

        def body(g, carry):
            base = pl.multiple_of(lo + g * SUBLANES, SUBLANES)
            for u in range(SUBLANES):
                pltpu.make_async_copy(hp_ref.at[pl.ds(src_ref[tile * tm + base + u], 1), :],
                                      xp_s.at[slot, pl.ds(base + u, 1), :],
                                      sem.at[slot]).start(priority=GATHER_PRIORITY)
            return carry
        lax.fori_loop(0, count // SUBLANES, body, 0)

    def wait_tile(tile):
        slot = tile % 2
        pltpu.make_async_copy(hp_ref.at[pl.ds(0, tm), :], xp_s.at[slot], sem.at[slot]).wait()

    @pl.when(t < nu)
    def _():
        @pl.when(t + 1 < nu)
        def _():
            start_rows(t + 1, ph * share, share)

        @pl.when(ph == 0)
        def _():
            @pl.when(t == 0)
            def _():
                start_rows(t, 0, tm)

            wait_tile(t)
            for c in range(nk // 2):
                w = xp_s[t % 2, :, c * kc:(c + 1) * kc]
                xb_s[c] = _unpack_bf16_lo(w).astype(BF16)
                xb_s[c + nk // 2] = _unpack_bf16_hi(w).astype(BF16)

        @pl.when(ph < nk)
        def _():
            x = xb_s[ph]
            pg = _dot(x, wg_ref[0, 0].astype(BF16))
            pu = _dot(x, wu_ref[0, 0].astype(BF16))

            @pl.when(ph == 0)
            def _():
                hg_s[...] = pg
                hu_s[...] = pu

            @pl.when(ph != 0)
            def _():
                hg_s[...] += pg
                hu_s[...] += pu

        @pl.when(ph == nk - 1)
        def _():
            hg = hg_s[...]
            a_s[...] = (hg * jax.nn.sigmoid(hg) * hu_s[...]).astype(BF16)

        @pl.when((ph >= nk) & (ph < nk + nn // 2))
        def _():
            lo_s[ph - nk] = _dot(a_s[...], wd_ref[0, 0].astype(BF16))

        @pl.when(ph >= nk + nn // 2)
        def _():
            o_ref[...] = _pack_bf16_pair(lo_s[ph - nk - nn // 2], _dot(a_s[...], wd_ref[0, 0].astype(BF16)))


def _moe_experts(hp, src, w_gate, w_up, w_down, tile_expert, tile_index, n_used, layer):
    tm = MOE_TILE
    n_tiles = tile_expert.shape[0]
    d = w_gate.shape[2]
    f = w_gate.shape[3]
    kc = nc = MOE_CHUNK
    nk = d // kc
    nn = d // nc

    def used(t, nu, a, b):
        return jnp.where(t < nu[0], a, b)

    def k_idx(t, ph, nu):
        return used(t, nu, jnp.minimum(ph, nk - 1), nk - 1)

    def n_idx(t, ph, nu):
        return used(t, nu, jnp.maximum(ph - nk, 0), nn - 1)

    def o_idx(t, ph, nu):
        return used(t, nu, jnp.maximum(ph - nk - nn // 2, 0), nn // 2 - 1)

    grid_spec = pltpu.PrefetchScalarGridSpec(
        num_scalar_prefetch=4, grid=(n_tiles, nk + nn),
        in_specs=[pl.BlockSpec(memory_space=pl.ANY),
                  pl.BlockSpec((1, 1, kc, f), lambda t, ph, te, ti, nu, src: (layer, te[t], k_idx(t, ph, nu), 0)),
                  pl.BlockSpec((1, 1, kc, f), lambda t, ph, te, ti, nu, src: (layer, te[t], k_idx(t, ph, nu), 0)),
                  pl.BlockSpec((1, 1, f, nc), lambda t, ph, te, ti, nu, src: (layer, te[t], 0, n_idx(t, ph, nu)))],
        out_specs=pl.BlockSpec((tm, nc), lambda t, ph, te, ti, nu, src: (ti[t], o_idx(t, ph, nu))),
        scratch_shapes=[pltpu.VMEM((2, tm, d // 2), U32), pltpu.VMEM((nk, tm, kc), BF16),
                        pltpu.VMEM((tm, f), F32), pltpu.VMEM((tm, f), F32), pltpu.VMEM((tm, f), BF16),
                        pltpu.VMEM((nn // 2, tm, nc), F32), pltpu.SemaphoreType.DMA((2,))])
    return pl.pallas_call(
        _moe_kernel,
        out_shape=jax.ShapeDtypeStruct((n_tiles * tm, d // 2), U32),
        grid_spec=grid_spec,
        compiler_params=_params(("arbitrary", "arbitrary"), 48),
        name="moe_experts",
    )(tile_expert, tile_index, n_used, src, hp, w_gate, w_up, w_down)


def _combine_kernel(alpha, row0, n_tok, pos_ref, h_ref, w_ref, g_ref, b_ref, ys_ref, o_ref, f_s, sem):
    i = pl.program_id(0)
    tm, d = o_ref.shape

    def start_tile(tile):
        slot = tile % 2

        def body(g, carry):
            base = pl.multiple_of(g * SUBLANES, SUBLANES)
            for u in range(SUBLANES):
                for s in range(2):
                    pltpu.make_async_copy(
                        ys_ref.at[pl.ds(pos_ref[s * n_tok + row0 + tile * tm + base + u], 1), :],
                        f_s.at[slot, s, pl.ds(base + u, 1), :], sem.at[slot]).start(priority=GATHER_PRIORITY)
            return carry
        lax.fori_loop(0, tm // SUBLANES, body, 0)

    def wait_tile(tile):
        slot = tile % 2
        for s in range(2):
            pltpu.make_async_copy(ys_ref.at[pl.ds(0, tm), :], f_s.at[slot, s], sem.at[slot]).wait()

    @pl.when(i == 0)
    def _():
        start_tile(i)

    @pl.when(i + 1 < pl.num_programs(0))
    def _():
        start_tile(i + 1)

    wait_tile(i)
    w = w_ref[...]
    w1, w2 = w[:, 0:1], w[:, 1:2]
    f0, f1 = f_s[i % 2, 0], f_s[i % 2, 1]
    half = d // 2
    v = jnp.concatenate(
        [alpha * h_ref[:, :half] + w1 * _unpack_bf16_lo(f0) + w2 * _unpack_bf16_lo(f1),
         alpha * h_ref[:, half:] + w1 * _unpack_bf16_hi(f0) + w2 * _unpack_bf16_hi(f1)], axis=1)
    mu = jnp.mean(v, axis=1, keepdims=True)
    vc = v - mu
    var = jnp.sum(vc * vc, axis=1, keepdims=True) * (1.0 / d)
    o_ref[...] = vc * lax.rsqrt(var + LN_EPS) * g_ref[...] + b_ref[...]


def _combine(h, ys, pos, wts, ln_g, ln_b, alpha, row0, nrows):
    n, d = h.shape
    tm = next(c for c in (128, 64, 32, 16, 8) if nrows % c == 0 and row0 % c == 0)
    o0 = row0 // tm
    rows = pl.BlockSpec((1, d), lambda i, pos: (0, 0))
    grid_spec = pltpu.PrefetchScalarGridSpec(
        num_scalar_prefetch=1, grid=(nrows // tm,),
        in_specs=[pl.BlockSpec((tm, d), lambda i, pos: (i + o0, 0)),
                  pl.BlockSpec((tm, LANES), lambda i, pos: (i + o0, 0)), rows, rows,
                  pl.BlockSpec(memory_space=pl.ANY)],
        out_specs=pl.BlockSpec((tm, d), lambda i, pos: (i, 0)),
        scratch_shapes=[pltpu.VMEM((2, 2, tm, d // 2), U32), pltpu.SemaphoreType.DMA((2,))])
    return pl.pallas_call(
        functools.partial(_combine_kernel, alpha, row0, n),
        out_shape=jax.ShapeDtypeStruct((nrows, d), F32),
        grid_spec=grid_spec,
        compiler_params=_params(("arbitrary",), 32),
        name="combine_ln",
    )(pos, h, wts, ln_g, ln_b, ys)


def _routing_tables(ids, n_tiles):
    n = ids.shape[0]
    tm = MOE_TILE
    e = ids[:, :2].reshape(-1)
    onehot = (e[:, None] == jnp.arange(N_EXPERTS, dtype=I32)[None, :]).astype(I32)
    csum = jnp.cumsum(onehot, axis=0)
    rank = jnp.sum((csum - onehot) * onehot, axis=1)
    counts = csum[-1]
    tiles = (counts + tm - 1) // tm
    tile_end = jnp.cumsum(tiles)
    n_used = tile_end[-1]
    pos = ((tile_end - tiles)[e] * tm + rank).astype(I32)
    src = (jnp.arange(n_tiles * tm, dtype=I32) % n).at[pos].set(jnp.arange(2 * n, dtype=I32) // 2)
    t = jnp.minimum(jnp.arange(n_tiles, dtype=I32), n_used - 1)
    tile_expert = jnp.minimum(jnp.sum((tile_end[None, :] <= t[:, None]).astype(I32), axis=1), N_EXPERTS - 1)
    pos2 = pos.reshape(n, 2)
    back = jnp.concatenate([pos2[:, 0], pos2[:, 1]])
    return src, back, tile_expert, t, n_used.reshape(1).astype(I32)


def _layer(l, x_p, x_s, bsz, seq, cache_k, cache_v, state_c, state_n, state_m, page_table,
           w_in, b_gates, norm_g, w_out, ln1_g, ln1_b, w_group, b_group, w_router, b_router,
           w_e_gate, w_e_up, w_e_down, ln2_g, ln2_b, alpha):
    n_p, d = x_p.shape
    n_s = x_s.shape[0]
    n = n_p + n_s

    wt = jnp.transpose(w_in[l]).astype(BF16)
    wg = jnp.pad(w_in[l, :, COL_G:], ((0, 0), (0, LANES - 2 * M_HEADS)))
    bg = jnp.pad(b_gates[l], (0, LANES - 2 * M_HEADS)).reshape(1, LANES)
    z_p = _in_proj(x_p.astype(BF16), wt, COL_G)
    z_s = _in_proj(x_s.astype(BF16), wt, COL_G)
    g_p = _gates(x_p, wg, bg)
    g_s = _gates(x_s, wg, bg)

    ng = norm_g[l].reshape(1, M_V_W)
    att_p = _moba_prompt(z_p, bsz, seq)
    mix_p, c_p, n_pr, m_p = _mlstm_prompt(z_p, g_p, ng, bsz, seq)
    att_s = _moba_sample(z_s, cache_k, cache_v, page_table, l)
    mix_s, c_s, n_sm, m_s = _mlstm_sample(z_s, g_s, state_c, state_n, state_m, ng, l)

    wr = jnp.pad(jnp.concatenate([w_group[l], w_router[l]], axis=1),
                 ((0, 0), (0, LANES - N_GROUPS - N_EXPERTS))).astype(BF16)
    br = jnp.pad(jnp.concatenate([b_group[l], b_router[l]]), (0, LANES - N_GROUPS - N_EXPERTS)).reshape(1, LANES)
    wob = w_out[l].astype(BF16)
    g1, b1 = ln1_g[l].reshape(1, d), ln1_b[l].reshape(1, d)
    done = _merge(att_p, mix_p, wob, x_p, g1, b1, wr, br, alpha, n, 0)
    h2, hp, ids, wts = _merge(att_s, mix_s.reshape(n_s, M_V_W), wob, x_s, g1, b1, wr, br, alpha, n, n_p, prev=done)

    n_tiles = (2 * n) // MOE_TILE + N_EXPERTS + 1
    src, back, tile_expert, tile_index, n_used = _routing_tables(ids, n_tiles)
    ys = _moe_experts(hp, src, w_e_gate, w_e_up, w_e_down, tile_expert, tile_index, n_used, l)
    g2, b2 = ln2_g[l].reshape(1, d), ln2_b[l].reshape(1, d)
    y_p = _combine(h2, ys, back, wts, g2, b2, alpha, 0, n_p)
    y_s = _combine(h2, ys, back, wts, g2, b2, alpha, n_p, n_s)

    kv = lambda rows, col: rows[:, col:col + ATT_KV_W]
    return (y_p, y_s,
            kv(z_p, COL_K).reshape(bsz, seq, ATT_KV_HEADS, HEAD_DIM),
            kv(z_p, COL_V).reshape(bsz, seq, ATT_KV_HEADS, HEAD_DIM),
            kv(z_s, COL_K).reshape(n_s, 1, ATT_KV_HEADS, HEAD_DIM),
            kv(z_s, COL_V).reshape(n_s, 1, ATT_KV_HEADS, HEAD_DIM),
            c_p, n_pr.reshape(bsz, M_HEADS, M_QK), m_p[:, :, 0, 0],
            c_s, n_sm.reshape(n_s, M_HEADS, M_QK), m_s[:, 0, :M_HEADS])


def kernel(x_prompt, x_sample, cache_k, cache_v, state_mlstm_c, state_mlstm_n, state_mlstm_m, page_table,
           w_in, b_gates, mlstm_norm_g, w_out, ln1_g, ln1_b, w_group, b_group, w_router, b_router,
           w_e_gate, w_e_up, w_e_down, ln2_g, ln2_b):
    bsz, seq, d = x_prompt.shape
    n_s, dec_seq, _ = x_sample.shape
    depth = w_in.shape[0]
    assert dec_seq == 1 and seq % MOBA_BLOCK == 0 and w_in.shape[2] == COL_G + 2 * M_HEADS
    assert (page_table.shape[1] * cache_k.shape[2]) % MOBA_BLOCK == 0 and MOBA_BLOCK % cache_k.shape[2] == 0
    alpha = (2 * depth) ** 0.25
    x_p = x_prompt.reshape(bsz * seq, d)
    x_s = x_sample.reshape(n_s, d)
    per_layer = []
    for l in range(depth):
        outs = _layer(l, x_p, x_s, bsz, seq, cache_k, cache_v, state_mlstm_c, state_mlstm_n, state_mlstm_m,
                      page_table, w_in, b_gates, mlstm_norm_g, w_out, ln1_g, ln1_b, w_group, b_group, w_router,
                      b_router, w_e_gate, w_e_up, w_e_down, ln2_g, ln2_b, alpha)
        x_p, x_s = outs[0], outs[1]
        per_layer.append(outs[2:])
    stacked = [jnp.stack([p[i] for p in per_layer]) for i in range(10)]
    return (x_p.reshape(bsz, seq, d), x_s.reshape(n_s, 1, d), *stacked)
```

```python
import functools

import jax
import jax.numpy as jnp
from jax import lax
from jax.experimental import pallas as pl
from jax.experimental.pallas import tpu as pltpu

F32, BF16, I32 = jnp.float32, jnp.bfloat16, jnp.int32

SUBLANES = 8
LANES = 128
MIB = 1024 * 1024

HEAD_DIM = 128
ATT_HEADS = 16
ATT_KV_HEADS = 8
ATT_GROUP = ATT_HEADS // ATT_KV_HEADS
ATT_SCALE = HEAD_DIM ** -0.5
LOG2_E = 1.4426950408889634
MOBA_BLOCK = 256
MOBA_TOPK = 3
M_HEADS = 4
M_QK = 256
M_V = 512
M_Q_SCALE = M_QK ** -0.5
N_GROUPS = 4
EXPERTS_PER_GROUP = 8
N_EXPERTS = N_GROUPS * EXPERTS_PER_GROUP
LN_EPS = 1e-5
NORM_EPS = 1e-6

ATT_Q_W = ATT_HEADS * HEAD_DIM
ATT_KV_W = ATT_KV_HEADS * HEAD_DIM
M_QK_W = M_HEADS * M_QK
M_V_W = M_HEADS * M_V
COL_Q = 0
COL_K = COL_Q + ATT_Q_W
COL_V = COL_K + ATT_KV_W
COL_QM = COL_V + ATT_KV_W
COL_KM = COL_QM + M_QK_W
COL_VM = COL_KM + M_QK_W
COL_OM = COL_VM + M_V_W
COL_G = COL_OM + M_V_W

MOE_TILE = 640
MOE_CHUNK = 1024
GATHER_PRIORITY = 1


def _params(sem, vmem_mib):
    return pltpu.CompilerParams(dimension_semantics=sem, vmem_limit_bytes=vmem_mib * MIB)


def _dot(a, b):
    return jnp.dot(a, b, preferred_element_type=F32)


def _dot_nt(a, b):
    return lax.dot_general(a, b, (((1,), (1,)), ((), ())), preferred_element_type=F32)


def _log_sigmoid(x):
    return jnp.minimum(x, 0.0) - jnp.log1p(jnp.exp(-jnp.abs(x)))


def _tile(n, candidates):
    for c in candidates:
        if n % c == 0:
            return c
    raise ValueError(f"no tile for {n} in {candidates}")


def _matmul_nt_kernel(x_ref, w_ref, o_ref):
    o_ref[...] = _dot_nt(x_ref[...], w_ref[...])


def _in_proj(xb, wt, w):
    n, d = xb.shape
    tm = _tile(n, (1024, 832, 640, 512, 256, 128, 64))
    tn = _tile(w, (1024, 512, 256, 128))
    return pl.pallas_call(
        _matmul_nt_kernel,
        out_shape=jax.ShapeDtypeStruct((n, w), F32),
        grid=(w // tn, n // tm),
        in_specs=[pl.BlockSpec((tm, d), lambda j, i: (i, 0)),
                  pl.BlockSpec((tn, d), lambda j, i: (j, 0))],
        out_specs=pl.BlockSpec((tm, tn), lambda j, i: (i, j)),
        compiler_params=_params(("parallel", "parallel"), 48),
        name="in_proj",
    )(xb, wt)


def _gates_kernel(x_ref, w_ref, b_ref, o_ref):
    x = x_ref[...]
    w = w_ref[...]
    xh = x.astype(BF16)
    xl = (x - xh.astype(F32)).astype(BF16)
    wh = w.astype(BF16)
    wl = (w - wh.astype(F32)).astype(BF16)
    o_ref[...] = _dot(xh, wh) + _dot(xl, wh) + _dot(xh, wl) + b_ref[...]


def _gates(x, wg, bg):
    n, d = x.shape
    tm = _tile(n, (512, 416, 320, 256, 128, 64))
    return pl.pallas_call(
        _gates_kernel,
        out_shape=jax.ShapeDtypeStruct((n, LANES), F32),
        grid=(n // tm,),
        in_specs=[pl.BlockSpec((tm, d), lambda i: (i, 0)),
                  pl.BlockSpec((d, LANES), lambda i: (0, 0)),
                  pl.BlockSpec((1, LANES), lambda i: (0, 0))],
        out_specs=pl.BlockSpec((tm, LANES), lambda i: (i, 0)),
        compiler_params=_params(("parallel",), 40),
        name="gates",
    )(x, wg, bg)


def _moba_prompt_kernel(q_ref, k_ref, v_ref, o_ref, kb_s, vt_s, km_s):
    i = pl.program_id(2)
    nb = kb_s.shape[0]
    blk = MOBA_BLOCK
    nq = ATT_GROUP * blk

    @pl.when(i == 0)
    def _():
        for j in range(nb):
            kj = k_ref[j * blk:(j + 1) * blk, :]
            kb_s[j] = kj.astype(BF16)
            km_s[j:j + 1, :] = jnp.mean(kj, axis=0, keepdims=True)
            vt_s[j] = v_ref[j * blk:(j + 1) * blk, :].T.astype(BF16)
        km_s[nb:, :] = jnp.zeros((km_s.shape[0] - nb, HEAD_DIM), F32)

    q = q_ref[...]
    q2 = jnp.concatenate([q[:, g * HEAD_DIM:(g + 1) * HEAD_DIM] for g in range(ATT_GROUP)], axis=0).astype(BF16)

    gate = _dot_nt(km_s[...].astype(BF16), q2)
    jj = lax.broadcasted_iota(I32, gate.shape, 0)
    g = jnp.where(jj < i, gate, -jnp.inf)
    sel = jnp.zeros(gate.shape, F32)
    for _ in range(MOBA_TOPK):
        mx = jnp.max(g, axis=0, keepdims=True)
        first = jnp.min(jnp.where(g == mx, jj, 2 * nb), axis=0, keepdims=True)
        pick = (jj == first) & (mx > -jnp.inf)
        sel = jnp.where(pick, 1.0, sel)
        g = jnp.where(pick, -jnp.inf, g)
    ri = lax.broadcasted_iota(I32, (blk, nq), 0)
    ci = lax.broadcasted_iota(I32, (blk, nq), 1)
    causal = ri <= (ci & (blk - 1))
    c2 = ATT_SCALE * LOG2_E

    def attend_all(own):
        m = l = acc = None
        for j in (own,) + tuple(range(own)):
            s = _dot_nt(kb_s[j], q2)
            s = jnp.where(causal if j == own else sel[j:j + 1, :] > 0.5, s, -jnp.inf)
            m_blk = jnp.max(s, axis=0, keepdims=True)
            m_new = m_blk if m is None else jnp.maximum(m, m_blk)
            p = jnp.exp2((s - m_new) * c2)
            pv = _dot(vt_s[j], p.astype(BF16))
            if m is None:
                l, acc = jnp.sum(p, axis=0, keepdims=True), pv
            else:
                alpha = jnp.exp2((m - m_new) * c2)
                l = alpha * l + jnp.sum(p, axis=0, keepdims=True)
                acc = alpha * acc + pv
            m = m_new
        out = (acc / l).T
        for g_i in range(ATT_GROUP):
            o_ref[:, g_i * HEAD_DIM:(g_i + 1) * HEAD_DIM] = out[g_i * blk:(g_i + 1) * blk, :].astype(o_ref.dtype)

    for own in range(nb):
        pl.when(i == own)(functools.partial(attend_all, own))


def _moba_prompt(z, bsz, seq):
    nb = seq // MOBA_BLOCK
    qw = ATT_GROUP * HEAD_DIM
    return pl.pallas_call(
        _moba_prompt_kernel,
        out_shape=jax.ShapeDtypeStruct((bsz * seq, ATT_Q_W), BF16),
        grid=(bsz, ATT_KV_HEADS, nb),
        in_specs=[pl.BlockSpec((MOBA_BLOCK, qw), lambda b, h, i: (b * nb + i, COL_Q // qw + h)),
                  pl.BlockSpec((seq, HEAD_DIM), lambda b, h, i: (b, COL_K // HEAD_DIM + h)),
                  pl.BlockSpec((seq, HEAD_DIM), lambda b, h, i: (b, COL_V // HEAD_DIM + h))],
        out_specs=pl.BlockSpec((MOBA_BLOCK, qw), lambda b, h, i: (b * nb + i, h)),
        scratch_shapes=[pltpu.VMEM((nb, MOBA_BLOCK, HEAD_DIM), BF16),
                        pltpu.VMEM((nb, HEAD_DIM, MOBA_BLOCK), BF16),
                        pltpu.VMEM((2 * SUBLANES, HEAD_DIM), F32)],
        compiler_params=_params(("parallel", "parallel", "arbitrary"), 32),
        name="moba_prompt",
    )(z, z, z)


def _moba_sample_kernel(n_pages, pt_ref, q_ref, kn_ref, vn_ref, *rest):
    del pt_ref
    kp = rest[:n_pages]
    vp = rest[n_pages:2 * n_pages]
    o_ref = rest[2 * n_pages]
    page = kp[0].shape[2]
    per_blk = MOBA_BLOCK // page
    nb = n_pages // per_blk
    nrow = page * ATT_KV_HEADS
    both = lambda a: jnp.concatenate([a] * ATT_GROUP, axis=0)

    q = q_ref[0]
    qb = q.astype(BF16)
    ri = lax.broadcasted_iota(I32, (ATT_HEADS, nrow), 0)
    ci = lax.broadcasted_iota(I32, (ATT_HEADS, nrow), 1)
    own = (ci % ATT_KV_HEADS) == (ri % ATT_KV_HEADS)

    scores, ksum = [], []
    for p in range(n_pages):
        k3 = kp[p][0, 0]
        scores.append(_dot_nt(qb, k3.reshape(nrow, HEAD_DIM).astype(BF16)) * ATT_SCALE)
        ksum.append(jnp.sum(k3, axis=0))
    gates = []
    for j in range(nb):
        kmean = sum(ksum[j * per_blk:(j + 1) * per_blk]) * (1.0 / MOBA_BLOCK)
        gates.append(jnp.sum(q * both(kmean), axis=1, keepdims=True))
    sel = []
    for j in range(nb):
        rank = jnp.zeros(gates[j].shape, F32)
        for jp in range(nb):
            if jp < j:
                rank = rank + (gates[jp] >= gates[j]).astype(F32)
            elif jp > j:
                rank = rank + (gates[jp] > gates[j]).astype(F32)
        sel.append(rank < MOBA_TOPK)

    s_new = jnp.sum(q * both(kn_ref[0]), axis=1, keepdims=True) * ATT_SCALE
    m = s_new
    for p in range(n_pages):
        scores[p] = jnp.where(sel[p // per_blk], jnp.where(own, scores[p], -jnp.inf), -jnp.inf)
        m = jnp.maximum(m, jnp.max(scores[p], axis=1, keepdims=True))
    p_new = jnp.exp(s_new - m)
    l = p_new
    acc = p_new * both(vn_ref[0])
    for p in range(n_pages):
        pp = jnp.exp(scores[p] - m)
        l = l + jnp.sum(pp, axis=1, keepdims=True)
        acc = acc + _dot(pp.astype(BF16), vp[p][0, 0].reshape(nrow, HEAD_DIM).astype(BF16))
    o_ref[0] = (acc / l).astype(o_ref.dtype)


def _moba_sample(zs, cache_k, cache_v, page_table, layer):
    ns = zs.shape[0]
    n_pages = page_table.shape[1]
    page = cache_k.shape[2]
    pt_flat = page_table.reshape(-1).astype(I32)
    qg = jnp.transpose(zs[:, COL_Q:COL_Q + ATT_Q_W].reshape(ns, ATT_KV_HEADS, ATT_GROUP, HEAD_DIM), (0, 2, 1, 3))
    kv_new = lambda col: zs[:, col:col + ATT_KV_W].reshape(ns, ATT_KV_HEADS, HEAD_DIM)

    def page_spec(p):
        return pl.BlockSpec((1, 1, page, ATT_KV_HEADS, HEAD_DIM),
                            lambda s, pt: (layer, pt[s * n_pages + p], 0, 0, 0))

    rows = lambda r: pl.BlockSpec((1, r, HEAD_DIM), lambda s, pt: (s, 0, 0))
    grid_spec = pltpu.PrefetchScalarGridSpec(
        num_scalar_prefetch=1, grid=(ns,),
        in_specs=[rows(ATT_HEADS), rows(ATT_KV_HEADS), rows(ATT_KV_HEADS)] + [page_spec(p) for p in range(n_pages)] * 2,
        out_specs=rows(ATT_HEADS))
    out = pl.pallas_call(
        functools.partial(_moba_sample_kernel, n_pages),
        out_shape=jax.ShapeDtypeStruct((ns, ATT_HEADS, HEAD_DIM), BF16),
        grid_spec=grid_spec,
        compiler_params=_params(("parallel",), 48),
        name="moba_sample",
    )(pt_flat, qg.reshape(ns, ATT_HEADS, HEAD_DIM), kv_new(COL_K), kv_new(COL_V),
      *([cache_k] * n_pages), *([cache_v] * n_pages))
    return jnp.transpose(out.reshape(ns, ATT_GROUP, ATT_KV_HEADS, HEAD_DIM), (0, 2, 1, 3)).reshape(ns, ATT_Q_W)


def _mlstm_prompt_kernel(q_ref, k_ref, v_ref, og_ref, igr_ref, fgr_ref, igc_ref, fgc_ref, ng_ref,
                         y_ref, c_ref, n_ref, m_ref):
    @pl.when(pl.program_id(2) == 0)
    def _():
        c_ref[...] = jnp.zeros(c_ref.shape, F32)
        n_ref[...] = jnp.zeros(n_ref.shape, F32)
        m_ref[...] = jnp.zeros(m_ref.shape, F32)

    L = q_ref.shape[0]
    q = q_ref[...] * M_Q_SCALE
    k = k_ref[...]
    ig_r = igr_ref[0, 0]
    lf_r = _log_sigmoid(fgr_ref[0, 0])
    ig_c = igc_ref[0, 0]
    lf_c = _log_sigmoid(fgc_ref[0, 0])
    ri = lax.broadcasted_iota(I32, (L, L), 0)
    ci = lax.broadcasted_iota(I32, (L, L), 1)
    tril = ci <= ri
    b_c = jnp.sum(jnp.where(tril, lf_r, 0.0), axis=1, keepdims=True)
    b_r = jnp.sum(jnp.where(ri <= ci, lf_c, 0.0), axis=0, keepdims=True)
    r_r = ig_r - b_r
    m0 = m_ref[0, 0][:, 0:1]
    m_c = b_c + jnp.maximum(m0, jnp.max(jnp.where(tril, r_r, -jnp.inf), axis=1, keepdims=True))
    dmat = jnp.exp(jnp.where(tril, b_c - m_c + r_r, -jnp.inf))
    qb = q.astype(BF16)
    vb = v_ref[...].astype(BF16)
    s = _dot_nt(qb, k.astype(BF16)) * dmat
    inter = jnp.exp(b_c + m0 - m_c)
    c0 = c_ref[0, 0]
    n0 = n_ref[0, 0]
    num = inter * _dot(qb, c0.astype(BF16)) + _dot(s.astype(BF16), vb)
    den = inter * jnp.sum(q * n0, axis=1, keepdims=True) + jnp.sum(s, axis=1, keepdims=True)
    h = num / jnp.maximum(jnp.abs(den), jnp.exp(-m_c))
    hn = h * lax.rsqrt(jnp.mean(h * h, axis=1, keepdims=True) + NORM_EPS)
    y_ref[...] = (hn * ng_ref[...] * jax.nn.sigmoid(og_ref[...])).astype(y_ref.dtype)

    b_last = b_c[L - 1:L, :]
    m_last = m_c[L - 1:L, :]
    kw = k * jnp.exp(b_last - b_c + ig_c - m_last)
    decay = jnp.exp(b_last + m0 - m_last)
    c_ref[0, 0] = decay * c0 + _dot(kw.T.astype(BF16), vb)
    n_ref[0, 0] = decay * n0 + jnp.sum(kw, axis=0, keepdims=True)
    m_ref[0, 0] = jnp.broadcast_to(m_last, (1, LANES))


def _mlstm_prompt(z, g_all, norm_g, bsz, seq):
    L = _tile(seq, (256, 128))
    nc = seq // L
    gp = jnp.transpose(g_all[:, :2 * M_HEADS].reshape(bsz, seq, 2 * M_HEADS), (0, 2, 1))
    g_row = gp.reshape(bsz, 2 * M_HEADS, 1, seq)
    g_col = gp.reshape(bsz, 2 * M_HEADS, seq, 1)
    row = lambda off: pl.BlockSpec((1, 1, 1, L), lambda b, h, c: (b, off + h, 0, c))
    col = lambda off: pl.BlockSpec((1, 1, L, 1), lambda b, h, c: (b, off + h, c, 0))
    state = lambda shape: pl.BlockSpec((1, 1) + shape, lambda b, h, c: (b, h, 0, 0))
    return pl.pallas_call(
        _mlstm_prompt_kernel,
        out_shape=(jax.ShapeDtypeStruct((bsz * seq, M_V_W), BF16),
                   jax.ShapeDtypeStruct((bsz, M_HEADS, M_QK, M_V), F32),
                   jax.ShapeDtypeStruct((bsz, M_HEADS, 1, M_QK), F32),
                   jax.ShapeDtypeStruct((bsz, M_HEADS, 1, LANES), F32)),
        grid=(bsz, M_HEADS, nc),
        in_specs=[pl.BlockSpec((L, M_QK), lambda b, h, c: (b * nc + c, COL_QM // M_QK + h)),
                  pl.BlockSpec((L, M_QK), lambda b, h, c: (b * nc + c, COL_KM // M_QK + h)),
                  pl.BlockSpec((L, M_V), lambda b, h, c: (b * nc + c, COL_VM // M_V + h)),
                  pl.BlockSpec((L, M_V), lambda b, h, c: (b * nc + c, COL_OM // M_V + h)),
                  row(0), row(M_HEADS), col(0), col(M_HEADS),
                  pl.BlockSpec((1, M_V), lambda b, h, c: (0, h))],
        out_specs=(pl.BlockSpec((L, M_V), lambda b, h, c: (b * nc + c, h)),
                   state((M_QK, M_V)), state((1, M_QK)), state((1, LANES))),
        compiler_params=_params(("parallel", "parallel", "arbitrary"), 32),
        name="mlstm_prompt",
    )(z, z, z, z, g_row, g_row, g_col, g_col, norm_g)


def _mlstm_sample_kernel(qr_ref, kr_ref, vr_ref, or_ref, qc_ref, kc_ref, g_ref, m0_ref, n0_ref, c0_ref, ng_ref,
                         y_ref, c_ref, n_ref, m_ref):
    g = g_ref[0]
    m0_all = m0_ref[0]
    lane = lax.broadcasted_iota(I32, (1, LANES), 1)
    m_out = jnp.zeros((1, LANES), F32)
    for h in range(M_HEADS):
        ig = g[:, h:h + 1]
        lf = _log_sigmoid(g[:, M_HEADS + h:M_HEADS + h + 1])
        m0 = m0_all[:, h:h + 1]
        m = jnp.maximum(lf + m0, ig)
        w = jnp.exp(ig - m)
        decay = jnp.exp(lf + m0 - m)
        q_r = qr_ref[0, h] * M_Q_SCALE
        k_r = kr_ref[0, h]
        v_r = vr_ref[0, h]
        q_c = qc_ref[0, h] * M_Q_SCALE
        c0 = c0_ref[0, 0, h]
        n0 = n0_ref[0, 0, h]
        s = jnp.sum(q_r * k_r, axis=1, keepdims=True) * w
        num = decay * jnp.sum(q_c * c0, axis=0, keepdims=True) + s * v_r
        den = decay * jnp.sum(q_r * n0, axis=1, keepdims=True) + s
        hh = num / jnp.maximum(jnp.abs(den), jnp.exp(-m))
        hn = hh * lax.rsqrt(jnp.mean(hh * hh, axis=1, keepdims=True) + NORM_EPS)
        y = hn * ng_ref[:, h * M_V:(h + 1) * M_V] * jax.nn.sigmoid(or_ref[0, h])
        y_ref[0, :, h * M_V:(h + 1) * M_V] = y.astype(y_ref.dtype)
        c_ref[0, h] = decay * c0 + (w * kc_ref[0, h]) * v_r
        n_ref[0, h] = decay * n0 + w * k_r
        m_out = jnp.where(lane == h, m, m_out)
    m_ref[0] = m_out


def _mlstm_sample(zs, gs, state_c, state_n, state_m, norm_g, layer):
    ns = zs.shape[0]
    qm = zs[:, COL_QM:COL_QM + M_QK_W]
    km = zs[:, COL_KM:COL_KM + M_QK_W]
    rows = lambda a, w: a.reshape(ns, M_HEADS, 1, w)
    m0 = jnp.pad(state_m[layer], ((0, 0), (0, LANES - M_HEADS))).reshape(ns, 1, LANES)
    row_spec = lambda w: pl.BlockSpec((1, M_HEADS, 1, w), lambda s: (s, 0, 0, 0))
    lane_spec = pl.BlockSpec((1, 1, LANES), lambda s: (s, 0, 0))
    col_spec = pl.BlockSpec((1, M_HEADS, M_QK, 1), lambda s: (s, 0, 0, 0))
    return pl.pallas_call(
        _mlstm_sample_kernel,
        out_shape=(jax.ShapeDtypeStruct((ns, 1, M_V_W), BF16),
                   jax.ShapeDtypeStruct((ns, M_HEADS, M_QK, M_V), F32),
                   jax.ShapeDtypeStruct((ns, M_HEADS, 1, M_QK), F32),
                   jax.ShapeDtypeStruct((ns, 1, LANES), F32)),
        grid=(ns,),
        in_specs=[row_spec(M_QK), row_spec(M_QK), row_spec(M_V), row_spec(M_V), col_spec, col_spec,
                  lane_spec, lane_spec,
                  pl.BlockSpec((1, 1, M_HEADS, 1, M_QK), lambda s: (layer, s, 0, 0, 0)),
                  pl.BlockSpec((1, 1, M_HEADS, M_QK, M_V), lambda s: (layer, s, 0, 0, 0)),
                  pl.BlockSpec((1, M_V_W), lambda s: (0, 0))],
        out_specs=(pl.BlockSpec((1, 1, M_V_W), lambda s: (s, 0, 0)),
                   pl.BlockSpec((1, M_HEADS, M_QK, M_V), lambda s: (s, 0, 0, 0)),
                   row_spec(M_QK), lane_spec),
        compiler_params=_params(("parallel",), 32),
        name="mlstm_sample",
    )(rows(qm, M_QK), rows(km, M_QK), rows(zs[:, COL_VM:COL_VM + M_V_W], M_V), rows(zs[:, COL_OM:COL_OM + M_V_W], M_V),
      qm.reshape(ns, M_HEADS, M_QK, 1), km.reshape(ns, M_HEADS, M_QK, 1),
      gs.reshape(ns, 1, LANES), m0,
      state_n.reshape(state_n.shape[0], ns, M_HEADS, 1, M_QK), state_c, norm_g)


def _layernorm_chunks(chunks, d):
    total = sum(jnp.sum(c, axis=1, keepdims=True) for c in chunks)
    mu = total * (1.0 / d)
    var = sum(jnp.sum(jnp.square(c - mu), axis=1, keepdims=True) for c in chunks) * (1.0 / d)
    return mu, lax.rsqrt(var + LN_EPS)


def _merge_kernel(alpha, *refs):
    att_ref, mix_ref, wa_ref, wb_ref, x_ref, g_ref, b_ref, wr_ref, br_ref = refs[:9]
    h_ref, ids_ref, wts_ref = refs[-3:]
    j = pl.program_id(1)
    tn = x_ref.shape[1]
    d = h_ref.shape[1]
    nj = d // tn
    cols = lambda c: slice(c * tn, (c + 1) * tn)
    pre = alpha * x_ref[...] + _dot(att_ref[...], wa_ref[...]) + _dot(mix_ref[...], wb_ref[...])
    for c in range(nj):
        @pl.when(j == c)
        def _(c=c):
            h_ref[:, cols(c)] = pre

    @pl.when(j == nj - 1)
    def _():
        mu, rstd = _layernorm_chunks([h_ref[:, cols(c)] for c in range(nj)], d)
        lg = br_ref[...]
        for c in range(nj):
            y = (h_ref[:, cols(c)] - mu) * rstd * g_ref[:, cols(c)] + b_ref[:, cols(c)]
            h_ref[:, cols(c)] = y
            lg = lg + _dot(y.astype(BF16), wr_ref[cols(c), :])
        lane = lax.broadcasted_iota(I32, lg.shape, 1)
        lane_f = lane.astype(F32)
        far = float(LANES)
        gl = jnp.where(lane < N_GROUPS, lg, -jnp.inf)
        gmax = jnp.max(gl, axis=1, keepdims=True)
        gsel = jnp.min(jnp.where(gl == gmax, lane_f, far), axis=1, keepdims=True).astype(I32)
        gw = 1.0 / jnp.sum(jnp.where(lane < N_GROUPS, jnp.exp(gl - gmax), 0.0), axis=1, keepdims=True)
        e_lane = lane - N_GROUPS
        in_grp = (e_lane >= 0) & (e_lane < N_EXPERTS) & ((e_lane // EXPERTS_PER_GROUP) == gsel)
        el = jnp.where(in_grp, lg, -jnp.inf)
        t1 = jnp.max(el, axis=1, keepdims=True)
        i1 = jnp.min(jnp.where(el == t1, lane_f, far), axis=1, keepdims=True)
        el2 = jnp.where(lane_f == i1, -jnp.inf, el)
        t2 = jnp.max(el2, axis=1, keepdims=True)
        i2 = jnp.min(jnp.where(el2 == t2, lane_f, far), axis=1, keepdims=True)
        e21 = jnp.exp(t2 - t1)
        w1 = gw / (1.0 + e21)
        w2 = w1 * e21
        ids = jnp.where(lane == 0, i1, jnp.where(lane == 1, i2, float(N_GROUPS))) - float(N_GROUPS)
        ids_ref[...] = ids.astype(I32)
        wts_ref[...] = jnp.where(lane == 0, w1, jnp.where(lane == 1, w2, 0.0))


def _merge(att, mix, wob, x, ln_g, ln_b, wr, br, alpha, n_total, row0, prev=None):
    r, d = x.shape
    half = att.shape[1]
    tm = _tile(r, (512, 256, 128, 64))
    assert row0 % tm == 0
    o0 = row0 // tm
    tn = _tile(d, (1024, 512))
    rows = lambda w: pl.BlockSpec((1, w), lambda i, j: (0, 0))
    prev = () if prev is None else tuple(prev)
    n_in = 9
    return pl.pallas_call(
        functools.partial(_merge_kernel, alpha),
        out_shape=(jax.ShapeDtypeStruct((n_total, d), F32),
                   jax.ShapeDtypeStruct((n_total, LANES), I32),
                   jax.ShapeDtypeStruct((n_total, LANES), F32)),
        grid=(r // tm, d // tn),
        in_specs=[pl.BlockSpec((tm, half), lambda i, j: (i, 0)),
                  pl.BlockSpec((tm, half), lambda i, j: (i, 0)),
                  pl.BlockSpec((half, tn), lambda i, j: (0, j)),
                  pl.BlockSpec((half, tn), lambda i, j: (1, j)),
                  pl.BlockSpec((tm, tn), lambda i, j: (i, j)),
                  rows(d), rows(d),
                  pl.BlockSpec((d, LANES), lambda i, j: (0, 0)),
                  rows(LANES)] + [pl.BlockSpec(memory_space=pl.ANY)] * len(prev),
        out_specs=(pl.BlockSpec((tm, d), lambda i, j: (i + o0, 0)),
                   pl.BlockSpec((tm, LANES), lambda i, j: (i + o0, 0)),
                   pl.BlockSpec((tm, LANES), lambda i, j: (i + o0, 0))),
        input_output_aliases={n_in + k: k for k in range(len(prev))},
        compiler_params=_params(("parallel", "arbitrary"), 52),
        name="merge_ln_router",
    )(att, mix, wob, wob, x, ln_g, ln_b, wr, br, *prev)


def _moe_kernel(te_ref, ti_ref, nu_ref, src_ref, h_ref, wg_ref, wu_ref, wd_ref, o_ref,
                xf_s, xb_s, hg_s, hu_s, a_s, sem):
    del te_ref, ti_ref
    t = pl.program_id(0)
    ph = pl.program_id(1)
    nk, tm, kc = xb_s.shape
    n_phase = nk + xf_s.shape[2] // o_ref.shape[1]
    share = tm // n_phase
    assert share * n_phase == tm
    nu = nu_ref[0]

    def start_rows(tile, lo, count):
        slot = tile % 2

        def body(g, carry):
            base = pl.multiple_of(lo + g * SUBLANES, SUBLANES)
            for u in range(SUBLANES):
                pltpu.make_async_copy(h_ref.at[pl.ds(src_ref[tile * tm + base + u], 1), :],
                                      xf_s.at[slot, pl.ds(base + u, 1), :],
                                      sem.at[slot]).start(priority=GATHER_PRIORITY)
            return carry
        lax.fori_loop(0, count // SUBLANES, body, 0)

    def wait_tile(tile):
        slot = tile % 2
        pltpu.make_async_copy(h_ref.at[pl.ds(0, tm), :], xf_s.at[slot], sem.at[slot]).wait()

    @pl.when(t < nu)
    def _():
        @pl.when(t + 1 < nu)
        def _():
            start_rows(t + 1, ph * share, share)

        @pl.when(ph == 0)
        def _():
            @pl.when(t == 0)
            def _():
                start_rows(t, 0, tm)

            wait_tile(t)
            for c in range(nk):
                xb_s[c] = xf_s[t % 2, :, c * kc:(c + 1) * kc].astype(BF16)

        @pl.when(ph < nk)
        def _():
            x = xb_s[ph]
            pg = _dot(x, wg_ref[0, 0].astype(BF16))
            pu = _dot(x, wu_ref[0, 0].astype(BF16))

            @pl.when(ph == 0)
            def _():
                hg_s[...] = pg
                hu_s[...] = pu

            @pl.when(ph != 0)
            def _():
                hg_s[...] += pg
                hu_s[...] += pu

        @pl.when(ph == nk - 1)
        def _():
            hg = hg_s[...]
            a_s[...] = (hg * jax.nn.sigmoid(hg) * hu_s[...]).astype(BF16)

        @pl.when(ph >= nk)
        def _():
            o_ref[...] = _dot(a_s[...], wd_ref[0, 0].astype(BF16))


def _moe_experts(h, src, w_gate, w_up, w_down, tile_expert, tile_index, n_used, layer):
    tm = MOE_TILE
    n_tiles = tile_expert.shape[0]
    d = w_gate.shape[2]
    f = w_gate.shape[3]
    kc = nc = MOE_CHUNK
    nk = d // kc
    nn = d // nc

    def used(t, nu, a, b):
        return jnp.where(t < nu[0], a, b)

    def k_idx(t, ph, nu):
        return used(t, nu, jnp.minimum(ph, nk - 1), nk - 1)

    def n_idx(t, ph, nu):
        return used(t, nu, jnp.maximum(ph - nk, 0), nn - 1)

    grid_spec = pltpu.PrefetchScalarGridSpec(
        num_scalar_prefetch=4, grid=(n_tiles, nk + nn),
        in_specs=[pl.BlockSpec(memory_space=pl.ANY),
                  pl.BlockSpec((1, 1, kc, f), lambda t, ph, te, ti, nu, src: (layer, te[t], k_idx(t, ph, nu), 0)),
                  pl.BlockSpec((1, 1, kc, f), lambda t, ph, te, ti, nu, src: (layer, te[t], k_idx(t, ph, nu), 0)),
                  pl.BlockSpec((1, 1, f, nc), lambda t, ph, te, ti, nu, src: (layer, te[t], 0, n_idx(t, ph, nu)))],
        out_specs=pl.BlockSpec((tm, nc), lambda t, ph, te, ti, nu, src: (ti[t], n_idx(t, ph, nu))),
        scratch_shapes=[pltpu.VMEM((2, tm, d), F32), pltpu.VMEM((nk, tm, kc), BF16),
                        pltpu.VMEM((tm, f), F32), pltpu.VMEM((tm, f), F32), pltpu.VMEM((tm, f), BF16),
                        pltpu.SemaphoreType.DMA((2,))])
    return pl.pallas_call(
        _moe_kernel,
        out_shape=jax.ShapeDtypeStruct((n_tiles * tm, d), F32),
        grid_spec=grid_spec,
        compiler_params=_params(("arbitrary", "arbitrary"), 52),
        name="moe_experts",
    )(tile_expert, tile_index, n_used, src, h, w_gate, w_up, w_down)


def _combine_kernel(alpha, row0, n_tok, pos_ref, h_ref, w_ref, g_ref, b_ref, ys_ref, o_ref, f_s, sem):
    i = pl.program_id(0)
    tm, d = o_ref.shape

    def start_tile(tile):
        slot = tile % 2

        def body(g, carry):
            base = pl.multiple_of(g * SUBLANES, SUBLANES)
            for u in range(SUBLANES):
                for s in range(2):
                    pltpu.make_async_copy(
                        ys_ref.at[pl.ds(pos_ref[s * n_tok + row0 + tile * tm + base + u], 1), :],
                        f_s.at[slot, s, pl.ds(base + u, 1), :], sem.at[slot]).start(priority=GATHER_PRIORITY)
            return carry
        lax.fori_loop(0, tm // SUBLANES, body, 0)

    def wait_tile(tile):
        slot = tile % 2
        for s in range(2):
            pltpu.make_async_copy(ys_ref.at[pl.ds(0, tm), :], f_s.at[slot, s], sem.at[slot]).wait()

    @pl.when(i == 0)
    def _():
        start_tile(i)

    @pl.when(i + 1 < pl.num_programs(0))
    def _():
        start_tile(i + 1)

    wait_tile(i)
    w = w_ref[...]
    v = alpha * h_ref[...] + w[:, 0:1] * f_s[i % 2, 0] + w[:, 1:2] * f_s[i % 2, 1]
    mu = jnp.mean(v, axis=1, keepdims=True)
    vc = v - mu
    var = jnp.sum(vc * vc, axis=1, keepdims=True) * (1.0 / d)
    o_ref[...] = vc * lax.rsqrt(var + LN_EPS) * g_ref[...] + b_ref[...]


def _combine(h, ys, pos, wts, ln_g, ln_b, alpha, row0, nrows):
    n, d = h.shape
    tm = next(c for c in (128, 64, 32, 16, 8) if nrows % c == 0 and row0 % c == 0)
    o0 = row0 // tm
    rows = pl.BlockSpec((1, d), lambda i, pos: (0, 0))
    grid_spec = pltpu.PrefetchScalarGridSpec(
        num_scalar_prefetch=1, grid=(nrows // tm,),
        in_specs=[pl.BlockSpec((tm, d), lambda i, pos: (i + o0, 0)),
                  pl.BlockSpec((tm, LANES), lambda i, pos: (i + o0, 0)), rows, rows,
                  pl.BlockSpec(memory_space=pl.ANY)],
        out_specs=pl.BlockSpec((tm, d), lambda i, pos: (i, 0)),
        scratch_shapes=[pltpu.VMEM((2, 2, tm, d), F32), pltpu.SemaphoreType.DMA((2,))])
    return pl.pallas_call(
        functools.partial(_combine_kernel, alpha, row0, n),
        out_shape=jax.ShapeDtypeStruct((nrows, d), F32),
        grid_spec=grid_spec,
        compiler_params=_params(("arbitrary",), 32),
        name="combine_ln",
    )(pos, h, wts, ln_g, ln_b, ys)


def _routing_tables(ids, n_tiles):
    n = ids.shape[0]
    tm = MOE_TILE
    e = ids[:, :2].reshape(-1)
    onehot = (e[:, None] == jnp.arange(N_EXPERTS, dtype=I32)[None, :]).astype(I32)
    csum = jnp.cumsum(onehot, axis=0)
    rank = jnp.sum((csum - onehot) * onehot, axis=1)
    counts = csum[-1]
    tiles = (counts + tm - 1) // tm
    tile_end = jnp.cumsum(tiles)
    n_used = tile_end[-1]
    pos = ((tile_end - tiles)[e] * tm + rank).astype(I32)
    src = (jnp.arange(n_tiles * tm, dtype=I32) % n).at[pos].set(jnp.arange(2 * n, dtype=I32) // 2)
    t = jnp.minimum(jnp.arange(n_tiles, dtype=I32), n_used - 1)
    tile_expert = jnp.minimum(jnp.sum((tile_end[None, :] <= t[:, None]).astype(I32), axis=1), N_EXPERTS - 1)
    pos2 = pos.reshape(n, 2)
    back = jnp.concatenate([pos2[:, 0], pos2[:, 1]])
    return src, back, tile_expert, t, n_used.reshape(1).astype(I32)


def _layer(l, x_p, x_s, bsz, seq, cache_k, cache_v, state_c, state_n, state_m, page_table,
           w_in, b_gates, norm_g, w_out, ln1_g, ln1_b, w_group, b_group, w_router, b_router,
           w_e_gate, w_e_up, w_e_down, ln2_g, ln2_b, alpha):
    n_p, d = x_p.shape
    n_s = x_s.shape[0]
    n = n_p + n_s

    wt = jnp.transpose(w_in[l]).astype(BF16)
    wg = jnp.pad(w_in[l, :, COL_G:], ((0, 0), (0, LANES - 2 * M_HEADS)))
    bg = jnp.pad(b_gates[l], (0, LANES - 2 * M_HEADS)).reshape(1, LANES)
    z_p = _in_proj(x_p.astype(BF16), wt, COL_G)
    z_s = _in_proj(x_s.astype(BF16), wt, COL_G)
    g_p = _gates(x_p, wg, bg)
    g_s = _gates(x_s, wg, bg)

    ng = norm_g[l].reshape(1, M_V_W)
    att_p = _moba_prompt(z_p, bsz, seq)
    mix_p, c_p, n_pr, m_p = _mlstm_prompt(z_p, g_p, ng, bsz, seq)
    att_s = _moba_sample(z_s, cache_k, cache_v, page_table, l)
    mix_s, c_s, n_sm, m_s = _mlstm_sample(z_s, g_s, state_c, state_n, state_m, ng, l)

    wr = jnp.pad(jnp.concatenate([w_group[l], w_router[l]], axis=1),
                 ((0, 0), (0, LANES - N_GROUPS - N_EXPERTS))).astype(BF16)
    br = jnp.pad(jnp.concatenate([b_group[l], b_router[l]]), (0, LANES - N_GROUPS - N_EXPERTS)).reshape(1, LANES)
    wob = w_out[l].astype(BF16)
    g1, b1 = ln1_g[l].reshape(1, d), ln1_b[l].reshape(1, d)
    done = _merge(att_p, mix_p, wob, x_p, g1, b1, wr, br, alpha, n, 0)
    h2, ids, wts = _merge(att_s, mix_s.reshape(n_s, M_V_W), wob, x_s, g1, b1, wr, br, alpha, n, n_p, prev=done)

    n_tiles = (2 * n) // MOE_TILE + N_EXPERTS + 1
    src, back, tile_expert, tile_index, n_used = _routing_tables(ids, n_tiles)
    ys = _moe_experts(h2, src, w_e_gate, w_e_up, w_e_down, tile_expert, tile_index, n_used, l)
    g2, b2 = ln2_g[l].reshape(1, d), ln2_b[l].reshape(1, d)
    y_p = _combine(h2, ys, back, wts, g2, b2, alpha, 0, n_p)
    y_s = _combine(h2, ys, back, wts, g2, b2, alpha, n_p, n_s)

    kv = lambda rows, col: rows[:, col:col + ATT_KV_W]
    return (y_p, y_s,
            kv(z_p, COL_K).reshape(bsz, seq, ATT_KV_HEADS, HEAD_DIM),
            kv(z_p, COL_V).reshape(bsz, seq, ATT_KV_HEADS, HEAD_DIM),
            kv(z_s, COL_K).reshape(n_s, 1, ATT_KV_HEADS, HEAD_DIM),
            kv(z_s, COL_V).reshape(n_s, 1, ATT_KV_HEADS, HEAD_DIM),
            c_p, n_pr.reshape(bsz, M_HEADS, M_QK), m_p[:, :, 0, 0],
            c_s, n_sm.reshape(n_s, M_HEADS, M_QK), m_s[:, 0, :M_HEADS])


def kernel(x_prompt, x_sample, cache_k, cache_v, state_mlstm_c, state_mlstm_n, state_mlstm_m, page_table,
           w_in, b_gates, mlstm_norm_g, w_out, ln1_g, ln1_b, w_group, b_group, w_router, b_router,
           w_e_gate, w_e_up, w_e_down, ln2_g, ln2_b):
    bsz, seq, d = x_prompt.shape
    n_s, dec_seq, _ = x_sample.shape
    depth = w_in.shape[0]
    assert dec_seq == 1 and seq % MOBA_BLOCK == 0 and w_in.shape[2] == COL_G + 2 * M_HEADS
    assert (page_table.shape[1] * cache_k.shape[2]) % MOBA_BLOCK == 0 and MOBA_BLOCK % cache_k.shape[2] == 0
    alpha = (2 * depth) ** 0.25
    x_p = x_prompt.reshape(bsz * seq, d)
    x_s = x_sample.reshape(n_s, d)
    per_layer = []
    for l in range(depth):
        outs = _layer(l, x_p, x_s, bsz, seq, cache_k, cache_v, state_mlstm_c, state_mlstm_n, state_mlstm_m,
                      page_table, w_in, b_gates, mlstm_norm_g, w_out, ln1_g, ln1_b, w_group, b_group, w_router,
                      b_router, w_e_gate, w_e_up, w_e_down, ln2_g, ln2_b, alpha)
        x_p, x_s = outs[0], outs[1]
        per_layer.append(outs[2:])
    stacked = [jnp.stack([p[i] for p in per_layer]) for i in range(10)]
    return (x_p.reshape(bsz, seq, d), x_s.reshape(n_s, 1, d), *stacked)
```

```python
import functools

import jax
import jax.numpy as jnp
from jax import lax
from jax.experimental import pallas as pl
from jax.experimental.pallas import tpu as pltpu

F32, BF16, I32 = jnp.float32, jnp.bfloat16, jnp.int32

SUBLANES = 8
LANES = 128
MIB = 1024 * 1024

HEAD_DIM = 128
ATT_HEADS = 16
ATT_KV_HEADS = 8
ATT_GROUP = ATT_HEADS // ATT_KV_HEADS
ATT_SCALE = HEAD_DIM ** -0.5
LOG2_E = 1.4426950408889634
MOBA_BLOCK = 256
MOBA_TOPK = 3
M_HEADS = 4
M_QK = 256
M_V = 512
M_Q_SCALE = M_QK ** -0.5
N_GROUPS = 4
EXPERTS_PER_GROUP = 8
N_EXPERTS = N_GROUPS * EXPERTS_PER_GROUP
LN_EPS = 1e-5
NORM_EPS = 1e-6

ATT_Q_W = ATT_HEADS * HEAD_DIM
ATT_KV_W = ATT_KV_HEADS * HEAD_DIM
M_QK_W = M_HEADS * M_QK
M_V_W = M_HEADS * M_V
COL_Q = 0
COL_K = COL_Q + ATT_Q_W
COL_V = COL_K + ATT_KV_W
COL_QM = COL_V + ATT_KV_W
COL_KM = COL_QM + M_QK_W
COL_VM = COL_KM + M_QK_W
COL_OM = COL_VM + M_V_W
COL_G = COL_OM + M_V_W

MOE_TILE = 640
MOE_CHUNK = 1024
GATHER_PRIORITY = 1


def _params(sem, vmem_mib):
    return pltpu.CompilerParams(dimension_semantics=sem, vmem_limit_bytes=vmem_mib * MIB)


def _dot(a, b):
    return jnp.dot(a, b, preferred_element_type=F32)


def _dot_nt(a, b):
    return lax.dot_general(a, b, (((1,), (1,)), ((), ())), preferred_element_type=F32)


def _log_sigmoid(x):
    return jnp.minimum(x, 0.0) - jnp.log1p(jnp.exp(-jnp.abs(x)))


def _tile(n, candidates):
    for c in candidates:
        if n % c == 0:
            return c
    raise ValueError(f"no tile for {n} in {candidates}")


def _matmul_nt_kernel(x_ref, w_ref, o_ref):
    o_ref[...] = _dot_nt(x_ref[...], w_ref[...])


def _in_proj(xb, wt, w):
    n, d = xb.shape
    tm = _tile(n, (1024, 832, 640, 512, 256, 128, 64))
    tn = _tile(w, (1024, 512, 256, 128))
    return pl.pallas_call(
        _matmul_nt_kernel,
        out_shape=jax.ShapeDtypeStruct((n, w), F32),
        grid=(w // tn, n // tm),
        in_specs=[pl.BlockSpec((tm, d), lambda j, i: (i, 0)),
                  pl.BlockSpec((tn, d), lambda j, i: (j, 0))],
        out_specs=pl.BlockSpec((tm, tn), lambda j, i: (i, j)),
        compiler_params=_params(("parallel", "parallel"), 48),
        name="in_proj",
    )(xb, wt)


def _gates_kernel(x_ref, w_ref, b_ref, o_ref):
    x = x_ref[...]
    w = w_ref[...]
    xh = x.astype(BF16)
    xl = (x - xh.astype(F32)).astype(BF16)
    wh = w.astype(BF16)
    wl = (w - wh.astype(F32)).astype(BF16)
    o_ref[...] = _dot(xh, wh) + _dot(xl, wh) + _dot(xh, wl) + b_ref[...]


def _gates(x, wg, bg):
    n, d = x.shape
    tm = _tile(n, (512, 416, 320, 256, 128, 64))
    return pl.pallas_call(
        _gates_kernel,
        out_shape=jax.ShapeDtypeStruct((n, LANES), F32),
        grid=(n // tm,),
        in_specs=[pl.BlockSpec((tm, d), lambda i: (i, 0)),
                  pl.BlockSpec((d, LANES), lambda i: (0, 0)),
                  pl.BlockSpec((1, LANES), lambda i: (0, 0))],
        out_specs=pl.BlockSpec((tm, LANES), lambda i: (i, 0)),
        compiler_params=_params(("parallel",), 40),
        name="gates",
    )(x, wg, bg)


def _moba_prompt_kernel(q_ref, k_ref, v_ref, o_ref, kb_s, vt_s, km_s):
    i = pl.program_id(2)
    nb = kb_s.shape[0]
    blk = MOBA_BLOCK
    nq = ATT_GROUP * blk

    @pl.when(i == 0)
    def _():
        for j in range(nb):
            kj = k_ref[j * blk:(j + 1) * blk, :]
            kb_s[j] = kj.astype(BF16)
            km_s[j:j + 1, :] = jnp.mean(kj, axis=0, keepdims=True)
            vt_s[j] = v_ref[j * blk:(j + 1) * blk, :].T.astype(BF16)
        km_s[nb:, :] = jnp.zeros((km_s.shape[0] - nb, HEAD_DIM), F32)

    q = q_ref[...]
    q2 = jnp.concatenate([q[:, g * HEAD_DIM:(g + 1) * HEAD_DIM] for g in range(ATT_GROUP)], axis=0).astype(BF16)

    gate = _dot_nt(km_s[...].astype(BF16), q2)
    jj = lax.broadcasted_iota(I32, gate.shape, 0)
    g = jnp.where(jj < i, gate, -jnp.inf)
    sel = jnp.zeros(gate.shape, F32)
    for _ in range(MOBA_TOPK):
        mx = jnp.max(g, axis=0, keepdims=True)
        first = jnp.min(jnp.where(g == mx, jj, 2 * nb), axis=0, keepdims=True)
        pick = (jj == first) & (mx > -jnp.inf)
        sel = jnp.where(pick, 1.0, sel)
        g = jnp.where(pick, -jnp.inf, g)
    ri = lax.broadcasted_iota(I32, (blk, nq), 0)
    ci = lax.broadcasted_iota(I32, (blk, nq), 1)
    causal = ri <= (ci & (blk - 1))
    c2 = ATT_SCALE * LOG2_E

    def attend_all(own):
        m = l = acc = None
        for j in (own,) + tuple(range(own)):
            s = _dot_nt(kb_s[j], q2)
            s = jnp.where(causal if j == own else sel[j:j + 1, :] > 0.5, s, -jnp.inf)
            m_blk = jnp.max(s, axis=0, keepdims=True)
            m_new = m_blk if m is None else jnp.maximum(m, m_blk)
            p = jnp.exp2((s - m_new) * c2)
            pv = _dot(vt_s[j], p.astype(BF16))
            if m is None:
                l, acc = jnp.sum(p, axis=0, keepdims=True), pv
            else:
                alpha = jnp.exp2((m - m_new) * c2)
                l = alpha * l + jnp.sum(p, axis=0, keepdims=True)
                acc = alpha * acc + pv
            m = m_new
        out = (acc / l).T
        for g_i in range(ATT_GROUP):
            o_ref[:, g_i * HEAD_DIM:(g_i + 1) * HEAD_DIM] = out[g_i * blk:(g_i + 1) * blk, :].astype(o_ref.dtype)

    for own in range(nb):
        pl.when(i == own)(functools.partial(attend_all, own))


def _moba_prompt(z, bsz, seq):
    nb = seq // MOBA_BLOCK
    qw = ATT_GROUP * HEAD_DIM
    return pl.pallas_call(
        _moba_prompt_kernel,
        out_shape=jax.ShapeDtypeStruct((bsz * seq, ATT_Q_W), BF16),
        grid=(bsz, ATT_KV_HEADS, nb),
        in_specs=[pl.BlockSpec((MOBA_BLOCK, qw), lambda b, h, i: (b * nb + i, COL_Q // qw + h)),
                  pl.BlockSpec((seq, HEAD_DIM), lambda b, h, i: (b, COL_K // HEAD_DIM + h)),
                  pl.BlockSpec((seq, HEAD_DIM), lambda b, h, i: (b, COL_V // HEAD_DIM + h))],
        out_specs=pl.BlockSpec((MOBA_BLOCK, qw), lambda b, h, i: (b * nb + i, h)),
        scratch_shapes=[pltpu.VMEM((nb, MOBA_BLOCK, HEAD_DIM), BF16),
                        pltpu.VMEM((nb, HEAD_DIM, MOBA_BLOCK), BF16),
                        pltpu.VMEM((2 * SUBLANES, HEAD_DIM), F32)],
        compiler_params=_params(("parallel", "parallel", "arbitrary"), 32),
        name="moba_prompt",
    )(z, z, z)


def _moba_sample_kernel(n_pages, pt_ref, q_ref, kn_ref, vn_ref, *rest):
    del pt_ref
    kp = rest[:n_pages]
    vp = rest[n_pages:2 * n_pages]
    o_ref = rest[2 * n_pages]
    page = kp[0].shape[2]
    per_blk = MOBA_BLOCK // page
    nb = n_pages // per_blk
    nrow = page * ATT_KV_HEADS
    both = lambda a: jnp.concatenate([a] * ATT_GROUP, axis=0)

    q = q_ref[0]
    qb = q.astype(BF16)
    ri = lax.broadcasted_iota(I32, (ATT_HEADS, nrow), 0)
    ci = lax.broadcasted_iota(I32, (ATT_HEADS, nrow), 1)
    own = (ci % ATT_KV_HEADS) == (ri % ATT_KV_HEADS)

    scores, ksum = [], []
    for p in range(n_pages):
        k3 = kp[p][0, 0]
        scores.append(_dot_nt(qb, k3.reshape(nrow, HEAD_DIM).astype(BF16)) * ATT_SCALE)
        ksum.append(jnp.sum(k3, axis=0))
    gates = []
    for j in range(nb):
        kmean = sum(ksum[j * per_blk:(j + 1) * per_blk]) * (1.0 / MOBA_BLOCK)
        gates.append(jnp.sum(q * both(kmean), axis=1, keepdims=True))
    sel = []
    for j in range(nb):
        rank = jnp.zeros(gates[j].shape, F32)
        for jp in range(nb):
            if jp < j:
                rank = rank + (gates[jp] >= gates[j]).astype(F32)
            elif jp > j:
                rank = rank + (gates[jp] > gates[j]).astype(F32)
        sel.append(rank < MOBA_TOPK)

    s_new = jnp.sum(q * both(kn_ref[0]), axis=1, keepdims=True) * ATT_SCALE
    m = s_new
    for p in range(n_pages):
        scores[p] = jnp.where(sel[p // per_blk], jnp.where(own, scores[p], -jnp.inf), -jnp.inf)
        m = jnp.maximum(m, jnp.max(scores[p], axis=1, keepdims=True))
    p_new = jnp.exp(s_new - m)
    l = p_new
    acc = p_new * both(vn_ref[0])
    for p in range(n_pages):
        pp = jnp.exp(scores[p] - m)
        l = l + jnp.sum(pp, axis=1, keepdims=True)
        acc = acc + _dot(pp.astype(BF16), vp[p][0, 0].reshape(nrow, HEAD_DIM).astype(BF16))
    o_ref[0] = (acc / l).astype(o_ref.dtype)


def _moba_sample(zs, cache_k, cache_v, page_table, layer):
    ns = zs.shape[0]
    n_pages = page_table.shape[1]
    page = cache_k.shape[2]
    pt_flat = page_table.reshape(-1).astype(I32)
    qg = jnp.transpose(zs[:, COL_Q:COL_Q + ATT_Q_W].reshape(ns, ATT_KV_HEADS, ATT_GROUP, HEAD_DIM), (0, 2, 1, 3))
    kv_new = lambda col: zs[:, col:col + ATT_KV_W].reshape(ns, ATT_KV_HEADS, HEAD_DIM)

    def page_spec(p):
        return pl.BlockSpec((1, 1, page, ATT_KV_HEADS, HEAD_DIM),
                            lambda s, pt: (layer, pt[s * n_pages + p], 0, 0, 0))

    rows = lambda r: pl.BlockSpec((1, r, HEAD_DIM), lambda s, pt: (s, 0, 0))
    grid_spec = pltpu.PrefetchScalarGridSpec(
        num_scalar_prefetch=1, grid=(ns,),
        in_specs=[rows(ATT_HEADS), rows(ATT_KV_HEADS), rows(ATT_KV_HEADS)] + [page_spec(p) for p in range(n_pages)] * 2,
        out_specs=rows(ATT_HEADS))
    out = pl.pallas_call(
        functools.partial(_moba_sample_kernel, n_pages),
        out_shape=jax.ShapeDtypeStruct((ns, ATT_HEADS, HEAD_DIM), BF16),
        grid_spec=grid_spec,
        compiler_params=_params(("parallel",), 48),
        name="moba_sample",
    )(pt_flat, qg.reshape(ns, ATT_HEADS, HEAD_DIM), kv_new(COL_K), kv_new(COL_V),
      *([cache_k] * n_pages), *([cache_v] * n_pages))
    return jnp.transpose(out.reshape(ns, ATT_GROUP, ATT_KV_HEADS, HEAD_DIM), (0, 2, 1, 3)).reshape(ns, ATT_Q_W)


def _mlstm_prompt_kernel(q_ref, k_ref, v_ref, og_ref, igr_ref, fgr_ref, igc_ref, fgc_ref, ng_ref,
                         y_ref, c_ref, n_ref, m_ref):
    @pl.when(pl.program_id(2) == 0)
    def _():
        c_ref[...] = jnp.zeros(c_ref.shape, F32)
        n_ref[...] = jnp.zeros(n_ref.shape, F32)
        m_ref[...] = jnp.zeros(m_ref.shape, F32)

    L = q_ref.shape[0]
    q = q_ref[...] * M_Q_SCALE
    k = k_ref[...]
    ig_r = igr_ref[0, 0]
    lf_r = _log_sigmoid(fgr_ref[0, 0])
    ig_c = igc_ref[0, 0]
    lf_c = _log_sigmoid(fgc_ref[0, 0])
    ri = lax.broadcasted_iota(I32, (L, L), 0)
    ci = lax.broadcasted_iota(I32, (L, L), 1)
    tril = ci <= ri
    b_c = jnp.sum(jnp.where(tril, lf_r, 0.0), axis=1, keepdims=True)
    b_r = jnp.sum(jnp.where(ri <= ci, lf_c, 0.0), axis=0, keepdims=True)
    r_r = ig_r - b_r
    m0 = m_ref[0, 0][:, 0:1]
    m_c = b_c + jnp.maximum(m0, jnp.max(jnp.where(tril, r_r, -jnp.inf), axis=1, keepdims=True))
    dmat = jnp.exp(jnp.where(tril, b_c - m_c + r_r, -jnp.inf))
    qb = q.astype(BF16)
    vb = v_ref[...].astype(BF16)
    s = _dot_nt(qb, k.astype(BF16)) * dmat
    inter = jnp.exp(b_c + m0 - m_c)
    c0 = c_ref[0, 0]
    n0 = n_ref[0, 0]
    num = inter * _dot(qb, c0.astype(BF16)) + _dot(s.astype(BF16), vb)
    den = inter * jnp.sum(q * n0, axis=1, keepdims=True) + jnp.sum(s, axis=1, keepdims=True)
    h = num / jnp.maximum(jnp.abs(den), jnp.exp(-m_c))
    hn = h * lax.rsqrt(jnp.mean(h * h, axis=1, keepdims=True) + NORM_EPS)
    y_ref[...] = (hn * ng_ref[...] * jax.nn.sigmoid(og_ref[...])).astype(y_ref.dtype)

    b_last = b_c[L - 1:L, :]
    m_last = m_c[L - 1:L, :]
    kw = k * jnp.exp(b_last - b_c + ig_c - m_last)
    decay = jnp.exp(b_last + m0 - m_last)
    c_ref[0, 0] = decay * c0 + _dot(kw.T.astype(BF16), vb)
    n_ref[0, 0] = decay * n0 + jnp.sum(kw, axis=0, keepdims=True)
    m_ref[0, 0] = jnp.broadcast_to(m_last, (1, LANES))


def _mlstm_prompt(z, g_all, norm_g, bsz, seq):
    L = _tile(seq, (256, 128))
    nc = seq // L
    gp = jnp.transpose(g_all[:, :2 * M_HEADS].reshape(bsz, seq, 2 * M_HEADS), (0, 2, 1))
    g_row = gp.reshape(bsz, 2 * M_HEADS, 1, seq)
    g_col = gp.reshape(bsz, 2 * M_HEADS, seq, 1)
    row = lambda off: pl.BlockSpec((1, 1, 1, L), lambda b, h, c: (b, off + h, 0, c))
    col = lambda off: pl.BlockSpec((1, 1, L, 1), lambda b, h, c: (b, off + h, c, 0))
    state = lambda shape: pl.BlockSpec((1, 1) + shape, lambda b, h, c: (b, h, 0, 0))
    return pl.pallas_call(
        _mlstm_prompt_kernel,
        out_shape=(jax.ShapeDtypeStruct((bsz * seq, M_V_W), BF16),
                   jax.ShapeDtypeStruct((bsz, M_HEADS, M_QK, M_V), F32),
                   jax.ShapeDtypeStruct((bsz, M_HEADS, 1, M_QK), F32),
                   jax.ShapeDtypeStruct((bsz, M_HEADS, 1, LANES), F32)),
        grid=(bsz, M_HEADS, nc),
        in_specs=[pl.BlockSpec((L, M_QK), lambda b, h, c: (b * nc + c, COL_QM // M_QK + h)),
                  pl.BlockSpec((L, M_QK), lambda b, h, c: (b * nc + c, COL_KM // M_QK + h)),
                  pl.BlockSpec((L, M_V), lambda b, h, c: (b * nc + c, COL_VM // M_V + h)),
                  pl.BlockSpec((L, M_V), lambda b, h, c: (b * nc + c, COL_OM // M_V + h)),
                  row(0), row(M_HEADS), col(0), col(M_HEADS),
                  pl.BlockSpec((1, M_V), lambda b, h, c: (0, h))],
        out_specs=(pl.BlockSpec((L, M_V), lambda b, h, c: (b * nc + c, h)),
                   state((M_QK, M_V)), state((1, M_QK)), state((1, LANES))),
        compiler_params=_params(("parallel", "parallel", "arbitrary"), 32),
        name="mlstm_prompt",
    )(z, z, z, z, g_row, g_row, g_col, g_col, norm_g)


def _mlstm_sample_kernel(qr_ref, kr_ref, vr_ref, or_ref, g_ref, m0_ref, n0_ref, c0_ref, ng_ref,
                         y_ref, c_ref, n_ref, m_ref):
    g = g_ref[0]
    m0_all = m0_ref[0]
    lane = lax.broadcasted_iota(I32, (1, LANES), 1)
    m_out = jnp.zeros((1, LANES), F32)
    gate = []
    for h in range(M_HEADS):
        ig = g[:, h:h + 1]
        lf = _log_sigmoid(g[:, M_HEADS + h:M_HEADS + h + 1])
        m0 = m0_all[:, h:h + 1]
        m = jnp.maximum(lf + m0, ig)
        gate.append((m, jnp.exp(ig - m), jnp.exp(lf + m0 - m)))
    stack = [qr_ref[0, h] * M_Q_SCALE for h in range(M_HEADS)] + [gate[h][1] * kr_ref[0, h] for h in range(M_HEADS)]
    stack.append(jnp.zeros((LANES - len(stack), M_QK), F32))
    cols = jnp.transpose(jnp.concatenate(stack, axis=0))
    for h in range(M_HEADS):
        m, w, decay = gate[h]
        q_r = qr_ref[0, h] * M_Q_SCALE
        k_r = kr_ref[0, h]
        v_r = vr_ref[0, h]
        q_c = cols[:, h:h + 1]
        c0 = c0_ref[0, 0, h]
        n0 = n0_ref[0, 0, h]
        s = jnp.sum(q_r * k_r, axis=1, keepdims=True) * w
        num = decay * jnp.sum(q_c * c0, axis=0, keepdims=True) + s * v_r
        den = decay * jnp.sum(q_r * n0, axis=1, keepdims=True) + s
        hh = num / jnp.maximum(jnp.abs(den), jnp.exp(-m))
        hn = hh * lax.rsqrt(jnp.mean(hh * hh, axis=1, keepdims=True) + NORM_EPS)
        y = hn * ng_ref[:, h * M_V:(h + 1) * M_V] * jax.nn.sigmoid(or_ref[0, h])
        y_ref[0, :, h * M_V:(h + 1) * M_V] = y.astype(y_ref.dtype)
        c_ref[0, h] = decay * c0 + cols[:, M_HEADS + h:M_HEADS + h + 1] * v_r
        n_ref[0, h] = decay * n0 + w * k_r
        m_out = jnp.where(lane == h, m, m_out)
    m_ref[0] = m_out


def _mlstm_sample(zs, gs, state_c, state_n, state_m, norm_g, layer):
    ns = zs.shape[0]
    qm = zs[:, COL_QM:COL_QM + M_QK_W]
    km = zs[:, COL_KM:COL_KM + M_QK_W]
    rows = lambda a, w: a.reshape(ns, M_HEADS, 1, w)
    m0 = jnp.pad(state_m[layer], ((0, 0), (0, LANES - M_HEADS))).reshape(ns, 1, LANES)
    row_spec = lambda w: pl.BlockSpec((1, M_HEADS, 1, w), lambda s: (s, 0, 0, 0))
    lane_spec = pl.BlockSpec((1, 1, LANES), lambda s: (s, 0, 0))
    return pl.pallas_call(
        _mlstm_sample_kernel,
        out_shape=(jax.ShapeDtypeStruct((ns, 1, M_V_W), BF16),
                   jax.ShapeDtypeStruct((ns, M_HEADS, M_QK, M_V), F32),
                   jax.ShapeDtypeStruct((ns, M_HEADS, 1, M_QK), F32),
                   jax.ShapeDtypeStruct((ns, 1, LANES), F32)),
        grid=(ns,),
        in_specs=[row_spec(M_QK), row_spec(M_QK), row_spec(M_V), row_spec(M_V),
                  lane_spec, lane_spec,
                  pl.BlockSpec((1, 1, M_HEADS, 1, M_QK), lambda s: (layer, s, 0, 0, 0)),
                  pl.BlockSpec((1, 1, M_HEADS, M_QK, M_V), lambda s: (layer, s, 0, 0, 0)),
                  pl.BlockSpec((1, M_V_W), lambda s: (0, 0))],
        out_specs=(pl.BlockSpec((1, 1, M_V_W), lambda s: (s, 0, 0)),
                   pl.BlockSpec((1, M_HEADS, M_QK, M_V), lambda s: (s, 0, 0, 0)),
                   row_spec(M_QK), lane_spec),
        compiler_params=_params(("parallel",), 32),
        name="mlstm_sample",
    )(rows(qm, M_QK), rows(km, M_QK), rows(zs[:, COL_VM:COL_VM + M_V_W], M_V), rows(zs[:, COL_OM:COL_OM + M_V_W], M_V),
      gs.reshape(ns, 1, LANES), m0,
      state_n.reshape(state_n.shape[0], ns, M_HEADS, 1, M_QK), state_c, norm_g)


def _layernorm_chunks(chunks, d):
    total = sum(jnp.sum(c, axis=1, keepdims=True) for c in chunks)
    mu = total * (1.0 / d)
    var = sum(jnp.sum(jnp.square(c - mu), axis=1, keepdims=True) for c in chunks) * (1.0 / d)
    return mu, lax.rsqrt(var + LN_EPS)


def _merge_kernel(alpha, *refs):
    att_ref, mix_ref, wa_ref, wb_ref, x_ref, g_ref, b_ref, wr_ref, br_ref = refs[:9]
    h_ref, ids_ref, wts_ref = refs[-3:]
    j = pl.program_id(1)
    tn = x_ref.shape[1]
    d = h_ref.shape[1]
    nj = d // tn
    cols = lambda c: slice(c * tn, (c + 1) * tn)
    pre = alpha * x_ref[...] + _dot(att_ref[...], wa_ref[...]) + _dot(mix_ref[...], wb_ref[...])
    for c in range(nj):
        @pl.when(j == c)
        def _(c=c):
            h_ref[:, cols(c)] = pre

    @pl.when(j == nj - 1)
    def _():
        mu, rstd = _layernorm_chunks([h_ref[:, cols(c)] for c in range(nj)], d)
        lg = br_ref[...]
        for c in range(nj):
            y = (h_ref[:, cols(c)] - mu) * rstd * g_ref[:, cols(c)] + b_ref[:, cols(c)]
            h_ref[:, cols(c)] = y
            lg = lg + _dot(y.astype(BF16), wr_ref[cols(c), :])
        lane = lax.broadcasted_iota(I32, lg.shape, 1)
        lane_f = lane.astype(F32)
        far = float(LANES)
        gl = jnp.where(lane < N_GROUPS, lg, -jnp.inf)
        gmax = jnp.max(gl, axis=1, keepdims=True)
        gsel = jnp.min(jnp.where(gl == gmax, lane_f, far), axis=1, keepdims=True).astype(I32)
        gw = 1.0 / jnp.sum(jnp.where(lane < N_GROUPS, jnp.exp(gl - gmax), 0.0), axis=1, keepdims=True)
        e_lane = lane - N_GROUPS
        in_grp = (e_lane >= 0) & (e_lane < N_EXPERTS) & ((e_lane // EXPERTS_PER_GROUP) == gsel)
        el = jnp.where(in_grp, lg, -jnp.inf)
        t1 = jnp.max(el, axis=1, keepdims=True)
        i1 = jnp.min(jnp.where(el == t1, lane_f, far), axis=1, keepdims=True)
        el2 = jnp.where(lane_f == i1, -jnp.inf, el)
        t2 = jnp.max(el2, axis=1, keepdims=True)
        i2 = jnp.min(jnp.where(el2 == t2, lane_f, far), axis=1, keepdims=True)
        e21 = jnp.exp(t2 - t1)
        w1 = gw / (1.0 + e21)
        w2 = w1 * e21
        ids = jnp.where(lane == 0, i1, jnp.where(lane == 1, i2, float(N_GROUPS))) - float(N_GROUPS)
        ids_ref[...] = ids.astype(I32)
        wts_ref[...] = jnp.where(lane == 0, w1, jnp.where(lane == 1, w2, 0.0))


def _merge(att, mix, wob, x, ln_g, ln_b, wr, br, alpha, n_total, row0, prev=None):
    r, d = x.shape
    half = att.shape[1]
    tm = _tile(r, (512, 256, 128, 64))
    assert row0 % tm == 0
    o0 = row0 // tm
    tn = _tile(d, (1024, 512))
    rows = lambda w: pl.BlockSpec((1, w), lambda i, j: (0, 0))
    prev = () if prev is None else tuple(prev)
    n_in = 9
    return pl.pallas_call(
        functools.partial(_merge_kernel, alpha),
        out_shape=(jax.ShapeDtypeStruct((n_total, d), F32),
                   jax.ShapeDtypeStruct((n_total, LANES), I32),
                   jax.ShapeDtypeStruct((n_total, LANES), F32)),
        grid=(r // tm, d // tn),
        in_specs=[pl.BlockSpec((tm, half), lambda i, j: (i, 0)),
                  pl.BlockSpec((tm, half), lambda i, j: (i, 0)),
                  pl.BlockSpec((half, tn), lambda i, j: (0, j)),
                  pl.BlockSpec((half, tn), lambda i, j: (1, j)),
                  pl.BlockSpec((tm, tn), lambda i, j: (i, j)),
                  rows(d), rows(d),
                  pl.BlockSpec((d, LANES), lambda i, j: (0, 0)),
                  rows(LANES)] + [pl.BlockSpec(memory_space=pl.ANY)] * len(prev),
        out_specs=(pl.BlockSpec((tm, d), lambda i, j: (i + o0, 0)),
                   pl.BlockSpec((tm, LANES), lambda i, j: (i + o0, 0)),
                   pl.BlockSpec((tm, LANES), lambda i, j: (i + o0, 0))),
        input_output_aliases={n_in + k: k for k in range(len(prev))},
        compiler_params=_params(("parallel", "arbitrary"), 52),
        name="merge_ln_router",
    )(att, mix, wob, wob, x, ln_g, ln_b, wr, br, *prev)


def _moe_kernel(te_ref, ti_ref, nu_ref, src_ref, h_ref, wg_ref, wu_ref, wd_ref, o_ref,
                xf_s, xb_s, hg_s, hu_s, a_s, sem):
    del te_ref, ti_ref
    t = pl.program_id(0)
    ph = pl.program_id(1)
    nk, tm, kc = xb_s.shape
    n_phase = nk + xf_s.shape[2] // o_ref.shape[1]
    share = tm // n_phase
    assert share * n_phase == tm
    nu = nu_ref[0]

    def start_rows(tile, lo, count):
        slot = tile % 2

        def body(g, carry):
            base = pl.multiple_of(lo + g * SUBLANES, SUBLANES)
            for u in range(SUBLANES):
                pltpu.make_async_copy(h_ref.at[pl.ds(src_ref[tile * tm + base + u], 1), :],
                                      xf_s.at[slot, pl.ds(base + u, 1), :],
                                      sem.at[slot]).start(priority=GATHER_PRIORITY)
            return carry
        lax.fori_loop(0, count // SUBLANES, body, 0)

    def wait_tile(tile):
        slot = tile % 2
        pltpu.make_async_copy(h_ref.at[pl.ds(0, tm), :], xf_s.at[slot], sem.at[slot]).wait()

    @pl.when(t < nu)
    def _():
        @pl.when(t + 1 < nu)
        def _():
            start_rows(t + 1, ph * share, share)

        @pl.when(ph == 0)
        def _():
            @pl.when(t == 0)
            def _():
                start_rows(t, 0, tm)

            wait_tile(t)
            for c in range(nk):
                xb_s[c] = xf_s[t % 2, :, c * kc:(c + 1) * kc].astype(BF16)

        @pl.when(ph < nk)
        def _():
            x = xb_s[ph]
            pg = _dot(x, wg_ref[0, 0].astype(BF16))
            pu = _dot(x, wu_ref[0, 0].astype(BF16))

            @pl.when(ph == 0)
            def _():
                hg_s[...] = pg
                hu_s[...] = pu

            @pl.when(ph != 0)
            def _():
                hg_s[...] += pg
                hu_s[...] += pu

        @pl.when(ph == nk - 1)
        def _():
            hg = hg_s[...]
            a_s[...] = (hg * jax.nn.sigmoid(hg) * hu_s[...]).astype(BF16)

        @pl.when(ph >= nk)
        def _():
            o_ref[...] = _dot(a_s[...], wd_ref[0, 0].astype(BF16))


def _moe_experts(h, src, w_gate, w_up, w_down, tile_expert, tile_index, n_used, layer):
    tm = MOE_TILE
    n_tiles = tile_expert.shape[0]
    d = w_gate.shape[2]
    f = w_gate.shape[3]
    kc = nc = MOE_CHUNK
    nk = d // kc
    nn = d // nc

    def used(t, nu, a, b):
        return jnp.where(t < nu[0], a, b)

    def k_idx(t, ph, nu):
        return used(t, nu, jnp.minimum(ph, nk - 1), nk - 1)

    def n_idx(t, ph, nu):
        return used(t, nu, jnp.maximum(ph - nk, 0), nn - 1)

    grid_spec = pltpu.PrefetchScalarGridSpec(
        num_scalar_prefetch=4, grid=(n_tiles, nk + nn),
        in_specs=[pl.BlockSpec(memory_space=pl.ANY),
                  pl.BlockSpec((1, 1, kc, f), lambda t, ph, te, ti, nu, src: (layer, te[t], k_idx(t, ph, nu), 0)),
                  pl.BlockSpec((1, 1, kc, f), lambda t, ph, te, ti, nu, src: (layer, te[t], k_idx(t, ph, nu), 0)),
                  pl.BlockSpec((1, 1, f, nc), lambda t, ph, te, ti, nu, src: (layer, te[t], 0, n_idx(t, ph, nu)))],
        out_specs=pl.BlockSpec((tm, nc), lambda t, ph, te, ti, nu, src: (ti[t], n_idx(t, ph, nu))),
        scratch_shapes=[pltpu.VMEM((2, tm, d), F32), pltpu.VMEM((nk, tm, kc), BF16),
                        pltpu.VMEM((tm, f), F32), pltpu.VMEM((tm, f), F32), pltpu.VMEM((tm, f), BF16),
                        pltpu.SemaphoreType.DMA((2,))])
    return pl.pallas_call(
        _moe_kernel,
        out_shape=jax.ShapeDtypeStruct((n_tiles * tm, d), F32),
        grid_spec=grid_spec,
        compiler_params=_params(("arbitrary", "arbitrary"), 52),
        name="moe_experts",
    )(tile_expert, tile_index, n_used, src, h, w_gate, w_up, w_down)


def _combine_kernel(alpha, row0, n_tok, pos_ref, h_ref, w_ref, g_ref, b_ref, ys_ref, o_ref, f_s, sem):
    i = pl.program_id(0)
    tm, d = o_ref.shape

    def start_tile(tile):
        slot = tile % 2

        def body(g, carry):
            base = pl.multiple_of(g * SUBLANES, SUBLANES)
            for u in range(SUBLANES):
                for s in range(2):
                    pltpu.make_async_copy(
                        ys_ref.at[pl.ds(pos_ref[s * n_tok + row0 + tile * tm + base + u], 1), :],
                        f_s.at[slot, s, pl.ds(base + u, 1), :], sem.at[slot]).start(priority=GATHER_PRIORITY)
            return carry
        lax.fori_loop(0, tm // SUBLANES, body, 0)

    def wait_tile(tile):
        slot = tile % 2
        for s in range(2):
            pltpu.make_async_copy(ys_ref.at[pl.ds(0, tm), :], f_s.at[slot, s], sem.at[slot]).wait()

    @pl.when(i == 0)
    def _():
        start_tile(i)

    @pl.when(i + 1 < pl.num_programs(0))
    def _():
        start_tile(i + 1)

    wait_tile(i)
    w = w_ref[...]
    v = alpha * h_ref[...] + w[:, 0:1] * f_s[i % 2, 0] + w[:, 1:2] * f_s[i % 2, 1]
    mu = jnp.mean(v, axis=1, keepdims=True)
    vc = v - mu
    var = jnp.sum(vc * vc, axis=1, keepdims=True) * (1.0 / d)
    o_ref[...] = vc * lax.rsqrt(var + LN_EPS) * g_ref[...] + b_ref[...]


def _combine(h, ys, pos, wts, ln_g, ln_b, alpha, row0, nrows):
    n, d = h.shape
    tm = next(c for c in (128, 64, 32, 16, 8) if nrows % c == 0 and row0 % c == 0)
    o0 = row0 // tm
    rows = pl.BlockSpec((1, d), lambda i, pos: (0, 0))
    grid_spec = pltpu.PrefetchScalarGridSpec(
        num_scalar_prefetch=1, grid=(nrows // tm,),
        in_specs=[pl.BlockSpec((tm, d), lambda i, pos: (i + o0, 0)),
                  pl.BlockSpec((tm, LANES), lambda i, pos: (i + o0, 0)), rows, rows,
                  pl.BlockSpec(memory_space=pl.ANY)],
        out_specs=pl.BlockSpec((tm, d), lambda i, pos: (i, 0)),
        scratch_shapes=[pltpu.VMEM((2, 2, tm, d), F32), pltpu.SemaphoreType.DMA((2,))])
    return pl.pallas_call(
        functools.partial(_combine_kernel, alpha, row0, n),
        out_shape=jax.ShapeDtypeStruct((nrows, d), F32),
        grid_spec=grid_spec,
        compiler_params=_params(("arbitrary",), 32),
        name="combine_ln",
    )(pos, h, wts, ln_g, ln_b, ys)


def _routing_tables(ids, n_tiles):
    n = ids.shape[0]
    tm = MOE_TILE
    e = ids[:, :2].reshape(-1)
    onehot = (e[:, None] == jnp.arange(N_EXPERTS, dtype=I32)[None, :]).astype(I32)
    csum = jnp.cumsum(onehot, axis=0)
    rank = jnp.sum((csum - onehot) * onehot, axis=1)
    counts = csum[-1]
    tiles = (counts + tm - 1) // tm
    tile_end = jnp.cumsum(tiles)
    n_used = tile_end[-1]
    pos = ((tile_end - tiles)[e] * tm + rank).astype(I32)
    src = (jnp.arange(n_tiles * tm, dtype=I32) % n).at[pos].set(jnp.arange(2 * n, dtype=I32) // 2)
    t = jnp.minimum(jnp.arange(n_tiles, dtype=I32), n_used - 1)
    tile_expert = jnp.minimum(jnp.sum((tile_end[None, :] <= t[:, None]).astype(I32), axis=1), N_EXPERTS - 1)
    pos2 = pos.reshape(n, 2)
    back = jnp.concatenate([pos2[:, 0], pos2[:, 1]])
    return src, back, tile_expert, t, n_used.reshape(1).astype(I32)


def _layer(l, x_p, x_s, bsz, seq, cache_k, cache_v, state_c, state_n, state_m, page_table,
           w_in, b_gates, norm_g, w_out, ln1_g, ln1_b, w_group, b_group, w_router, b_router,
           w_e_gate, w_e_up, w_e_down, ln2_g, ln2_b, alpha):
    n_p, d = x_p.shape
    n_s = x_s.shape[0]
    n = n_p + n_s

    wt = jnp.transpose(w_in[l]).astype(BF16)
    wg = jnp.pad(w_in[l, :, COL_G:], ((0, 0), (0, LANES - 2 * M_HEADS)))
    bg = jnp.pad(b_gates[l], (0, LANES - 2 * M_HEADS)).reshape(1, LANES)
    z_p = _in_proj(x_p.astype(BF16), wt, COL_G)
    z_s = _in_proj(x_s.astype(BF16), wt, COL_G)
    g_p = _gates(x_p, wg, bg)
    g_s = _gates(x_s, wg, bg)

    ng = norm_g[l].reshape(1, M_V_W)
    att_p = _moba_prompt(z_p, bsz, seq)
    mix_p, c_p, n_pr, m_p = _mlstm_prompt(z_p, g_p, ng, bsz, seq)
    att_s = _moba_sample(z_s, cache_k, cache_v, page_table, l)
    mix_s, c_s, n_sm, m_s = _mlstm_sample(z_s, g_s, state_c, state_n, state_m, ng, l)

    wr = jnp.pad(jnp.concatenate([w_group[l], w_router[l]], axis=1),
                 ((0, 0), (0, LANES - N_GROUPS - N_EXPERTS))).astype(BF16)
    br = jnp.pad(jnp.concatenate([b_group[l], b_router[l]]), (0, LANES - N_GROUPS - N_EXPERTS)).reshape(1, LANES)
    wob = w_out[l].astype(BF16)
    g1, b1 = ln1_g[l].reshape(1, d), ln1_b[l].reshape(1, d)
    done = _merge(att_p, mix_p, wob, x_p, g1, b1, wr, br, alpha, n, 0)
    h2, ids, wts = _merge(att_s, mix_s.reshape(n_s, M_V_W), wob, x_s, g1, b1, wr, br, alpha, n, n_p, prev=done)

    n_tiles = (2 * n) // MOE_TILE + N_EXPERTS + 1
    src, back, tile_expert, tile_index, n_used = _routing_tables(ids, n_tiles)
    ys = _moe_experts(h2, src, w_e_gate, w_e_up, w_e_down, tile_expert, tile_index, n_used, l)
    g2, b2 = ln2_g[l].reshape(1, d), ln2_b[l].reshape(1, d)
    y_p = _combine(h2, ys, back, wts, g2, b2, alpha, 0, n_p)
    y_s = _combine(h2, ys, back, wts, g2, b2, alpha, n_p, n_s)

    kv = lambda rows, col: rows[:, col:col + ATT_KV_W]
    return (y_p, y_s,
            kv(z_p, COL_K).reshape(bsz, seq, ATT_KV_HEADS, HEAD_DIM),
            kv(z_p, COL_V).reshape(bsz, seq, ATT_KV_HEADS, HEAD_DIM),
            kv(z_s, COL_K).reshape(n_s, 1, ATT_KV_HEADS, HEAD_DIM),
            kv(z_s, COL_V).reshape(n_s, 1, ATT_KV_HEADS, HEAD_DIM),
            c_p, n_pr.reshape(bsz, M_HEADS, M_QK), m_p[:, :, 0, 0],
            c_s, n_sm.reshape(n_s, M_HEADS, M_QK), m_s[:, 0, :M_HEADS])


def kernel(x_prompt, x_sample, cache_k, cache_v, state_mlstm_c, state_mlstm_n, state_mlstm_m, page_table,
           w_in, b_gates, mlstm_norm_g, w_out, ln1_g, ln1_b, w_group, b_group, w_router, b_router,
           w_e_gate, w_e_up, w_e_down, ln2_g, ln2_b):
    bsz, seq, d = x_prompt.shape
    n_s, dec_seq, _ = x_sample.shape
    depth = w_in.shape[0]
    assert dec_seq == 1 and seq % MOBA_BLOCK == 0 and w_in.shape[2] == COL_G + 2 * M_HEADS
    assert (page_table.shape[1] * cache_k.shape[2]) % MOBA_BLOCK == 0 and MOBA_BLOCK % cache_k.shape[2] == 0
    alpha = (2 * depth) ** 0.25
    x_p = x_prompt.reshape(bsz * seq, d)
    x_s = x_sample.reshape(n_s, d)
    per_layer = []
    for l in range(depth):
        outs = _layer(l, x_p, x_s, bsz, seq, cache_k, cache_v, state_mlstm_c, state_mlstm_n, state_mlstm_m,
                      page_table, w_in, b_gates, mlstm_norm_g, w_out, ln1_g, ln1_b, w_group, b_group, w_router,
                      b_router, w_e_gate, w_e_up, w_e_down, ln2_g, ln2_b, alpha)
        x_p, x_s = outs[0], outs[1]
        per_layer.append(outs[2:])
    stacked = [jnp.stack([p[i] for p in per_layer]) for i in range(10)]
    return (x_p.reshape(bsz, seq, d), x_s.reshape(n_s, 1, d), *stacked)
```

```python
import functools

import jax
import jax.numpy as jnp
from jax import lax
from jax.experimental import pallas as pl
from jax.experimental.pallas import tpu as pltpu

F32, BF16, I32 = jnp.float32, jnp.bfloat16, jnp.int32

SUBLANES = 8
LANES = 128
MIB = 1024 * 1024

HEAD_DIM = 128
ATT_HEADS = 16
ATT_KV_HEADS = 8
ATT_GROUP = ATT_HEADS // ATT_KV_HEADS
ATT_SCALE = HEAD_DIM ** -0.5
LOG2_E = 1.4426950408889634
MOBA_BLOCK = 256
MOBA_TOPK = 3
M_HEADS = 4
M_QK = 256
M_V = 512
M_Q_SCALE = M_QK ** -0.5
N_GROUPS = 4
EXPERTS_PER_GROUP = 8
N_EXPERTS = N_GROUPS * EXPERTS_PER_GROUP
LN_EPS = 1e-5
NORM_EPS = 1e-6

ATT_Q_W = ATT_HEADS * HEAD_DIM
ATT_KV_W = ATT_KV_HEADS * HEAD_DIM
M_QK_W = M_HEADS * M_QK
M_V_W = M_HEADS * M_V
COL_Q = 0
COL_K = COL_Q + ATT_Q_W
COL_V = COL_K + ATT_KV_W
COL_QM = COL_V + ATT_KV_W
COL_KM = COL_QM + M_QK_W
COL_VM = COL_KM + M_QK_W
COL_OM = COL_VM + M_V_W
COL_G = COL_OM + M_V_W

MOE_TILE = 640
MOE_CHUNK = 1024
GATHER_PRIORITY = 1


def _params(sem, vmem_mib):
    return pltpu.CompilerParams(dimension_semantics=sem, vmem_limit_bytes=vmem_mib * MIB)


def _dot(a, b):
    return jnp.dot(a, b, preferred_element_type=F32)


def _dot_nt(a, b):
    return lax.dot_general(a, b, (((1,), (1,)), ((), ())), preferred_element_type=F32)


def _log_sigmoid(x):
    return jnp.minimum(x, 0.0) - jnp.log1p(jnp.exp(-jnp.abs(x)))


def _tile(n, candidates):
    for c in candidates:
        if n % c == 0:
            return c
    raise ValueError(f"no tile for {n} in {candidates}")


def _matmul_nt_kernel(x_ref, w_ref, o_ref):
    o_ref[...] = _dot_nt(x_ref[...], w_ref[...])


def _in_proj(xb, wt, w):
    n, d = xb.shape
    tm = _tile(n, (1024, 832, 640, 512, 256, 128, 64))
    tn = _tile(w, (1024, 512, 256, 128))
    return pl.pallas_call(
        _matmul_nt_kernel,
        out_shape=jax.ShapeDtypeStruct((n, w), F32),
        grid=(w // tn, n // tm),
        in_specs=[pl.BlockSpec((tm, d), lambda j, i: (i, 0)),
                  pl.BlockSpec((tn, d), lambda j, i: (j, 0))],
        out_specs=pl.BlockSpec((tm, tn), lambda j, i: (i, j)),
        compiler_params=_params(("parallel", "parallel"), 48),
        name="in_proj",
    )(xb, wt)


def _gates_kernel(x_ref, w_ref, b_ref, o_ref):
    x = x_ref[...]
    w = w_ref[...]
    xh = x.astype(BF16)
    xl = (x - xh.astype(F32)).astype(BF16)
    wh = w.astype(BF16)
    wl = (w - wh.astype(F32)).astype(BF16)
    o_ref[...] = _dot(xh, wh) + _dot(xl, wh) + _dot(xh, wl) + b_ref[...]


def _gates(x, wg, bg):
    n, d = x.shape
    tm = _tile(n, (512, 416, 320, 256, 128, 64))
    return pl.pallas_call(
        _gates_kernel,
        out_shape=jax.ShapeDtypeStruct((n, LANES), F32),
        grid=(n // tm,),
        in_specs=[pl.BlockSpec((tm, d), lambda i: (i, 0)),
                  pl.BlockSpec((d, LANES), lambda i: (0, 0)),
                  pl.BlockSpec((1, LANES), lambda i: (0, 0))],
        out_specs=pl.BlockSpec((tm, LANES), lambda i: (i, 0)),
        compiler_params=_params(("parallel",), 40),
        name="gates",
    )(x, wg, bg)


def _moba_prompt_kernel(q_ref, k_ref, v_ref, o_ref, kb_s, vt_s, km_s):
    i = pl.program_id(2)
    nb = kb_s.shape[0]
    blk = MOBA_BLOCK
    nq = ATT_GROUP * blk

    @pl.when(i == 0)
    def _():
        for j in range(nb):
            kj = k_ref[j * blk:(j + 1) * blk, :]
            kb_s[j] = kj.astype(BF16)
            km_s[j:j + 1, :] = jnp.mean(kj, axis=0, keepdims=True)
            vt_s[j] = v_ref[j * blk:(j + 1) * blk, :].T.astype(BF16)
        km_s[nb:, :] = jnp.zeros((km_s.shape[0] - nb, HEAD_DIM), F32)

    q = q_ref[...]
    q2 = jnp.concatenate([q[:, g * HEAD_DIM:(g + 1) * HEAD_DIM] for g in range(ATT_GROUP)], axis=0).astype(BF16)

    def select(own):
        gate = _dot_nt(km_s[...].astype(BF16), q2)
        jj = lax.broadcasted_iota(I32, gate.shape, 0)
        g = jnp.where(jj < own, gate, -jnp.inf)
        sel = jnp.zeros(gate.shape, F32)
        for _ in range(MOBA_TOPK):
            mx = jnp.max(g, axis=0, keepdims=True)
            first = jnp.min(jnp.where(g == mx, jj, 2 * nb), axis=0, keepdims=True)
            pick = (jj == first) & (mx > -jnp.inf)
            sel = jnp.where(pick, 1.0, sel)
            g = jnp.where(pick, -jnp.inf, g)
        return sel

    ri = lax.broadcasted_iota(I32, (blk, nq), 0)
    ci = lax.broadcasted_iota(I32, (blk, nq), 1)
    causal = ri <= (ci & (blk - 1))
    c2 = ATT_SCALE * LOG2_E

    def attend_all(own):
        sel = select(own) if own > MOBA_TOPK else None
        m = l = acc = None
        for j in (own,) + tuple(range(own)):
            s = _dot_nt(kb_s[j], q2)
            if j == own:
                s = jnp.where(causal, s, -jnp.inf)
            elif sel is not None:
                s = jnp.where(sel[j:j + 1, :] > 0.5, s, -jnp.inf)
            m_blk = jnp.max(s, axis=0, keepdims=True)
            m_new = m_blk if m is None else jnp.maximum(m, m_blk)
            p = jnp.exp2((s - m_new) * c2)
            pv = _dot(vt_s[j], p.astype(BF16))
            if m is None:
                l, acc = jnp.sum(p, axis=0, keepdims=True), pv
            else:
                alpha = jnp.exp2((m - m_new) * c2)
                l = alpha * l + jnp.sum(p, axis=0, keepdims=True)
                acc = alpha * acc + pv
            m = m_new
        out = (acc / l).T
        for g_i in range(ATT_GROUP):
            o_ref[:, g_i * HEAD_DIM:(g_i + 1) * HEAD_DIM] = out[g_i * blk:(g_i + 1) * blk, :].astype(o_ref.dtype)

    for own in range(nb):
        pl.when(i == own)(functools.partial(attend_all, own))


def _moba_prompt(z, bsz, seq):
    nb = seq // MOBA_BLOCK
    qw = ATT_GROUP * HEAD_DIM
    return pl.pallas_call(
        _moba_prompt_kernel,
        out_shape=jax.ShapeDtypeStruct((bsz * seq, ATT_Q_W), BF16),
        grid=(bsz, ATT_KV_HEADS, nb),
        in_specs=[pl.BlockSpec((MOBA_BLOCK, qw), lambda b, h, i: (b * nb + i, COL_Q // qw + h)),
                  pl.BlockSpec((seq, HEAD_DIM), lambda b, h, i: (b, COL_K // HEAD_DIM + h)),
                  pl.BlockSpec((seq, HEAD_DIM), lambda b, h, i: (b, COL_V // HEAD_DIM + h))],
        out_specs=pl.BlockSpec((MOBA_BLOCK, qw), lambda b, h, i: (b * nb + i, h)),
        scratch_shapes=[pltpu.VMEM((nb, MOBA_BLOCK, HEAD_DIM), BF16),
                        pltpu.VMEM((nb, HEAD_DIM, MOBA_BLOCK), BF16),
                        pltpu.VMEM((2 * SUBLANES, HEAD_DIM), F32)],
        compiler_params=_params(("parallel", "parallel", "arbitrary"), 32),
        name="moba_prompt",
    )(z, z, z)


def _moba_sample_kernel(n_pages, pt_ref, q_ref, kn_ref, vn_ref, *rest):
    del pt_ref
    kp = rest[:n_pages]
    vp = rest[n_pages:2 * n_pages]
    o_ref = rest[2 * n_pages]
    page = kp[0].shape[2]
    per_blk = MOBA_BLOCK // page
    nb = n_pages // per_blk
    nrow = page * ATT_KV_HEADS
    both = lambda a: jnp.concatenate([a] * ATT_GROUP, axis=0)

    q = q_ref[0]
    qb = q.astype(BF16)
    ri = lax.broadcasted_iota(I32, (ATT_HEADS, nrow), 0)
    ci = lax.broadcasted_iota(I32, (ATT_HEADS, nrow), 1)
    own = (ci % ATT_KV_HEADS) == (ri % ATT_KV_HEADS)

    scores, ksum = [], []
    for p in range(n_pages):
        k3 = kp[p][0, 0]
        scores.append(_dot_nt(qb, k3.reshape(nrow, HEAD_DIM).astype(BF16)) * ATT_SCALE)
        ksum.append(jnp.sum(k3, axis=0))
    gates = []
    for j in range(nb):
        kmean = sum(ksum[j * per_blk:(j + 1) * per_blk]) * (1.0 / MOBA_BLOCK)
        gates.append(jnp.sum(q * both(kmean), axis=1, keepdims=True))
    sel = []
    for j in range(nb):
        rank = jnp.zeros(gates[j].shape, F32)
        for jp in range(nb):
            if jp < j:
                rank = rank + (gates[jp] >= gates[j]).astype(F32)
            elif jp > j:
                rank = rank + (gates[jp] > gates[j]).astype(F32)
        sel.append(rank < MOBA_TOPK)

    s_new = jnp.sum(q * both(kn_ref[0]), axis=1, keepdims=True) * ATT_SCALE
    m = s_new
    for p in range(n_pages):
        scores[p] = jnp.where(sel[p // per_blk], jnp.where(own, scores[p], -jnp.inf), -jnp.inf)
        m = jnp.maximum(m, jnp.max(scores[p], axis=1, keepdims=True))
    p_new = jnp.exp(s_new - m)
    l = p_new
    acc = p_new * both(vn_ref[0])
    for p in range(n_pages):
        pp = jnp.exp(scores[p] - m)
        l = l + jnp.sum(pp, axis=1, keepdims=True)
        acc = acc + _dot(pp.astype(BF16), vp[p][0, 0].reshape(nrow, HEAD_DIM).astype(BF16))
    o_ref[0] = (acc / l).astype(o_ref.dtype)


def _moba_sample(zs, cache_k, cache_v, page_table, layer):
    ns = zs.shape[0]
    n_pages = page_table.shape[1]
    page = cache_k.shape[2]
    pt_flat = page_table.reshape(-1).astype(I32)
    qg = jnp.transpose(zs[:, COL_Q:COL_Q + ATT_Q_W].reshape(ns, ATT_KV_HEADS, ATT_GROUP, HEAD_DIM), (0, 2, 1, 3))
    kv_new = lambda col: zs[:, col:col + ATT_KV_W].reshape(ns, ATT_KV_HEADS, HEAD_DIM)

    def page_spec(p):
        return pl.BlockSpec((1, 1, page, ATT_KV_HEADS, HEAD_DIM),
                            lambda s, pt: (layer, pt[s * n_pages + p], 0, 0, 0))

    rows = lambda r: pl.BlockSpec((1, r, HEAD_DIM), lambda s, pt: (s, 0, 0))
    grid_spec = pltpu.PrefetchScalarGridSpec(
        num_scalar_prefetch=1, grid=(ns,),
        in_specs=[rows(ATT_HEADS), rows(ATT_KV_HEADS), rows(ATT_KV_HEADS)] + [page_spec(p) for p in range(n_pages)] * 2,
        out_specs=rows(ATT_HEADS))
    out = pl.pallas_call(
        functools.partial(_moba_sample_kernel, n_pages),
        out_shape=jax.ShapeDtypeStruct((ns, ATT_HEADS, HEAD_DIM), BF16),
        grid_spec=grid_spec,
        compiler_params=_params(("parallel",), 48),
        name="moba_sample",
    )(pt_flat, qg.reshape(ns, ATT_HEADS, HEAD_DIM), kv_new(COL_K), kv_new(COL_V),
      *([cache_k] * n_pages), *([cache_v] * n_pages))
    return jnp.transpose(out.reshape(ns, ATT_GROUP, ATT_KV_HEADS, HEAD_DIM), (0, 2, 1, 3)).reshape(ns, ATT_Q_W)


def _mlstm_prompt_kernel(q_ref, k_ref, v_ref, og_ref, igr_ref, fgr_ref, igc_ref, fgc_ref, ng_ref,
                         y_ref, c_ref, n_ref, m_ref):
    @pl.when(pl.program_id(2) == 0)
    def _():
        c_ref[...] = jnp.zeros(c_ref.shape, F32)
        n_ref[...] = jnp.zeros(n_ref.shape, F32)
        m_ref[...] = jnp.zeros(m_ref.shape, F32)

    L = q_ref.shape[0]
    q = q_ref[...] * M_Q_SCALE
    k = k_ref[...]
    ig_r = igr_ref[0, 0]
    lf_r = _log_sigmoid(fgr_ref[0, 0])
    ig_c = igc_ref[0, 0]
    lf_c = _log_sigmoid(fgc_ref[0, 0])
    ri = lax.broadcasted_iota(I32, (L, L), 0)
    ci = lax.broadcasted_iota(I32, (L, L), 1)
    tril = ci <= ri
    b_c = jnp.sum(jnp.where(tril, lf_r, 0.0), axis=1, keepdims=True)
    b_r = jnp.sum(jnp.where(ri <= ci, lf_c, 0.0), axis=0, keepdims=True)
    r_r = ig_r - b_r
    m0 = m_ref[0, 0][:, 0:1]
    m_c = b_c + jnp.maximum(m0, jnp.max(jnp.where(tril, r_r, -jnp.inf), axis=1, keepdims=True))
    dmat = jnp.exp(jnp.where(tril, b_c - m_c + r_r, -jnp.inf))
    qb = q.astype(BF16)
    vb = v_ref[...].astype(BF16)
    s = _dot_nt(qb, k.astype(BF16)) * dmat
    inter = jnp.exp(b_c + m0 - m_c)
    c0 = c_ref[0, 0]
    n0 = n_ref[0, 0]
    num = inter * _dot(qb, c0.astype(BF16)) + _dot(s.astype(BF16), vb)
    den = inter * jnp.sum(q * n0, axis=1, keepdims=True) + jnp.sum(s, axis=1, keepdims=True)
    h = num / jnp.maximum(jnp.abs(den), jnp.exp(-m_c))
    hn = h * lax.rsqrt(jnp.mean(h * h, axis=1, keepdims=True) + NORM_EPS)
    y_ref[...] = (hn * ng_ref[...] * jax.nn.sigmoid(og_ref[...])).astype(y_ref.dtype)

    b_last = b_c[L - 1:L, :]
    m_last = m_c[L - 1:L, :]
    kw = k * jnp.exp(b_last - b_c + ig_c - m_last)
    decay = jnp.exp(b_last + m0 - m_last)
    c_ref[0, 0] = decay * c0 + _dot(kw.T.astype(BF16), vb)
    n_ref[0, 0] = decay * n0 + jnp.sum(kw, axis=0, keepdims=True)
    m_ref[0, 0] = jnp.broadcast_to(m_last, (1, LANES))


def _mlstm_prompt(z, g_all, norm_g, bsz, seq):
    L = _tile(seq, (256, 128))
    nc = seq // L
    gp = jnp.transpose(g_all[:, :2 * M_HEADS].reshape(bsz, seq, 2 * M_HEADS), (0, 2, 1))
    g_row = gp.reshape(bsz, 2 * M_HEADS, 1, seq)
    g_col = gp.reshape(bsz, 2 * M_HEADS, seq, 1)
    row = lambda off: pl.BlockSpec((1, 1, 1, L), lambda b, h, c: (b, off + h, 0, c))
    col = lambda off: pl.BlockSpec((1, 1, L, 1), lambda b, h, c: (b, off + h, c, 0))
    state = lambda shape: pl.BlockSpec((1, 1) + shape, lambda b, h, c: (b, h, 0, 0))
    return pl.pallas_call(
        _mlstm_prompt_kernel,
        out_shape=(jax.ShapeDtypeStruct((bsz * seq, M_V_W), BF16),
                   jax.ShapeDtypeStruct((bsz, M_HEADS, M_QK, M_V), F32),
                   jax.ShapeDtypeStruct((bsz, M_HEADS, 1, M_QK), F32),
                   jax.ShapeDtypeStruct((bsz, M_HEADS, 1, LANES), F32)),
        grid=(bsz, M_HEADS, nc),
        in_specs=[pl.BlockSpec((L, M_QK), lambda b, h, c: (b * nc + c, COL_QM // M_QK + h)),
                  pl.BlockSpec((L, M_QK), lambda b, h, c: (b * nc + c, COL_KM // M_QK + h)),
                  pl.BlockSpec((L, M_V), lambda b, h, c: (b * nc + c, COL_VM // M_V + h)),
                  pl.BlockSpec((L, M_V), lambda b, h, c: (b * nc + c, COL_OM // M_V + h)),
                  row(0), row(M_HEADS), col(0), col(M_HEADS),
                  pl.BlockSpec((1, M_V), lambda b, h, c: (0, h))],
        out_specs=(pl.BlockSpec((L, M_V), lambda b, h, c: (b * nc + c, h)),
                   state((M_QK, M_V)), state((1, M_QK)), state((1, LANES))),
        compiler_params=_params(("parallel", "parallel", "arbitrary"), 32),
        name="mlstm_prompt",
    )(z, z, z, z, g_row, g_row, g_col, g_col, norm_g)


def _mlstm_sample_kernel(qr_ref, kr_ref, vr_ref, or_ref, g_ref, m0_ref, n0_ref, c0_ref, ng_ref,
                         y_ref, c_ref, n_ref, m_ref):
    g = g_ref[0]
    m0_all = m0_ref[0]
    lane = lax.broadcasted_iota(I32, (1, LANES), 1)
    m_out = jnp.zeros((1, LANES), F32)
    gate = []
    for h in range(M_HEADS):
        ig = g[:, h:h + 1]
        lf = _log_sigmoid(g[:, M_HEADS + h:M_HEADS + h + 1])
        m0 = m0_all[:, h:h + 1]
        m = jnp.maximum(lf + m0, ig)
        gate.append((m, jnp.exp(ig - m), jnp.exp(lf + m0 - m)))
    stack = [qr_ref[0, h] * M_Q_SCALE for h in range(M_HEADS)] + [gate[h][1] * kr_ref[0, h] for h in range(M_HEADS)]
    stack.append(jnp.zeros((LANES - len(stack), M_QK), F32))
    cols = jnp.transpose(jnp.concatenate(stack, axis=0))
    for h in range(M_HEADS):
        m, w, decay = gate[h]
        q_r = qr_ref[0, h] * M_Q_SCALE
        k_r = kr_ref[0, h]
        v_r = vr_ref[0, h]
        q_c = cols[:, h:h + 1]
        c0 = c0_ref[0, 0, h]
        n0 = n0_ref[0, 0, h]
        s = jnp.sum(q_r * k_r, axis=1, keepdims=True) * w
        num = decay * jnp.sum(q_c * c0, axis=0, keepdims=True) + s * v_r
        den = decay * jnp.sum(q_r * n0, axis=1, keepdims=True) + s
        hh = num / jnp.maximum(jnp.abs(den), jnp.exp(-m))
        hn = hh * lax.rsqrt(jnp.mean(hh * hh, axis=1, keepdims=True) + NORM_EPS)
        y = hn * ng_ref[:, h * M_V:(h + 1) * M_V] * jax.nn.sigmoid(or_ref[0, h])
        y_ref[0, :, h * M_V:(h + 1) * M_V] = y.astype(y_ref.dtype)
        c_ref[0, h] = decay * c0 + cols[:, M_HEADS + h:M_HEADS + h + 1] * v_r
        n_ref[0, h] = decay * n0 + w * k_r
        m_out = jnp.where(lane == h, m, m_out)
    m_ref[0] = m_out


def _mlstm_sample(zs, gs, state_c, state_n, state_m, norm_g, layer):
    ns = zs.shape[0]
    qm = zs[:, COL_QM:COL_QM + M_QK_W]
    km = zs[:, COL_KM:COL_KM + M_QK_W]
    rows = lambda a, w: a.reshape(ns, M_HEADS, 1, w)
    m0 = jnp.pad(state_m[layer], ((0, 0), (0, LANES - M_HEADS))).reshape(ns, 1, LANES)
    row_spec = lambda w: pl.BlockSpec((1, M_HEADS, 1, w), lambda s: (s, 0, 0, 0))
    lane_spec = pl.BlockSpec((1, 1, LANES), lambda s: (s, 0, 0))
    return pl.pallas_call(
        _mlstm_sample_kernel,
        out_shape=(jax.ShapeDtypeStruct((ns, 1, M_V_W), BF16),
                   jax.ShapeDtypeStruct((ns, M_HEADS, M_QK, M_V), F32),
                   jax.ShapeDtypeStruct((ns, M_HEADS, 1, M_QK), F32),
                   jax.ShapeDtypeStruct((ns, 1, LANES), F32)),
        grid=(ns,),
        in_specs=[row_spec(M_QK), row_spec(M_QK), row_spec(M_V), row_spec(M_V),
                  lane_spec, lane_spec,
                  pl.BlockSpec((1, 1, M_HEADS, 1, M_QK), lambda s: (layer, s, 0, 0, 0)),
                  pl.BlockSpec((1, 1, M_HEADS, M_QK, M_V), lambda s: (layer, s, 0, 0, 0)),
                  pl.BlockSpec((1, M_V_W), lambda s: (0, 0))],
        out_specs=(pl.BlockSpec((1, 1, M_V_W), lambda s: (s, 0, 0)),
                   pl.BlockSpec((1, M_HEADS, M_QK, M_V), lambda s: (s, 0, 0, 0)),
                   row_spec(M_QK), lane_spec),
        compiler_params=_params(("parallel",), 32),
        name="mlstm_sample",
    )(rows(qm, M_QK), rows(km, M_QK), rows(zs[:, COL_VM:COL_VM + M_V_W], M_V), rows(zs[:, COL_OM:COL_OM + M_V_W], M_V),
      gs.reshape(ns, 1, LANES), m0,
      state_n.reshape(state_n.shape[0], ns, M_HEADS, 1, M_QK), state_c, norm_g)


def _layernorm_chunks(chunks, d):
    total = sum(jnp.sum(c, axis=1, keepdims=True) for c in chunks)
    mu = total * (1.0 / d)
    var = sum(jnp.sum(jnp.square(c - mu), axis=1, keepdims=True) for c in chunks) * (1.0 / d)
    return mu, lax.rsqrt(var + LN_EPS)


def _merge_kernel(alpha, *refs):
    att_ref, mix_ref, wa_ref, wb_ref, x_ref, g_ref, b_ref, wr_ref, br_ref = refs[:9]
    h_ref, ids_ref, wts_ref = refs[-3:]
    j = pl.program_id(1)
    tn = x_ref.shape[1]
    d = h_ref.shape[1]
    nj = d // tn
    cols = lambda c: slice(c * tn, (c + 1) * tn)
    pre = alpha * x_ref[...] + _dot(att_ref[...], wa_ref[...]) + _dot(mix_ref[...], wb_ref[...])
    for c in range(nj):
        @pl.when(j == c)
        def _(c=c):
            h_ref[:, cols(c)] = pre

    @pl.when(j == nj - 1)
    def _():
        mu, rstd = _layernorm_chunks([h_ref[:, cols(c)] for c in range(nj)], d)
        lg = br_ref[...]
        for c in range(nj):
            y = (h_ref[:, cols(c)] - mu) * rstd * g_ref[:, cols(c)] + b_ref[:, cols(c)]
            h_ref[:, cols(c)] = y
            lg = lg + _dot(y.astype(BF16), wr_ref[cols(c), :])
        lane = lax.broadcasted_iota(I32, lg.shape, 1)
        lane_f = lane.astype(F32)
        far = float(LANES)
        gl = jnp.where(lane < N_GROUPS, lg, -jnp.inf)
        gmax = jnp.max(gl, axis=1, keepdims=True)
        gsel = jnp.min(jnp.where(gl == gmax, lane_f, far), axis=1, keepdims=True).astype(I32)
        gw = 1.0 / jnp.sum(jnp.where(lane < N_GROUPS, jnp.exp(gl - gmax), 0.0), axis=1, keepdims=True)
        e_lane = lane - N_GROUPS
        in_grp = (e_lane >= 0) & (e_lane < N_EXPERTS) & ((e_lane // EXPERTS_PER_GROUP) == gsel)
        el = jnp.where(in_grp, lg, -jnp.inf)
        t1 = jnp.max(el, axis=1, keepdims=True)
        i1 = jnp.min(jnp.where(el == t1, lane_f, far), axis=1, keepdims=True)
        el2 = jnp.where(lane_f == i1, -jnp.inf, el)
        t2 = jnp.max(el2, axis=1, keepdims=True)
        i2 = jnp.min(jnp.where(el2 == t2, lane_f, far), axis=1, keepdims=True)
        e21 = jnp.exp(t2 - t1)
        w1 = gw / (1.0 + e21)
        w2 = w1 * e21
        ids = jnp.where(lane == 0, i1, jnp.where(lane == 1, i2, float(N_GROUPS))) - float(N_GROUPS)
        ids_ref[...] = ids.astype(I32)
        wts_ref[...] = jnp.where(lane == 0, w1, jnp.where(lane == 1, w2, 0.0))


def _merge(att, mix, wob, x, ln_g, ln_b, wr, br, alpha, n_total, row0, prev=None):
    r, d = x.shape
    half = att.shape[1]
    tm = _tile(r, (512, 256, 128, 64))
    assert row0 % tm == 0
    o0 = row0 // tm
    tn = _tile(d, (1024, 512))
    rows = lambda w: pl.BlockSpec((1, w), lambda i, j: (0, 0))
    prev = () if prev is None else tuple(prev)
    n_in = 9
    return pl.pallas_call(
        functools.partial(_merge_kernel, alpha),
        out_shape=(jax.ShapeDtypeStruct((n_total, d), F32),
                   jax.ShapeDtypeStruct((n_total, LANES), I32),
                   jax.ShapeDtypeStruct((n_total, LANES), F32)),
        grid=(r // tm, d // tn),
        in_specs=[pl.BlockSpec((tm, half), lambda i, j: (i, 0)),
                  pl.BlockSpec((tm, half), lambda i, j: (i, 0)),
                  pl.BlockSpec((half, tn), lambda i, j: (0, j)),
                  pl.BlockSpec((half, tn), lambda i, j: (1, j)),
                  pl.BlockSpec((tm, tn), lambda i, j: (i, j)),
                  rows(d), rows(d),
                  pl.BlockSpec((d, LANES), lambda i, j: (0, 0)),
                  rows(LANES)] + [pl.BlockSpec(memory_space=pl.ANY)] * len(prev),
        out_specs=(pl.BlockSpec((tm, d), lambda i, j: (i + o0, 0)),
                   pl.BlockSpec((tm, LANES), lambda i, j: (i + o0, 0)),
                   pl.BlockSpec((tm, LANES), lambda i, j: (i + o0, 0))),
        input_output_aliases={n_in + k: k for k in range(len(prev))},
        compiler_params=_params(("parallel", "arbitrary"), 52),
        name="merge_ln_router",
    )(att, mix, wob, wob, x, ln_g, ln_b, wr, br, *prev)


def _moe_kernel(te_ref, ti_ref, nu_ref, src_ref, h_ref, wg_ref, wu_ref, wd_ref, o_ref,
                xf_s, xb_s, hg_s, hu_s, a_s, sem):
    del te_ref, ti_ref
    t = pl.program_id(0)
    ph = pl.program_id(1)
    nk, tm, kc = xb_s.shape
    n_phase = nk + xf_s.shape[2] // o_ref.shape[1]
    share = tm // n_phase
    assert share * n_phase == tm
    nu = nu_ref[0]

    def start_rows(tile, lo, count):
        slot = tile % 2

        def body(g, carry):
            base = pl.multiple_of(lo + g * SUBLANES, SUBLANES)
            for u in range(SUBLANES):
                pltpu.make_async_copy(h_ref.at[pl.ds(src_ref[tile * tm + base + u], 1), :],
                                      xf_s.at[slot, pl.ds(base + u, 1), :],
                                      sem.at[slot]).start(priority=GATHER_PRIORITY)
            return carry
        lax.fori_loop(0, count // SUBLANES, body, 0)

    def wait_tile(tile):
        slot = tile % 2
        pltpu.make_async_copy(h_ref.at[pl.ds(0, tm), :], xf_s.at[slot], sem.at[slot]).wait()

    @pl.when(t < nu)
    def _():
        @pl.when(t + 1 < nu)
        def _():
            start_rows(t + 1, ph * share, share)

        @pl.when(ph == 0)
        def _():
            @pl.when(t == 0)
            def _():
                start_rows(t, 0, tm)

            wait_tile(t)
            for c in range(nk):
                xb_s[c] = xf_s[t % 2, :, c * kc:(c + 1) * kc].astype(BF16)

        @pl.when(ph < nk)
        def _():
            x = xb_s[ph]
            pg = _dot(x, wg_ref[0, 0].astype(BF16))
            pu = _dot(x, wu_ref[0, 0].astype(BF16))

            @pl.when(ph == 0)
            def _():
                hg_s[...] = pg
                hu_s[...] = pu

            @pl.when(ph != 0)
            def _():
                hg_s[...] += pg
                hu_s[...] += pu

        @pl.when(ph == nk - 1)
        def _():
            hg = hg_s[...]
            a_s[...] = (hg * jax.nn.sigmoid(hg) * hu_s[...]).astype(BF16)

        @pl.when(ph >= nk)
        def _():
            o_ref[...] = _dot(a_s[...], wd_ref[0, 0].astype(BF16))


def _moe_experts(h, src, w_gate, w_up, w_down, tile_expert, tile_index, n_used, layer):
    tm = MOE_TILE
    n_tiles = tile_expert.shape[0]
    d = w_gate.shape[2]
    f = w_gate.shape[3]
    kc = nc = MOE_CHUNK
    nk = d // kc
    nn = d // nc

    def used(t, nu, a, b):
        return jnp.where(t < nu[0], a, b)

    def k_idx(t, ph, nu):
        return used(t, nu, jnp.minimum(ph, nk - 1), nk - 1)

    def n_idx(t, ph, nu):
        return used(t, nu, jnp.maximum(ph - nk, 0), nn - 1)

    grid_spec = pltpu.PrefetchScalarGridSpec(
        num_scalar_prefetch=4, grid=(n_tiles, nk + nn),
        in_specs=[pl.BlockSpec(memory_space=pl.ANY),
                  pl.BlockSpec((1, 1, kc, f), lambda t, ph, te, ti, nu, src: (layer, te[t], k_idx(t, ph, nu), 0)),
                  pl.BlockSpec((1, 1, kc, f), lambda t, ph, te, ti, nu, src: (layer, te[t], k_idx(t, ph, nu), 0)),
                  pl.BlockSpec((1, 1, f, nc), lambda t, ph, te, ti, nu, src: (layer, te[t], 0, n_idx(t, ph, nu)))],
        out_specs=pl.BlockSpec((tm, nc), lambda t, ph, te, ti, nu, src: (ti[t], n_idx(t, ph, nu))),
        scratch_shapes=[pltpu.VMEM((2, tm, d), F32), pltpu.VMEM((nk, tm, kc), BF16),
                        pltpu.VMEM((tm, f), F32), pltpu.VMEM((tm, f), F32), pltpu.VMEM((tm, f), BF16),
                        pltpu.SemaphoreType.DMA((2,))])
    return pl.pallas_call(
        _moe_kernel,
        out_shape=jax.ShapeDtypeStruct((n_tiles * tm, d), F32),
        grid_spec=grid_spec,
        compiler_params=_params(("arbitrary", "arbitrary"), 52),
        name="moe_experts",
    )(tile_expert, tile_index, n_used, src, h, w_gate, w_up, w_down)


def _combine_kernel(alpha, row0, n_tok, pos_ref, h_ref, w_ref, g_ref, b_ref, ys_ref, o_ref, f_s, sem):
    i = pl.program_id(0)
    tm, d = o_ref.shape

    def start_tile(tile):
        slot = tile % 2

        def body(g, carry):
            base = pl.multiple_of(g * SUBLANES, SUBLANES)
            for u in range(SUBLANES):
                for s in range(2):
                    pltpu.make_async_copy(
                        ys_ref.at[pl.ds(pos_ref[s * n_tok + row0 + tile * tm + base + u], 1), :],
                        f_s.at[slot, s, pl.ds(base + u, 1), :], sem.at[slot]).start(priority=GATHER_PRIORITY)
            return carry
        lax.fori_loop(0, tm // SUBLANES, body, 0)

    def wait_tile(tile):
        slot = tile % 2
        for s in range(2):
            pltpu.make_async_copy(ys_ref.at[pl.ds(0, tm), :], f_s.at[slot, s], sem.at[slot]).wait()

    @pl.when(i == 0)
    def _():
        start_tile(i)

    @pl.when(i + 1 < pl.num_programs(0))
    def _():
        start_tile(i + 1)

    wait_tile(i)
    w = w_ref[...]
    v = alpha * h_ref[...] + w[:, 0:1] * f_s[i % 2, 0] + w[:, 1:2] * f_s[i % 2, 1]
    mu = jnp.mean(v, axis=1, keepdims=True)
    vc = v - mu
    var = jnp.sum(vc * vc, axis=1, keepdims=True) * (1.0 / d)
    o_ref[...] = vc * lax.rsqrt(var + LN_EPS) * g_ref[...] + b_ref[...]


def _combine(h, ys, pos, wts, ln_g, ln_b, alpha, row0, nrows):
    n, d = h.shape
    tm = next(c for c in (128, 64, 32, 16, 8) if nrows % c == 0 and row0 % c == 0)
    o0 = row0 // tm
    rows = pl.BlockSpec((1, d), lambda i, pos: (0, 0))
    grid_spec = pltpu.PrefetchScalarGridSpec(
        num_scalar_prefetch=1, grid=(nrows // tm,),
        in_specs=[pl.BlockSpec((tm, d), lambda i, pos: (i + o0, 0)),
                  pl.BlockSpec((tm, LANES), lambda i, pos: (i + o0, 0)), rows, rows,
                  pl.BlockSpec(memory_space=pl.ANY)],
        out_specs=pl.BlockSpec((tm, d), lambda i, pos: (i, 0)),
        scratch_shapes=[pltpu.VMEM((2, 2, tm, d), F32), pltpu.SemaphoreType.DMA((2,))])
    return pl.pallas_call(
        functools.partial(_combine_kernel, alpha, row0, n),
        out_shape=jax.ShapeDtypeStruct((nrows, d), F32),
        grid_spec=grid_spec,
        compiler_params=_params(("arbitrary",), 32),
        name="combine_ln",
    )(pos, h, wts, ln_g, ln_b, ys)


def _routing_tables(ids, n_tiles):
    n = ids.shape[0]
    tm = MOE_TILE
    e = ids[:, :2].reshape(-1)
    onehot = (e[:, None] == jnp.arange(N_EXPERTS, dtype=I32)[None, :]).astype(I32)
    csum = jnp.cumsum(onehot, axis=0)
    rank = jnp.sum((csum - onehot) * onehot, axis=1)
    counts = csum[-1]
    tiles = (counts + tm - 1) // tm
    tile_end = jnp.cumsum(tiles)
    n_used = tile_end[-1]
    pos = ((tile_end - tiles)[e] * tm + rank).astype(I32)
    src = (jnp.arange(n_tiles * tm, dtype=I32) % n).at[pos].set(jnp.arange(2 * n, dtype=I32) // 2)
    t = jnp.minimum(jnp.arange(n_tiles, dtype=I32), n_used - 1)
    tile_expert = jnp.minimum(jnp.sum((tile_end[None, :] <= t[:, None]).astype(I32), axis=1), N_EXPERTS - 1)
    pos2 = pos.reshape(n, 2)
    back = jnp.concatenate([pos2[:, 0], pos2[:, 1]])
    return src, back, tile_expert, t, n_used.reshape(1).astype(I32)


def _layer(l, x_p, x_s, bsz, seq, cache_k, cache_v, state_c, state_n, state_m, page_table,
           w_in, b_gates, norm_g, w_out, ln1_g, ln1_b, w_group, b_group, w_router, b_router,
           w_e_gate, w_e_up, w_e_down, ln2_g, ln2_b, alpha):
    n_p, d = x_p.shape
    n_s = x_s.shape[0]
    n = n_p + n_s

    wt = jnp.transpose(w_in[l]).astype(BF16)
    wg = jnp.pad(w_in[l, :, COL_G:], ((0, 0), (0, LANES - 2 * M_HEADS)))
    bg = jnp.pad(b_gates[l], (0, LANES - 2 * M_HEADS)).reshape(1, LANES)
    z_p = _in_proj(x_p.astype(BF16), wt, COL_G)
    z_s = _in_proj(x_s.astype(BF16), wt, COL_G)
    g_p = _gates(x_p, wg, bg)
    g_s = _gates(x_s, wg, bg)

    ng = norm_g[l].reshape(1, M_V_W)
    att_p = _moba_prompt(z_p, bsz, seq)
    mix_p, c_p, n_pr, m_p = _mlstm_prompt(z_p, g_p, ng, bsz, seq)
    att_s = _moba_sample(z_s, cache_k, cache_v, page_table, l)
    mix_s, c_s, n_sm, m_s = _mlstm_sample(z_s, g_s, state_c, state_n, state_m, ng, l)

    wr = jnp.pad(jnp.concatenate([w_group[l], w_router[l]], axis=1),
                 ((0, 0), (0, LANES - N_GROUPS - N_EXPERTS))).astype(BF16)
    br = jnp.pad(jnp.concatenate([b_group[l], b_router[l]]), (0, LANES - N_GROUPS - N_EXPERTS)).reshape(1, LANES)
    wob = w_out[l].astype(BF16)
    g1, b1 = ln1_g[l].reshape(1, d), ln1_b[l].reshape(1, d)
    done = _merge(att_p, mix_p, wob, x_p, g1, b1, wr, br, alpha, n, 0)
    h2, ids, wts = _merge(att_s, mix_s.reshape(n_s, M_V_W), wob, x_s, g1, b1, wr, br, alpha, n, n_p, prev=done)

    n_tiles = (2 * n) // MOE_TILE + N_EXPERTS + 1
    src, back, tile_expert, tile_index, n_used = _routing_tables(ids, n_tiles)
    ys = _moe_experts(h2, src, w_e_gate, w_e_up, w_e_down, tile_expert, tile_index, n_used, l)
    g2, b2 = ln2_g[l].reshape(1, d), ln2_b[l].reshape(1, d)
    y_p = _combine(h2, ys, back, wts, g2, b2, alpha, 0, n_p)
    y_s = _combine(h2, ys, back, wts, g2, b2, alpha, n_p, n_s)

    kv = lambda rows, col: rows[:, col:col + ATT_KV_W]
    return (y_p, y_s,
            kv(z_p, COL_K).reshape(bsz, seq, ATT_KV_HEADS, HEAD_DIM),
            kv(z_p, COL_V).reshape(bsz, seq, ATT_KV_HEADS, HEAD_DIM),
            kv(z_s, COL_K).reshape(n_s, 1, ATT_KV_HEADS, HEAD_DIM),
            kv(z_s, COL_V).reshape(n_s, 1, ATT_KV_HEADS, HEAD_DIM),
            c_p, n_pr.reshape(bsz, M_HEADS, M_QK), m_p[:, :, 0, 0],
            c_s, n_sm.reshape(n_s, M_HEADS, M_QK), m_s[:, 0, :M_HEADS])


def kernel(x_prompt, x_sample, cache_k, cache_v, state_mlstm_c, state_mlstm_n, state_mlstm_m, page_table,
           w_in, b_gates, mlstm_norm_g, w_out, ln1_g, ln1_b, w_group, b_group, w_router, b_router,
           w_e_gate, w_e_up, w_e_down, ln2_g, ln2_b):
    bsz, seq, d = x_prompt.shape
    n_s, dec_seq, _ = x_sample.shape
    depth = w_in.shape[0]
    assert dec_seq == 1 and seq % MOBA_BLOCK == 0 and w_in.shape[2] == COL_G + 2 * M_HEADS
    assert (page_table.shape[1] * cache_k.shape[2]) % MOBA_BLOCK == 0 and MOBA_BLOCK % cache_k.shape[2] == 0
    alpha = (2 * depth) ** 0.25
    x_p = x_prompt.reshape(bsz * seq, d)
    x_s = x_sample.reshape(n_s, d)
    per_layer = []
    for l in range(depth):
        outs = _layer(l, x_p, x_s, bsz, seq, cache_k, cache_v, state_mlstm_c, state_mlstm_n, state_mlstm_m,
                      page_table, w_in, b_gates, mlstm_norm_g, w_out, ln1_g, ln1_b, w_group, b_group, w_router,
                      b_router, w_e_gate, w_e_up, w_e_down, ln2_g, ln2_b, alpha)
        x_p, x_s = outs[0], outs[1]
        per_layer.append(outs[2:])
    stacked = [jnp.stack([p[i] for p in per_layer]) for i in range(10)]
    return (x_p.reshape(bsz, seq, d), x_s.reshape(n_s, 1, d), *stacked)
```
